```python
import jax
import jax.numpy as jnp
from jax import lax
import numpy as np

D_MODEL = 2048
BATCH = 4
SEQ = 2048
DEPTH = 4

GRID_W = 64
CTX_LEN = 256
N_MIXERS = 2
N_LRU_LAYERS = (DEPTH + N_MIXERS - 1) // N_MIXERS
N_ATTN_LAYERS = DEPTH // N_MIXERS
DN_ALPHA = (2.0 * DEPTH) ** 0.25
DN_BETA = (8.0 * DEPTH) ** -0.25
LN_EPS = 1e-5

LRU_WIDTH = D_MODEL
LRU_BLOCK_W = 256
LRU_BLOCKS = LRU_WIDTH // LRU_BLOCK_W
CONV_W = 4
CONV_LEFT = 1
LRU_C = 8.0

HEAD_DIM = 64
N_HEADS = D_MODEL // HEAD_DIM
N_KV_HEADS = N_HEADS // 8
GROUP = N_HEADS // N_KV_HEADS
Q_DIM = N_HEADS * HEAD_DIM
KV_DIM = N_KV_HEADS * HEAD_DIM
WINDOW = 128
Q_BLOCK = WINDOW
ROPE_BASE = 10000.0
NEG_INF = -1e30

N_EXPERTS = 32
TOP_K = 4
D_EXPERT = 768
SWIGLU_LIMIT = 7.0
SWIGLU_ALPHA = 1.702
EXPERT_BLOCK = 128

kernel_name = 'hybrid_rglru_swa_moe_diffusion_trunk'


def layer_norm(x, g, b):
    xf = x.astype(jnp.float32)
    mu = jnp.mean(xf, axis=-1, keepdims=True)
    var = jnp.mean(jnp.square(xf - mu), axis=-1, keepdims=True)
    return ((xf - mu) * lax.rsqrt(var + LN_EPS) * g + b).astype(x.dtype)


def rope_axis(x, pos):
    n = x.shape[-1]
    freqs = ROPE_BASE ** (-jnp.arange(0, n, 2, dtype=jnp.float32) / n)
    ang = pos[:, None] * freqs[None, :]
    cos = jnp.cos(ang)[None, :, None, :].astype(x.dtype)
    sin = jnp.sin(ang)[None, :, None, :].astype(x.dtype)
    x1, x2 = x[..., : n // 2], x[..., n // 2:]
    return jnp.concatenate([x1 * cos - x2 * sin, x1 * sin + x2 * cos], axis=-1)


def rope_2d(x, row_pos, col_pos):
    half = x.shape[-1] // 2
    return jnp.concatenate([rope_axis(x[..., :half], row_pos),
                            rope_axis(x[..., half:], col_pos)], axis=-1)


def centred_depthwise_conv(u, w, b):
    L = u.shape[1]
    up = jnp.pad(u, ((0, 0), (CONV_LEFT, CONV_W - 1 - CONV_LEFT), (0, 0)))
    out = b
    for k in range(CONV_W):
        out = out + up[:, k:k + L] * w[k]
    return out


def lru_coefficients(u, gate_w, gate_b, lam):
    B, L, _ = u.shape
    uf = u.astype(jnp.float32)
    ub = uf.reshape(B, L, LRU_BLOCKS, LRU_BLOCK_W)
    g = jnp.einsum('blnk,gnkj->gblnj', ub, gate_w.astype(jnp.float32)).reshape(2, B, L, LRU_WIDTH)
    g = g + gate_b.astype(jnp.float32)[:, None, None, :]
    r = jax.nn.sigmoid(g[0])
    i = jax.nn.sigmoid(g[1])
    log_a = -LRU_C * r * jax.nn.softplus(-lam.astype(jnp.float32))
    a = jnp.exp(log_a)
    b = jnp.sqrt(jnp.maximum(1.0 - jnp.exp(2.0 * log_a), 0.0)) * (i * uf)
    return a, b


def _combine(p, q):
    a1, b1 = p
    a2, b2 = q
    return a1 * a2, a2 * b1 + b2


def linear_scan(a, b, h0):
    a_cum, b_cum = lax.associative_scan(_combine, (a, b), axis=1)
    return a_cum * h0[:, None, :] + b_cum


def rglru_mixer(h_lat, h_ctx, w_in, conv_w, conv_b, gate_w, gate_b, lam, w_out, need_ctx):
    u_lat = h_lat @ w_in
    u_ctx = h_ctx @ w_in
    r_lat = centred_depthwise_conv(u_lat[..., LRU_WIDTH:], conv_w, conv_b)
    r_ctx = centred_depthwise_conv(u_ctx[..., LRU_WIDTH:], conv_w, conv_b)
    B = h_lat.shape[0]
    h0 = jnp.zeros((B, LRU_WIDTH), jnp.float32)
    y_lat = jnp.zeros(r_lat.shape, jnp.float32)
    y_ctx = jnp.zeros(r_ctx.shape, jnp.float32)
    for d in range(2):
        a_c, b_c = lru_coefficients(r_ctx, gate_w[d], gate_b[d], lam[d])
        a_l, b_l = lru_coefficients(r_lat, gate_w[d], gate_b[d], lam[d])
        if d == 1:
            a_c, b_c, a_l, b_l = [jnp.flip(t, axis=1) for t in (a_c, b_c, a_l, b_l)]
        s_c = linear_scan(a_c, b_c, h0)
        s_l = linear_scan(a_l, b_l, s_c[:, -1])
        if d == 1:
            s_c = jnp.flip(s_c, axis=1)
            s_l = jnp.flip(s_l, axis=1)
        y_lat = y_lat + s_l
        y_ctx = y_ctx + s_c
    out_lat = (y_lat.astype(h_lat.dtype) * jax.nn.gelu(u_lat[..., :LRU_WIDTH])) @ w_out
    out_ctx = None
    if need_ctx:
        out_ctx = (y_ctx.astype(h_ctx.dtype) * jax.nn.gelu(u_ctx[..., :LRU_WIDTH])) @ w_out
    return out_lat, out_ctx


def attend(q, k, v, mask, sink):
    s = jnp.einsum('bqhgd,bkhd->bhgqk', q, k).astype(jnp.float32)
    if mask is not None:
        s = jnp.where(mask, s, NEG_INF)
    sk = jnp.broadcast_to(sink.astype(jnp.float32)[None, :, :, None, None], s.shape[:-1] + (1,))
    p = jax.nn.softmax(jnp.concatenate([s, sk], axis=-1), axis=-1)[..., :-1]
    return jnp.einsum('bhgqk,bkhd->bqhgd', p.astype(v.dtype), v)


def attention_mixer(h_lat, h_ctx, w_qkv, sinks, w_o, row_pos, col_pos, need_ctx):
    B, S, _ = h_lat.shape
    C = h_ctx.shape[1]
    scale = HEAD_DIM ** -0.5
    sink_g = sinks.reshape(N_KV_HEADS, GROUP)
    qkv = h_lat @ w_qkv
    q_l = qkv[..., :Q_DIM].reshape(B, S, N_HEADS, HEAD_DIM)
    k_l = qkv[..., Q_DIM:Q_DIM + KV_DIM].reshape(B, S, N_KV_HEADS, HEAD_DIM)
    v_l = qkv[..., Q_DIM + KV_DIM:].reshape(B, S, N_KV_HEADS, HEAD_DIM)
    q_l = (rope_2d(q_l, row_pos, col_pos) * scale).reshape(B, S, N_KV_HEADS, GROUP, HEAD_DIM)
    k_l = rope_2d(k_l, row_pos, col_pos)
    kv_c = h_ctx @ w_qkv[:, Q_DIM:]
    k_c = kv_c[..., :KV_DIM].reshape(B, C, N_KV_HEADS, HEAD_DIM)
    v_c = kv_c[..., KV_DIM:].reshape(B, C, N_KV_HEADS, HEAD_DIM)

    nb = S // Q_BLOCK
    qb = jnp.moveaxis(q_l.reshape(B, nb, Q_BLOCK, N_KV_HEADS, GROUP, HEAD_DIM), 1, 0)

    def band(t):
        tp = jnp.pad(t, ((0, 0), (Q_BLOCK, Q_BLOCK), (0, 0), (0, 0)))
        tp = tp.reshape(B, nb + 2, Q_BLOCK, N_KV_HEADS, HEAD_DIM)
        tb = jnp.concatenate([tp[:, :-2], tp[:, 1:-1], tp[:, 2:]], axis=2)
        return jnp.moveaxis(tb, 1, 0)

    kb, vb = band(k_l), band(v_l)
    ctx_mask = jnp.ones((Q_BLOCK, C), dtype=bool)

    def block(args):
        q_j, k_j, v_j, j = args
        qpos = j * Q_BLOCK + jnp.arange(Q_BLOCK)
        kpos = (j - 1) * Q_BLOCK + jnp.arange(3 * Q_BLOCK)
        in_band = ((jnp.abs(qpos[:, None] - kpos[None, :]) <= WINDOW)
                   & (kpos >= 0)[None, :] & (kpos < S)[None, :])
        mask = jnp.concatenate([in_band, ctx_mask], axis=1)
        return attend(q_j, jnp.concatenate([k_j, k_c], axis=1),
                      jnp.concatenate([v_j, v_c], axis=1), mask, sink_g)

    o = lax.map(block, (qb, kb, vb, jnp.arange(nb)))
    out_lat = jnp.moveaxis(o, 0, 1).reshape(B, S, Q_DIM) @ w_o
    out_ctx = None
    if need_ctx:
        q_c = (h_ctx @ w_qkv[:, :Q_DIM]).reshape(B, C, N_KV_HEADS, GROUP, HEAD_DIM) * scale
        out_ctx = attend(q_c, k_c, v_c, None, sink_g).reshape(B, C, Q_DIM) @ w_o
    return out_lat, out_ctx


def moe_ffn(t, w_r, b_r, w_gu, b_gu, w_dn, b_dn):
    T, D = t.shape
    logits = (t @ w_r + b_r).astype(jnp.float32)
    top_v, top_e = lax.top_k(logits, TOP_K)
    gate = jax.nn.softmax(top_v, axis=-1)
    A = T * TOP_K
    flat_e = top_e.reshape(-1)
    flat_tok = jnp.arange(A) // TOP_K
    flat_g = gate.reshape(-1)
    order = jnp.argsort(flat_e)
    se, stok, sg = flat_e[order], flat_tok[order], flat_g[order]
    counts = jnp.bincount(flat_e, length=N_EXPERTS)
    padded = ((counts + EXPERT_BLOCK - 1) // EXPERT_BLOCK) * EXPERT_BLOCK
    start = jnp.cumsum(counts) - counts
    ends = jnp.cumsum(padded)
    pstart = ends - padded
    dest = pstart[se] + (jnp.arange(A) - start[se])
    P = (-(-A // EXPERT_BLOCK)) * EXPERT_BLOCK + N_EXPERTS * EXPERT_BLOCK
    n_blk = P // EXPERT_BLOCK
    row_tok = jnp.full((P,), T, dtype=jnp.int32).at[dest].set(stok.astype(jnp.int32))
    row_g = jnp.zeros((P,), jnp.float32).at[dest].set(sg)
    blk_e = jnp.minimum(jnp.searchsorted(ends, jnp.arange(n_blk) * EXPERT_BLOCK, side='right'),
                        N_EXPERTS - 1)
    t_pad = jnp.concatenate([t, jnp.zeros((1, D), t.dtype)], axis=0)
    xs = t_pad[row_tok].reshape(n_blk, EXPERT_BLOCK, D)

    def expert_block(args):
        xb, e = args
        h = xb @ w_gu[e] + b_gu[e]
        g = jnp.minimum(h[:, :D_EXPERT], SWIGLU_LIMIT)
        u = jnp.clip(h[:, D_EXPERT:], -SWIGLU_LIMIT, SWIGLU_LIMIT)
        y = (u + 1.0) * (g * jax.nn.sigmoid(SWIGLU_ALPHA * g))
        return y @ w_dn[e] + b_dn[e]

    out = lax.map(expert_block, (xs, blk_e)).reshape(P, D)
    out = out * row_g[:, None].astype(out.dtype)
    y = jnp.zeros((T + 1, D), out.dtype).at[row_tok].add(out)
    return y[:T]


def setup_inputs(seed: int = 0) -> dict:
    key = jax.random.key(seed)
    ks = jax.random.split(key, 24)
    D = D_MODEL
    f32 = jnp.float32

    def nrm(k, shape, s):
        return jax.random.normal(k, shape, f32) * s

    u = jax.random.uniform(ks[13], (N_LRU_LAYERS, 2, LRU_WIDTH), f32, 0.9, 0.999)
    a_base = u ** (1.0 / LRU_C)
    lru_lambda = jnp.log(a_base) - jnp.log1p(-a_base)
    return {
        'x': nrm(ks[0], (BATCH, SEQ, D), 1.0),
        'c': nrm(ks[1], (BATCH, D), 1.0),
        'ctx': nrm(ks[2], (BATCH, CTX_LEN, D), 1.0),
        'c_ctx': nrm(ks[3], (D,), 1.0),
        'ada_w': nrm(ks[4], (DEPTH, 2, D, 3 * D), 0.5 * D ** -0.5),
        'ada_b': nrm(ks[5], (DEPTH, 2, 3 * D), 0.02),
        'ln_g': 1.0 + nrm(ks[6], (DEPTH, 2, D), 0.02),
        'ln_b': nrm(ks[7], (DEPTH, 2, D), 0.02),
        'lru_w_in': nrm(ks[8], (N_LRU_LAYERS, D, 2 * LRU_WIDTH), D ** -0.5),
        'lru_conv_w': nrm(ks[9], (N_LRU_LAYERS, CONV_W, LRU_WIDTH), CONV_W ** -0.5),
        'lru_conv_b': nrm(ks[10], (N_LRU_LAYERS, LRU_WIDTH), 0.02),
        'lru_gate_w': nrm(ks[11], (N_LRU_LAYERS, 2, 2, LRU_BLOCKS, LRU_BLOCK_W, LRU_BLOCK_W), LRU_BLOCK_W ** -0.5),
        'lru_gate_b': nrm(ks[12], (N_LRU_LAYERS, 2, 2, LRU_WIDTH), 0.02),
        'lru_lambda': lru_lambda,
        'lru_w_out': nrm(ks[14], (N_LRU_LAYERS, LRU_WIDTH, D), LRU_WIDTH ** -0.5 * DN_BETA),
        'attn_w_qkv': nrm(ks[15], (N_ATTN_LAYERS, D, Q_DIM + 2 * KV_DIM), D ** -0.5),
        'attn_sinks': nrm(ks[16], (N_ATTN_LAYERS, N_HEADS), 0.5),
        'attn_w_o': nrm(ks[17], (N_ATTN_LAYERS, Q_DIM, D), Q_DIM ** -0.5 * DN_BETA),
        'router_w': nrm(ks[18], (DEPTH, D, N_EXPERTS), D ** -0.5),
        'router_b': nrm(ks[19], (DEPTH, N_EXPERTS), 0.01),
        'moe_w_gu': nrm(ks[20], (DEPTH, N_EXPERTS, D, 2 * D_EXPERT), D ** -0.5),
        'moe_b_gu': nrm(ks[21], (DEPTH, N_EXPERTS, 2 * D_EXPERT), 0.01),
        'moe_w_down': nrm(ks[22], (DEPTH, N_EXPERTS, D_EXPERT, D), D_EXPERT ** -0.5 * DN_BETA),
        'moe_b_down': nrm(ks[23], (DEPTH, N_EXPERTS, D), 0.01),
    }


def reference(x, c, ctx, c_ctx, ada_w, ada_b, ln_g, ln_b, lru_w_in, lru_conv_w, lru_conv_b,
              lru_gate_w, lru_gate_b, lru_lambda, lru_w_out, attn_w_qkv, attn_sinks, attn_w_o,
              router_w, router_b, moe_w_gu, moe_b_gu, moe_w_down, moe_b_down):
    B, S, D = x.shape
    C = ctx.shape[1]
    rows = S // GRID_W
    row_pos = jnp.repeat(jnp.arange(rows), GRID_W).astype(jnp.float32)
    col_pos = jnp.tile(jnp.arange(GRID_W), rows).astype(jnp.float32)
    c_act = jax.nn.silu(c)
    cc_act = jax.nn.silu(c_ctx)
    h_ctx = ctx
    for i in range(DEPTH):
        need_ctx = i < DEPTH - 1
        m_lat = (jnp.einsum('bd,sde->sbe', c_act, ada_w[i]) + ada_b[i][:, None, :])[:, :, None, :]
        m_ctx = (jnp.einsum('d,sde->se', cc_act, ada_w[i]) + ada_b[i])[:, None, None, :]

        sh_l, sc_l, g_l = jnp.split(m_lat[0], 3, axis=-1)
        sh_c, sc_c, g_c = jnp.split(m_ctx[0], 3, axis=-1)
        a_lat = x * (1.0 + sc_l) + sh_l
        a_ctx = h_ctx * (1.0 + sc_c) + sh_c
        j = i // N_MIXERS
        if i % N_MIXERS == 0:
            y_lat, y_ctx = rglru_mixer(a_lat, a_ctx, lru_w_in[j], lru_conv_w[j], lru_conv_b[j],
                                       lru_gate_w[j], lru_gate_b[j], lru_lambda[j], lru_w_out[j],
                                       need_ctx)
        else:
            y_lat, y_ctx = attention_mixer(a_lat, a_ctx, attn_w_qkv[j], attn_sinks[j], attn_w_o[j],
                                           row_pos, col_pos, need_ctx)
        x = layer_norm(DN_ALPHA * x + g_l * y_lat, ln_g[i, 0], ln_b[i, 0])
        if need_ctx:
            h_ctx = layer_norm(DN_ALPHA * h_ctx + g_c * y_ctx, ln_g[i, 0], ln_b[i, 0])

        sh_l, sc_l, g_l = jnp.split(m_lat[1], 3, axis=-1)
        sh_c, sc_c, g_c = jnp.split(m_ctx[1], 3, axis=-1)
        a_lat = (x * (1.0 + sc_l) + sh_l).reshape(B * S, D)
        if need_ctx:
            a_ctx = (h_ctx * (1.0 + sc_c) + sh_c).reshape(B * C, D)
            tokens = jnp.concatenate([a_lat, a_ctx], axis=0)
        else:
            tokens = a_lat
        y = moe_ffn(tokens, router_w[i], router_b[i], moe_w_gu[i], moe_b_gu[i],
                    moe_w_down[i], moe_b_down[i])
        x = layer_norm(DN_ALPHA * x + g_l * y[:B * S].reshape(B, S, D), ln_g[i, 1], ln_b[i, 1])
        if need_ctx:
            h_ctx = layer_norm(DN_ALPHA * h_ctx + g_c * y[B * S:].reshape(B, C, D),
                               ln_g[i, 1], ln_b[i, 1])
    return x
```

```python
import functools
import math

import jax
import jax.numpy as jnp
from jax import lax
from jax.experimental import pallas as pl
from jax.experimental.pallas import tpu as pltpu

GRID_W = 64
LN_EPS = 1e-5
LRU_BLOCK_W = 256
CONV_W = 4
LRU_C = 8.0
HEAD_DIM = 64
GROUP = 8
WINDOW = 128
Q_BLOCK = 128
ROPE_BASE = 10000.0
NEG_INF = -1e30
TOP_K = 4
SWIGLU_LIMIT = 7.0
SWIGLU_ALPHA = 1.702

LANES = 128
SUBLANES = 8
TM = 256
EXPERT_BM = 256
VMEM_LIMIT = 56 * 1024 * 1024

BF16 = jnp.bfloat16
F32 = jnp.float32


def _cparams(sem, vmem=None):
    return pltpu.CompilerParams(dimension_semantics=sem, vmem_limit_bytes=vmem)


def _ada_kernel(cb_ref, w_ref, b_ref, o_ref, acc_ref, *, nrows):
    k = pl.program_id(2)

    @pl.when(k == 0)
    def _():
        acc_ref[...] = jnp.zeros_like(acc_ref)

    tk, tn = w_ref.shape[-2:]
    for j in range(tn // LANES):
        w = w_ref[0, :, j * LANES:(j + 1) * LANES]
        for r in range(nrows):
            p = w * cb_ref[r]
            acc_ref[r, :, j * LANES:(j + 1) * LANES] += p.reshape(tk // SUBLANES, SUBLANES, LANES).sum(axis=0)

    @pl.when(k == pl.num_programs(2) - 1)
    def _():
        o_ref[0] = jnp.zeros(o_ref.shape[1:], F32)
        for r in range(nrows):
            o_ref[0, r:r + 1, :] = acc_ref[r].sum(axis=0, keepdims=True) + b_ref[0]


def _ada_params(act, w, bias):
    R, D = act.shape
    L, _, N = w.shape
    tk, tn = 256, 2048
    cb = jnp.broadcast_to(act[:, :, None], (R, D, LANES))
    return pl.pallas_call(
        functools.partial(_ada_kernel, nrows=R),
        out_shape=jax.ShapeDtypeStruct((L, SUBLANES, N), F32),
        grid=(L, N // tn, D // tk),
        in_specs=[
            pl.BlockSpec((R, tk, LANES), lambda l, n, k: (0, k, 0)),
            pl.BlockSpec((1, tk, tn), lambda l, n, k: (l, k, n)),
            pl.BlockSpec((1, 1, tn), lambda l, n, k: (l, 0, n)),
        ],
        out_specs=pl.BlockSpec((1, SUBLANES, tn), lambda l, n, k: (l, 0, n)),
        scratch_shapes=[pltpu.VMEM((R, SUBLANES, tn), F32)],
        compiler_params=_cparams(("parallel", "parallel", "arbitrary")),
        name="ada_params",
    )(cb, w, bias.reshape(L, 1, N))


def _mod_row(i, tiles_per_batch, ctx_tiles, nbatch):
    return jnp.where(i % tiles_per_batch < ctx_tiles, nbatch, i // tiles_per_batch)


def _rope_chunk(y, cos, sin_signed, first_half):
    rot = jnp.where(first_half, pltpu.roll(y, LANES - 16, axis=1), pltpu.roll(y, 16, axis=1))
    return y * cos + rot * sin_signed


def _modmm_kernel(x_ref, mod_ref, w_ref, *rest, rope_cols):
    if rope_cols is None:
        (o_ref,) = rest
    else:
        tab_ref, o_ref = rest
    shift = mod_ref[0:1, :]
    scale = mod_ref[1:2, :]
    a = (x_ref[...] * (1.0 + scale) + shift).astype(BF16)
    tn = o_ref.shape[1]
    ch = 512 if tn % 512 == 0 else 256
    for c in range(tn // ch):
        acc = jnp.dot(a, w_ref[:, c * ch:(c + 1) * ch].astype(BF16), preferred_element_type=F32)
        if rope_cols is None:
            o_ref[:, c * ch:(c + 1) * ch] = acc.astype(o_ref.dtype)
        else:
            q_cols, k_cols = rope_cols
            lane = lax.broadcasted_iota(jnp.int32, (acc.shape[0], LANES), 1)
            first_half = (lane % 32) < 16
            for s in range(ch // LANES):
                col = c * ch + s * LANES
                y = acc[:, s * LANES:(s + 1) * LANES]
                if col < q_cols:
                    y = _rope_chunk(y, tab_ref[0], tab_ref[1], first_half)
                elif col < q_cols + k_cols:
                    y = _rope_chunk(y, tab_ref[2], tab_ref[3], first_half)
                o_ref[:, col:col + LANES] = y.astype(o_ref.dtype)


def _mod_matmul(x, mods, ls, w, *, tn, out_dtype, geom, rope=None):
    M, K = x.shape
    N = w.shape[1]
    tpb, ctx_tiles, nbatch = geom
    in_specs = [
        pl.BlockSpec((TM, K), lambda j, i: (i, 0)),
        pl.BlockSpec((None, None, 3, K), lambda j, i: (ls, _mod_row(i, tpb, ctx_tiles, nbatch), 0, 0)),
        pl.BlockSpec((K, tn), lambda j, i: (0, j)),
    ]
    args = [x, mods, w]
    rope_cols = None
    if rope is not None:
        tab, rope_cols = rope
        in_specs.append(pl.BlockSpec((4, TM, LANES), lambda j, i: (0, i % tpb, 0)))
        args.append(tab)
    return pl.pallas_call(
        functools.partial(_modmm_kernel, rope_cols=rope_cols),
        out_shape=jax.ShapeDtypeStruct((M, N), out_dtype),
        grid=(N // tn, M // TM),
        in_specs=in_specs,
        out_specs=pl.BlockSpec((TM, tn), lambda j, i: (i, j)),
        compiler_params=_cparams(("parallel", "parallel"), VMEM_LIMIT),
        name="mod_matmul",
    )(*args)


def _layer_norm_rows(z, g, b):
    mu = jnp.mean(z, axis=-1, keepdims=True)
    zc = z - mu
    var = jnp.mean(zc * zc, axis=-1, keepdims=True)
    return zc * lax.rsqrt(var + LN_EPS) * g + b


def _proj_ln_kernel(y_ref, w_ref, x_ref, mod_ref, g_ref, b_ref, o_ref, z_ref, *, alpha):
    gate = mod_ref[2:3, :]
    y = y_ref[...]
    n = o_ref.shape[1]
    ch = 512
    for c in range(n // ch):
        sl = slice(c * ch, (c + 1) * ch)
        acc = jnp.dot(y, w_ref[:, sl].astype(BF16), preferred_element_type=F32)
        z_ref[:, sl] = alpha * x_ref[:, sl] + gate[:, sl] * acc
    o_ref[...] = _layer_norm_rows(z_ref[...], g_ref[...], b_ref[...])


def _proj_ln(y, w, x, mods, ls, ln_g, ln_b, *, alpha, geom):
    M, K = y.shape
    D = w.shape[1]
    tpb, ctx_tiles, nbatch = geom
    return pl.pallas_call(
        functools.partial(_proj_ln_kernel, alpha=alpha),
        out_shape=jax.ShapeDtypeStruct((M, D), F32),
        grid=(M // TM,),
        in_specs=[
            pl.BlockSpec((TM, K), lambda i: (i, 0)),
            pl.BlockSpec((K, D), lambda i: (0, 0)),
            pl.BlockSpec((TM, D), lambda i: (i, 0)),
            pl.BlockSpec((None, None, 3, D), lambda i: (ls, _mod_row(i, tpb, ctx_tiles, nbatch), 0, 0)),
            pl.BlockSpec((1, D), lambda i: (0, 0)),
            pl.BlockSpec((1, D), lambda i: (0, 0)),
        ],
        out_specs=pl.BlockSpec((TM, D), lambda i: (i, 0)),
        scratch_shapes=[pltpu.VMEM((TM, D), F32)],
        compiler_params=_cparams(("parallel",), VMEM_LIMIT),
        name="proj_ln",
    )(y, w, x, mods, ln_g.reshape(1, D), ln_b.reshape(1, D))


def _combine_ln_kernel(y4_ref, gate4_ref, x_ref, mod_ref, g_ref, b_ref, o_ref, *, alpha):
    gate = mod_ref[2:3, :]
    y = jnp.zeros(x_ref.shape, F32)
    for k in range(TOP_K):
        y = y + y4_ref[:, k, :] * gate4_ref[:, k:k + 1]
    z = alpha * x_ref[...] + gate * y
    o_ref[...] = _layer_norm_rows(z, g_ref[...], b_ref[...])


def _combine_ln(y4, gate4, x, mods, ls, ln_g, ln_b, *, alpha, geom):
    M, D = x.shape
    tpb, ctx_tiles, nbatch = geom
    return pl.pallas_call(
        functools.partial(_combine_ln_kernel, alpha=alpha),
        out_shape=jax.ShapeDtypeStruct((M, D), F32),
        grid=(M // TM,),
        in_specs=[
            pl.BlockSpec((TM, TOP_K, D), lambda i: (i, 0, 0)),
            pl.BlockSpec((TM, LANES), lambda i: (i, 0)),
            pl.BlockSpec((TM, D), lambda i: (i, 0)),
            pl.BlockSpec((None, None, 3, D), lambda i: (ls, _mod_row(i, tpb, ctx_tiles, nbatch), 0, 0)),
            pl.BlockSpec((1, D), lambda i: (0, 0)),
            pl.BlockSpec((1, D), lambda i: (0, 0)),
        ],
        out_specs=pl.BlockSpec((TM, D), lambda i: (i, 0)),
        compiler_params=_cparams(("parallel",), VMEM_LIMIT),
        name="combine_ln",
    )(y4, gate4, x, mods, ln_g.reshape(1, D), ln_b.reshape(1, D))


def _sigmoid(x):
    return 1.0 / (1.0 + jnp.exp(-x))


def _softplus(x):
    return jnp.maximum(x, 0.0) + jnp.log(1.0 + jnp.exp(-jnp.abs(x)))


def _gelu_tanh(x):
    c = math.sqrt(2.0 / math.pi)
    return 0.5 * x * (1.0 + jnp.tanh(c * (x + 0.044715 * (x * x * x))))


def _group_scan(a, b, row, reverse):
    for s in (1, 2, 4):
        if reverse:
            keep = row < SUBLANES - s
            shift = SUBLANES - s
        else:
            keep = row >= s
            shift = s
        a_sh = jnp.where(keep, pltpu.roll(a, shift, axis=0), 1.0)
        b_sh = jnp.where(keep, pltpu.roll(b, shift, axis=0), 0.0)
        b = b + a * b_sh
        a = a * a_sh
    return a, b


def _lru_kernel(ug_ref, ur_ref, cw_ref, cb_ref, wg_ref, bg_ref, lam_ref, o_ref,
                a0_ref, b0_ref, a1_ref, b1_ref, *, ctx_len):
    T, W = ur_ref.shape
    CH = 256
    n_chunks = T // CH
    ctx_chunks = ctx_len // CH
    row_c = lax.broadcasted_iota(jnp.int32, (CH, W), 0)
    sp = _softplus(-lam_ref[...])

    def coeff_chunk(c, carry):
        c0 = pl.multiple_of(c * CH, CH)
        cur = ur_ref[pl.ds(c0, CH), :]
        prev8 = ur_ref[pl.ds(pl.multiple_of(jnp.maximum(c0 - SUBLANES, 0), SUBLANES), SUBLANES), :]
        next8 = ur_ref[pl.ds(pl.multiple_of(jnp.minimum(c0 + CH, T - SUBLANES), SUBLANES), SUBLANES), :]
        seg_start = jnp.logical_or(c == 0, c == ctx_chunks)
        seg_end = jnp.logical_or(c == ctx_chunks - 1, c == n_chunks - 1)
        p_last = jnp.where(seg_start, 0.0, prev8[SUBLANES - 1:SUBLANES, :])
        n_first = jnp.where(seg_end, 0.0, next8[0:1, :])
        n_second = jnp.where(seg_end, 0.0, next8[1:2, :])
        u_m1 = jnp.where(row_c == 0, p_last, pltpu.roll(cur, 1, axis=0))
        u_p1 = jnp.where(row_c == CH - 1, n_first, pltpu.roll(cur, CH - 1, axis=0))
        u_p2 = jnp.where(row_c == CH - 2, n_first,
                         jnp.where(row_c == CH - 1, n_second, pltpu.roll(cur, CH - 2, axis=0)))
        r = cb_ref[...] + u_m1 * cw_ref[0:1, :]
        r = r + cur * cw_ref[1:2, :]
        r = r + u_p1 * cw_ref[2:3, :]
        r = r + u_p2 * cw_ref[3:4, :]
        gts = jnp.dot(r.astype(BF16), wg_ref[0].astype(BF16), preferred_element_type=F32) + bg_ref[0]
        for d, (a_ref, b_ref) in enumerate(((a0_ref, b0_ref), (a1_ref, b1_ref))):
            rg = _sigmoid(gts[:, (2 * d) * W:(2 * d + 1) * W])
            ig = _sigmoid(gts[:, (2 * d + 1) * W:(2 * d + 2) * W])
            log_a = -LRU_C * rg * sp[d:d + 1, :]
            a = jnp.exp(log_a)
            a_ref[pl.ds(c0, CH), :] = a
            b_ref[pl.ds(c0, CH), :] = jnp.sqrt(jnp.maximum(1.0 - a * a, 0.0)) * (ig * r)
        return carry

    lax.fori_loop(0, n_chunks, coeff_chunk, 0)

    row8 = lax.broadcasted_iota(jnp.int32, (SUBLANES, W), 0)
    n_groups = T // SUBLANES
    ctx_groups = ctx_len // SUBLANES

    def scan_step(k, carry):
        hf, hb = carry
        gf = pl.multiple_of(k * SUBLANES, SUBLANES)
        a, b = _group_scan(a0_ref[pl.ds(gf, SUBLANES), :], b0_ref[pl.ds(gf, SUBLANES), :], row8, False)
        h = a * hf + b
        a0_ref[pl.ds(gf, SUBLANES), :] = h
        hf = jnp.broadcast_to(h[SUBLANES - 1:SUBLANES, :], h.shape)
        g = jnp.where(k < ctx_groups, ctx_groups - 1 - k, n_groups - 1 - (k - ctx_groups))
        gb = pl.multiple_of(g * SUBLANES, SUBLANES)
        a, b = _group_scan(a1_ref[pl.ds(gb, SUBLANES), :], b1_ref[pl.ds(gb, SUBLANES), :], row8, True)
        h = a * hb + b
        a1_ref[pl.ds(gb, SUBLANES), :] = h
        hb = jnp.broadcast_to(h[0:1, :], h.shape)
        return hf, hb

    zero = jnp.zeros((SUBLANES, W), F32)
    lax.fori_loop(0, n_groups, scan_step, (zero, zero), unroll=4)

    def out_chunk(c, carry):
        c0 = pl.multiple_of(c * CH, CH)
        y = a0_ref[pl.ds(c0, CH), :] + a1_ref[pl.ds(c0, CH), :]
        o_ref[pl.ds(c0, CH), :] = (y * _gelu_tanh(ug_ref[pl.ds(c0, CH), :])).astype(o_ref.dtype)
        return carry

    lax.fori_loop(0, n_chunks, out_chunk, 0)


def _lru_mix(u, conv_w, conv_b, gate_w, gate_b, lam, *, nbatch, ctx_len):
    M, W2 = u.shape
    W = W2 // 2
    T = M // nbatch
    nblk = W // LRU_BLOCK_W
    bw = LRU_BLOCK_W
    wg = jnp.transpose(gate_w, (2, 3, 0, 1, 4)).reshape(nblk, bw, 4 * bw)
    bg = jnp.transpose(gate_b.reshape(2, 2, nblk, bw), (2, 0, 1, 3)).reshape(nblk, 1, 4 * bw)
    return pl.pallas_call(
        functools.partial(_lru_kernel, ctx_len=ctx_len),
        out_shape=jax.ShapeDtypeStruct((M, W), BF16),
        grid=(nbatch, nblk),
        in_specs=[
            pl.BlockSpec((T, bw), lambda b, n: (b, n)),
            pl.BlockSpec((T, bw), lambda b, n: (b, nblk + n)),
            pl.BlockSpec((CONV_W, bw), lambda b, n: (0, n)),
            pl.BlockSpec((1, bw), lambda b, n: (0, n)),
            pl.BlockSpec((1, bw, 4 * bw), lambda b, n: (n, 0, 0)),
            pl.BlockSpec((1, 1, 4 * bw), lambda b, n: (n, 0, 0)),
            pl.BlockSpec((2, bw), lambda b, n: (0, n)),
        ],
        out_specs=pl.BlockSpec((T, bw), lambda b, n: (b, n)),
        scratch_shapes=[pltpu.VMEM((T, bw), F32) for _ in range(4)],
        compiler_params=_cparams(("parallel", "parallel"), VMEM_LIMIT),
        name="lru_mix",
    )(u, u, conv_w, conv_b.reshape(1, W), wg, bg, lam)


def _attn_kernel(sink_ref, q_ref, k_ref, v_ref, o_ref, *, ctx_len, n_kv):
    QB = q_ref.shape[0]
    T = k_ref.shape[0]
    n_lat = (T - ctx_len) // QB
    ctx_blocks = ctx_len // QB
    band = 3 * QB
    qb = pl.program_id(1)
    j = qb - ctx_blocks
    start = jnp.clip(j - 1, 0, n_lat - 3)
    off = pl.multiple_of(ctx_len + start * QB, QB)
    qpos = j * QB + lax.broadcasted_iota(jnp.int32, (QB, band), 0)
    kpos = start * QB + lax.broadcasted_iota(jnp.int32, (QB, band), 1)
    valid = jnp.logical_and(jnp.abs(qpos - kpos) <= WINDOW, qb >= ctx_blocks)
    k_band = k_ref[pl.ds(off, band), :]
    v_band = v_ref[pl.ds(off, band), :]
    k_ctx = k_ref[0:ctx_len, :]
    v_ctx = v_ref[0:ctx_len, :]
    nt = (((1,), (1,)), ((), ()))
    for h in range(n_kv):
        hs = slice(h * HEAD_DIM, (h + 1) * HEAD_DIM)
        kb, vb, kc, vc = k_band[:, hs], v_band[:, hs], k_ctx[:, hs], v_ctx[:, hs]
        outs = []
        for g in range(GROUP):
            head = h * GROUP + g
            q = q_ref[:, head * HEAD_DIM:(head + 1) * HEAD_DIM]
            s_c = lax.dot_general(q, kc, nt, preferred_element_type=F32)
            s_b = lax.dot_general(q, kb, nt, preferred_element_type=F32)
            s_b = jnp.where(valid, s_b, NEG_INF)
            sink = sink_ref[head]
            m = jnp.maximum(jnp.maximum(jnp.max(s_c, axis=-1, keepdims=True),
                                        jnp.max(s_b, axis=-1, keepdims=True)), sink)
            p_c = jnp.exp(s_c - m)
            p_b = jnp.exp(s_b - m)
            denom = (jnp.sum(p_c, axis=-1, keepdims=True) + jnp.sum(p_b, axis=-1, keepdims=True)
                     + jnp.exp(sink - m))
            inv = 1.0 / denom
            acc = jnp.dot((p_c * inv).astype(BF16), vc, preferred_element_type=F32)
            acc = acc + jnp.dot((p_b * inv).astype(BF16), vb, preferred_element_type=F32)
            outs.append(acc)
        for g in range(0, GROUP, 2):
            col = (h * GROUP + g) * HEAD_DIM
            o_ref[:, col:col + 2 * HEAD_DIM] = jnp.concatenate(outs[g:g + 2], axis=1).astype(o_ref.dtype)


def _attention(qkv, sinks, *, nbatch, ctx_len, q_dim, kv_dim):
    M = qkv.shape[0]
    T = M // nbatch
    nqb = T // Q_BLOCK
    kcol = q_dim // kv_dim
    return pl.pallas_call(
        functools.partial(_attn_kernel, ctx_len=ctx_len, n_kv=kv_dim // HEAD_DIM),
        out_shape=jax.ShapeDtypeStruct((M, q_dim), BF16),
        grid_spec=pltpu.PrefetchScalarGridSpec(
            num_scalar_prefetch=1,
            grid=(nbatch, nqb),
            in_specs=[
                pl.BlockSpec((Q_BLOCK, q_dim), lambda b, q, s: (b * nqb + q, 0)),
                pl.BlockSpec((T, kv_dim), lambda b, q, s: (b, kcol)),
                pl.BlockSpec((T, kv_dim), lambda b, q, s: (b, kcol + 1)),
            ],
            out_specs=pl.BlockSpec((Q_BLOCK, q_dim), lambda b, q, s: (b * nqb + q, 0)),
        ),
        compiler_params=_cparams(("parallel", "parallel"), VMEM_LIMIT),
        name="swa_attention",
    )(sinks, qkv, qkv, qkv)


def _split_bf16(x):
    hi = x.astype(BF16)
    lo = (x - hi.astype(F32)).astype(BF16)
    return hi, lo


def _router_kernel(x_ref, mod_ref, wr_ref, br_ref, tok_ref, e_ref, g_ref, r_ref, cnt_ref, carry_ref):
    i = pl.program_id(0)

    @pl.when(i == 0)
    def _():
        carry_ref[...] = jnp.zeros_like(carry_ref)

    shift = mod_ref[0:1, :]
    scale = mod_ref[1:2, :]
    a = x_ref[...] * (1.0 + scale) + shift
    tok_ref[...] = a.astype(BF16)
    a_hi, a_lo = _split_bf16(a)
    w_hi, w_lo = _split_bf16(wr_ref[...])
    logits = (jnp.dot(a_hi, w_hi, preferred_element_type=F32)
              + (jnp.dot(a_lo, w_hi, preferred_element_type=F32)
                 + jnp.dot(a_hi, w_lo, preferred_element_type=F32))) + br_ref[...]
    tm, ne = logits.shape
    lane_e = lax.broadcasted_iota(jnp.int32, (tm, ne), 1)
    lane_o = lax.broadcasted_iota(jnp.int32, (tm, LANES), 1)
    vals, idxs, hots = [], [], []
    cur = logits
    for k in range(TOP_K):
        m = jnp.max(cur, axis=-1, keepdims=True)
        idx = jnp.min(jnp.where(cur == m, lane_e, ne), axis=-1, keepdims=True)
        hot = lane_e == idx
        vals.append(m)
        idxs.append(idx)
        hots.append(hot)
        cur = jnp.where(hot, -jnp.inf, cur)
    exps = [jnp.exp(v - vals[0]) for v in vals]
    inv = 1.0 / (exps[0] + exps[1] + exps[2] + exps[3])
    msum = jnp.zeros((tm, ne), F32)
    for hot in hots:
        msum = msum + hot.astype(F32)
    rr = lax.broadcasted_iota(jnp.int32, (tm, tm), 0)
    cc = lax.broadcasted_iota(jnp.int32, (tm, tm), 1)
    tri = (cc < rr).astype(BF16)
    before = jnp.dot(tri, msum.astype(BF16), preferred_element_type=F32) + carry_ref[...]
    e_out = jnp.zeros((tm, LANES), jnp.int32)
    g_out = jnp.zeros((tm, LANES), F32)
    r_out = jnp.zeros((tm, LANES), jnp.int32)
    for k in range(TOP_K):
        rank = jnp.sum(jnp.where(hots[k], before, 0.0), axis=-1, keepdims=True).astype(jnp.int32)
        e_out = jnp.where(lane_o == k, idxs[k], e_out)
        g_out = jnp.where(lane_o == k, exps[k] * inv, g_out)
        r_out = jnp.where(lane_o == k, rank, r_out)
    e_ref[...] = e_out
    g_ref[...] = g_out
    r_ref[...] = r_out
    carry_ref[...] = carry_ref[...] + jnp.sum(msum, axis=0, keepdims=True)
    cnt_ref[...] = carry_ref[...].astype(jnp.int32)


def _router(x, mods, ls, w_r, b_r, *, geom):
    M, D = x.shape
    ne = w_r.shape[1]
    tpb, ctx_tiles, nbatch = geom
    return pl.pallas_call(
        _router_kernel,
        out_shape=(
            jax.ShapeDtypeStruct((M, D), BF16),
            jax.ShapeDtypeStruct((M, LANES), jnp.int32),
            jax.ShapeDtypeStruct((M, LANES), F32),
            jax.ShapeDtypeStruct((M, LANES), jnp.int32),
            jax.ShapeDtypeStruct((1, ne), jnp.int32),
        ),
        grid=(M // TM,),
        in_specs=[
            pl.BlockSpec((TM, D), lambda i: (i, 0)),
            pl.BlockSpec((None, None, 3, D), lambda i: (ls, _mod_row(i, tpb, ctx_tiles, nbatch), 0, 0)),
            pl.BlockSpec((D, ne), lambda i: (0, 0)),
            pl.BlockSpec((1, ne), lambda i: (0, 0)),
        ],
        out_specs=(
            pl.BlockSpec((TM, D), lambda i: (i, 0)),
            pl.BlockSpec((TM, LANES), lambda i: (i, 0)),
            pl.BlockSpec((TM, LANES), lambda i: (i, 0)),
            pl.BlockSpec((TM, LANES), lambda i: (i, 0)),
            pl.BlockSpec((1, ne), lambda i: (0, 0)),
        ),
        scratch_shapes=[pltpu.VMEM((1, ne), F32)],
        compiler_params=_cparams(("arbitrary",), VMEM_LIMIT),
        name="moe_router",
    )(x, mods, w_r, b_r.reshape(1, ne))


def _expert_kernel(blk_e_ref, nused_ref, xs_ref, wgu_ref, bgu_ref, wdn_ref, bdn_ref, o_ref):
    b = pl.program_id(0)
    de = wdn_ref.shape[1]
    d = wdn_ref.shape[2]

    @pl.when(b < nused_ref[0])
    def _():
        x = xs_ref[...]
        hs = []
        for c in range(2 * de // 256):
            sl = slice(c * 256, (c + 1) * 256)
            hs.append(jnp.dot(x, wgu_ref[0, :, sl].astype(BF16), preferred_element_type=F32) + bgu_ref[0, :, sl])
        h = jnp.concatenate(hs, axis=1)
        g = jnp.minimum(h[:, :de], SWIGLU_LIMIT)
        u = jnp.clip(h[:, de:], -SWIGLU_LIMIT, SWIGLU_LIMIT)
        act = ((u + 1.0) * (g * _sigmoid(SWIGLU_ALPHA * g))).astype(BF16)
        for c in range(d // 512):
            sl = slice(c * 512, (c + 1) * 512)
            o_ref[:, sl] = jnp.dot(act, wdn_ref[0, :, sl].astype(BF16), preferred_element_type=F32) + bdn_ref[0, :, sl]

    @pl.when(b >= nused_ref[0])
    def _():
        o_ref[...] = jnp.zeros_like(o_ref)


def _expert_ffn(xs, blk_e, n_used, w_gu, b_gu, w_dn, b_dn):
    P, D = xs.shape
    ne, _, de2 = w_gu.shape
    de = de2 // 2
    n_blk = P // EXPERT_BM
    return pl.pallas_call(
        _expert_kernel,
        out_shape=jax.ShapeDtypeStruct((P, D), F32),
        grid_spec=pltpu.PrefetchScalarGridSpec(
            num_scalar_prefetch=2,
            grid=(n_blk,),
            in_specs=[
                pl.BlockSpec((EXPERT_BM, D), lambda b, e, n: (b, 0)),
                pl.BlockSpec((1, D, de2), lambda b, e, n: (e[b], 0, 0)),
                pl.BlockSpec((1, 1, de2), lambda b, e, n: (e[b], 0, 0)),
                pl.BlockSpec((1, de, D), lambda b, e, n: (e[b], 0, 0)),
                pl.BlockSpec((1, 1, D), lambda b, e, n: (e[b], 0, 0)),
            ],
            out_specs=pl.BlockSpec((EXPERT_BM, D), lambda b, e, n: (b, 0)),
        ),
        compiler_params=_cparams(("arbitrary",), VMEM_LIMIT),
        name="expert_ffn",
    )(blk_e, n_used, xs, w_gu, b_gu.reshape(ne, 1, de2), w_dn, b_dn.reshape(ne, 1, D))


def _moe_layer(x, mods, ls, w_r, b_r, w_gu, b_gu, w_dn, b_dn, *, geom):
    M, D = x.shape
    ne = w_r.shape[1]
    bm = EXPERT_BM
    tok, e128, g128, r128, counts = _router(x, mods, ls, w_r, b_r, geom=geom)
    top_e = e128[:, :TOP_K]
    rank = r128[:, :TOP_K]
    counts = counts[0]
    padded = ((counts + bm - 1) // bm) * bm
    ends = jnp.cumsum(padded)
    pstart = ends - padded
    dest = pstart[top_e] + rank
    A = M * TOP_K
    P = (-(-A // bm)) * bm + ne * bm
    n_blk = P // bm
    tok_id = jnp.broadcast_to(jnp.arange(M, dtype=jnp.int32)[:, None], (M, TOP_K))
    row_tok = jnp.zeros((P,), jnp.int32).at[dest.reshape(-1)].set(tok_id.reshape(-1))
    blk_e = jnp.minimum(jnp.searchsorted(ends, jnp.arange(n_blk, dtype=jnp.int32) * bm, side='right'),
                        ne - 1).astype(jnp.int32)
    n_used = (ends[-1] // bm).astype(jnp.int32).reshape(1)
    blk_e = jnp.where(jnp.arange(n_blk) < n_used[0], blk_e, blk_e[jnp.maximum(n_used[0] - 1, 0)])
    xs = jnp.take(tok, row_tok, axis=0)
    ys = _expert_ffn(xs, blk_e, n_used, w_gu, b_gu, w_dn, b_dn)
    y4 = jnp.take(ys, dest.reshape(-1), axis=0).reshape(M, TOP_K, D)
    return y4, g128


def _rope_tables(seq, ctx_len):
    rows = seq // GRID_W
    row_pos = jnp.repeat(jnp.arange(rows), GRID_W).astype(F32)
    col_pos = jnp.tile(jnp.arange(GRID_W), rows).astype(F32)
    n = HEAD_DIM // 2
    freqs = ROPE_BASE ** (-jnp.arange(0, n, 2, dtype=F32) / n)
    ang_r = row_pos[:, None] * freqs[None, :]
    ang_c = col_pos[:, None] * freqs[None, :]
    cos = jnp.concatenate([jnp.cos(ang_r), jnp.cos(ang_r), jnp.cos(ang_c), jnp.cos(ang_c)], axis=1)
    sin = jnp.concatenate([-jnp.sin(ang_r), jnp.sin(ang_r), -jnp.sin(ang_c), jnp.sin(ang_c)], axis=1)
    cos = jnp.concatenate([jnp.ones((ctx_len, HEAD_DIM), F32), cos], axis=0)
    sin = jnp.concatenate([jnp.zeros((ctx_len, HEAD_DIM), F32), sin], axis=0)
    cos = jnp.tile(cos, (1, LANES // HEAD_DIM))
    sin = jnp.tile(sin, (1, LANES // HEAD_DIM))
    scale = HEAD_DIM ** -0.5
    return jnp.stack([cos * scale, sin * scale, cos, sin], axis=0)


def kernel(x, c, ctx, c_ctx, ada_w, ada_b, ln_g, ln_b, lru_w_in, lru_conv_w, lru_conv_b, lru_gate_w, lru_gate_b, lru_lambda, lru_w_out, attn_w_qkv, attn_sinks, attn_w_o, router_w, router_b, moe_w_gu, moe_b_gu, moe_w_down, moe_b_down):
    B, S, D = x.shape
    C = ctx.shape[1]
    depth = ada_w.shape[0]
    T = C + S
    assert C % TM == 0 and S % TM == 0 and C % Q_BLOCK == 0
    geom = (T // TM, C // TM, B)
    alpha = (2.0 * depth) ** 0.25
    kv_dim = (attn_w_qkv.shape[2] - D) // 2

    act = jax.nn.silu(jnp.concatenate([c, c_ctx[None, :]], axis=0))
    mods = _ada_params(act, ada_w.reshape(depth * 2, D, 3 * D), ada_b.reshape(depth * 2, 3 * D))
    mods = mods.reshape(depth * 2, SUBLANES, 3, D)
    tab = _rope_tables(S, C)

    z = jnp.concatenate([ctx, x], axis=1).reshape(B * T, D)
    for i in range(depth):
        j = i // 2
        if i % 2 == 0:
            u = _mod_matmul(z, mods, 2 * i, lru_w_in[j].astype(BF16), tn=D, out_dtype=F32, geom=geom)
            y = _lru_mix(u, lru_conv_w[j], lru_conv_b[j], lru_gate_w[j], lru_gate_b[j], lru_lambda[j],
                         nbatch=B, ctx_len=C)
            w_out = lru_w_out[j]
        else:
            qkv = _mod_matmul(z, mods, 2 * i, attn_w_qkv[j].astype(BF16), tn=D + 2 * kv_dim, out_dtype=BF16,
                              geom=geom, rope=(tab, (D, kv_dim)))
            y = _attention(qkv, attn_sinks[j], nbatch=B, ctx_len=C, q_dim=D, kv_dim=kv_dim)
            w_out = attn_w_o[j]
        z = _proj_ln(y, w_out.astype(BF16), z, mods, 2 * i, ln_g[i, 0], ln_b[i, 0], alpha=alpha, geom=geom)
        y4, g128 = _moe_layer(z, mods, 2 * i + 1, router_w[i], router_b[i], moe_w_gu[i], moe_b_gu[i],
                              moe_w_down[i], moe_b_down[i], geom=geom)
        z = _combine_ln(y4, g128, z, mods, 2 * i + 1, ln_g[i, 1], ln_b[i, 1], alpha=alpha, geom=geom)
    return z.reshape(B, T, D)[:, C:, :]
```

```python
import functools
import math

import jax
import jax.numpy as jnp
from jax import lax
from jax.experimental import pallas as pl
from jax.experimental.pallas import tpu as pltpu

GRID_W = 64
LN_EPS = 1e-5
LRU_BLOCK_W = 256
CONV_W = 4
LRU_C = 8.0
HEAD_DIM = 64
GROUP = 8
WINDOW = 128
Q_BLOCK = 128
ROPE_BASE = 10000.0
NEG_INF = -1e30
TOP_K = 4
SWIGLU_LIMIT = 7.0
SWIGLU_ALPHA = 1.702

LANES = 128
SUBLANES = 8
TM = 256
EXPERT_BM = 256
VMEM_LIMIT = 56 * 1024 * 1024

BF16 = jnp.bfloat16
F32 = jnp.float32


def _cparams(sem, vmem=None):
    return pltpu.CompilerParams(dimension_semantics=sem, vmem_limit_bytes=vmem)


def _ada_kernel(cb_ref, w_ref, b_ref, o_ref, acc_ref, *, nrows):
    k = pl.program_id(2)

    @pl.when(k == 0)
    def _():
        acc_ref[...] = jnp.zeros_like(acc_ref)

    tk, tn = w_ref.shape[-2:]
    for j in range(tn // LANES):
        w = w_ref[0, :, j * LANES:(j + 1) * LANES]
        for r in range(nrows):
            p = w * cb_ref[r]
            acc_ref[r, :, j * LANES:(j + 1) * LANES] += p.reshape(tk // SUBLANES, SUBLANES, LANES).sum(axis=0)

    @pl.when(k == pl.num_programs(2) - 1)
    def _():
        o_ref[0] = jnp.zeros(o_ref.shape[1:], F32)
        for r in range(nrows):
            o_ref[0, r:r + 1, :] = acc_ref[r].sum(axis=0, keepdims=True) + b_ref[0]


def _ada_params(act, w, bias):
    R, D = act.shape
    L, _, N = w.shape
    tk, tn = 256, 2048
    cb = jnp.broadcast_to(act[:, :, None], (R, D, LANES))
    return pl.pallas_call(
        functools.partial(_ada_kernel, nrows=R),
        out_shape=jax.ShapeDtypeStruct((L, SUBLANES, N), F32),
        grid=(L, N // tn, D // tk),
        in_specs=[
            pl.BlockSpec((R, tk, LANES), lambda l, n, k: (0, k, 0)),
            pl.BlockSpec((1, tk, tn), lambda l, n, k: (l, k, n)),
            pl.BlockSpec((1, 1, tn), lambda l, n, k: (l, 0, n)),
        ],
        out_specs=pl.BlockSpec((1, SUBLANES, tn), lambda l, n, k: (l, 0, n)),
        scratch_shapes=[pltpu.VMEM((R, SUBLANES, tn), F32)],
        compiler_params=_cparams(("parallel", "parallel", "arbitrary")),
        name="ada_params",
    )(cb, w, bias.reshape(L, 1, N))


def _mod_row(i, tiles_per_batch, ctx_tiles, nbatch):
    return jnp.where(i % tiles_per_batch < ctx_tiles, nbatch, i // tiles_per_batch)


def _rope_chunk(y, cos, sin_signed, first_half):
    rot = jnp.where(first_half, pltpu.roll(y, LANES - 16, axis=1), pltpu.roll(y, 16, axis=1))
    return y * cos + rot * sin_signed


def _modmm_kernel(x_ref, mod_ref, w_ref, *rest, rope_cols):
    if rope_cols is None:
        (o_ref,) = rest
    else:
        tab_ref, o_ref = rest
    shift = mod_ref[0:1, :]
    scale = mod_ref[1:2, :]
    a = (x_ref[...] * (1.0 + scale) + shift).astype(BF16)
    tn = o_ref.shape[1]
    ch = 512 if tn % 512 == 0 else 256
    for c in range(tn // ch):
        acc = jnp.dot(a, w_ref[:, c * ch:(c + 1) * ch].astype(BF16), preferred_element_type=F32)
        if rope_cols is None:
            o_ref[:, c * ch:(c + 1) * ch] = acc.astype(o_ref.dtype)
        else:
            q_cols, k_cols = rope_cols
            lane = lax.broadcasted_iota(jnp.int32, (acc.shape[0], LANES), 1)
            first_half = (lane % 32) < 16
            for s in range(ch // LANES):
                col = c * ch + s * LANES
                y = acc[:, s * LANES:(s + 1) * LANES]
                if col < q_cols:
                    y = _rope_chunk(y, tab_ref[0], tab_ref[1], first_half)
                elif col < q_cols + k_cols:
                    y = _rope_chunk(y, tab_ref[2], tab_ref[3], first_half)
                o_ref[:, col:col + LANES] = y.astype(o_ref.dtype)


def _mod_matmul(x, mods, ls, w, *, tn, out_dtype, geom, rope=None):
    M, K = x.shape
    N = w.shape[1]
    tpb, ctx_tiles, nbatch = geom
    in_specs = [
        pl.BlockSpec((TM, K), lambda j, i: (i, 0)),
        pl.BlockSpec((None, None, 3, K), lambda j, i: (ls, _mod_row(i, tpb, ctx_tiles, nbatch), 0, 0)),
        pl.BlockSpec((K, tn), lambda j, i: (0, j)),
    ]
    args = [x, mods, w]
    rope_cols = None
    if rope is not None:
        tab, rope_cols = rope
        in_specs.append(pl.BlockSpec((4, TM, LANES), lambda j, i: (0, i % tpb, 0)))
        args.append(tab)
    return pl.pallas_call(
        functools.partial(_modmm_kernel, rope_cols=rope_cols),
        out_shape=jax.ShapeDtypeStruct((M, N), out_dtype),
        grid=(N // tn, M // TM),
        in_specs=in_specs,
        out_specs=pl.BlockSpec((TM, tn), lambda j, i: (i, j)),
        compiler_params=_cparams(("parallel", "parallel"), VMEM_LIMIT),
        name="mod_matmul",
    )(*args)


def _layer_norm_rows(z, g, b):
    mu = jnp.mean(z, axis=-1, keepdims=True)
    zc = z - mu
    var = jnp.mean(zc * zc, axis=-1, keepdims=True)
    return zc * lax.rsqrt(var + LN_EPS) * g + b


def _proj_ln_kernel(y_ref, w_ref, x_ref, mod_ref, g_ref, b_ref, o_ref, z_ref, *, alpha):
    gate = mod_ref[2:3, :]
    y = y_ref[...]
    n = o_ref.shape[1]
    ch = 512
    for c in range(n // ch):
        sl = slice(c * ch, (c + 1) * ch)
        acc = jnp.dot(y, w_ref[:, sl].astype(BF16), preferred_element_type=F32)
        z_ref[:, sl] = alpha * x_ref[:, sl] + gate[:, sl] * acc
    o_ref[...] = _layer_norm_rows(z_ref[...], g_ref[...], b_ref[...])


def _proj_ln(y, w, x, mods, ls, ln_g, ln_b, *, alpha, geom):
    M, K = y.shape
    D = w.shape[1]
    tpb, ctx_tiles, nbatch = geom
    return pl.pallas_call(
        functools.partial(_proj_ln_kernel, alpha=alpha),
        out_shape=jax.ShapeDtypeStruct((M, D), F32),
        grid=(M // TM,),
        in_specs=[
            pl.BlockSpec((TM, K), lambda i: (i, 0)),
            pl.BlockSpec((K, D), lambda i: (0, 0)),
            pl.BlockSpec((TM, D), lambda i: (i, 0)),
            pl.BlockSpec((None, None, 3, D), lambda i: (ls, _mod_row(i, tpb, ctx_tiles, nbatch), 0, 0)),
            pl.BlockSpec((1, D), lambda i: (0, 0)),
            pl.BlockSpec((1, D), lambda i: (0, 0)),
        ],
        out_specs=pl.BlockSpec((TM, D), lambda i: (i, 0)),
        scratch_shapes=[pltpu.VMEM((TM, D), F32)],
        compiler_params=_cparams(("parallel",), VMEM_LIMIT),
        name="proj_ln",
    )(y, w, x, mods, ln_g.reshape(1, D), ln_b.reshape(1, D))


def _load_tile_indices(idx_hbm, idx_smem, sem):
    cp = pltpu.make_async_copy(idx_hbm.at[pl.program_id(0)], idx_smem, sem)
    cp.start()
    cp.wait()


def _combine_ln_kernel(dest_hbm, ys_hbm, gate4_ref, x_ref, mod_ref, g_ref, b_ref, o_ref,
                       idx_smem, buf_ref, idx_sem, row_sem, *, alpha):
    _load_tile_indices(dest_hbm, idx_smem, idx_sem)

    def issue(r, carry):
        for k in range(TOP_K):
            row = idx_smem[r * TOP_K + k]
            pltpu.make_async_copy(ys_hbm.at[pl.ds(row, 1)], buf_ref.at[k, pl.ds(r, 1)], row_sem).start()
        return carry

    lax.fori_loop(0, TM, issue, 0, unroll=8)
    for k in range(TOP_K):
        pltpu.make_async_copy(ys_hbm.at[pl.ds(0, TM)], buf_ref.at[k], row_sem).wait()
    gate = mod_ref[2:3, :]
    y = buf_ref[0] * gate4_ref[:, 0:1]
    for k in range(1, TOP_K):
        y = y + buf_ref[k] * gate4_ref[:, k:k + 1]
    z = alpha * x_ref[...] + gate * y
    o_ref[...] = _layer_norm_rows(z, g_ref[...], b_ref[...])


def _combine_ln(dest_tiles, ys, gate4, x, mods, ls, ln_g, ln_b, *, alpha, geom):
    M, D = x.shape
    tpb, ctx_tiles, nbatch = geom
    return pl.pallas_call(
        functools.partial(_combine_ln_kernel, alpha=alpha),
        out_shape=jax.ShapeDtypeStruct((M, D), F32),
        grid=(M // TM,),
        in_specs=[
            pl.BlockSpec(memory_space=pl.ANY),
            pl.BlockSpec(memory_space=pl.ANY),
            pl.BlockSpec((TM, LANES), lambda i: (i, 0)),
            pl.BlockSpec((TM, D), lambda i: (i, 0)),
            pl.BlockSpec((None, None, 3, D), lambda i: (ls, _mod_row(i, tpb, ctx_tiles, nbatch), 0, 0)),
            pl.BlockSpec((1, D), lambda i: (0, 0)),
            pl.BlockSpec((1, D), lambda i: (0, 0)),
        ],
        out_specs=pl.BlockSpec((TM, D), lambda i: (i, 0)),
        scratch_shapes=[
            pltpu.SMEM((TM * TOP_K,), jnp.int32),
            pltpu.VMEM((TOP_K, TM, D), F32),
            pltpu.SemaphoreType.DMA,
            pltpu.SemaphoreType.DMA,
        ],
        compiler_params=_cparams(("arbitrary",), VMEM_LIMIT),
        name="combine_ln",
    )(dest_tiles, ys, gate4, x, mods, ln_g.reshape(1, D), ln_b.reshape(1, D))


def _dispatch_kernel(pad_lo_ref, nused_ref, dest_hbm, x_ref, mod_ref, xs_hbm, idx_smem, a_ref, zero_ref,
                     idx_sem, row_sem, *, bm):
    i = pl.program_id(0)

    @pl.when(i == 0)
    def _():
        zero_ref[...] = jnp.zeros_like(zero_ref)
        n_exp = pad_lo_ref.shape[0]
        n_blk = xs_hbm.shape[0] // bm

        def zero_block(row0):
            return pltpu.make_async_copy(zero_ref, xs_hbm.at[pl.ds(pl.multiple_of(row0, bm), bm)], row_sem)

        for e in range(n_exp):
            @pl.when(pad_lo_ref[e] >= 0)
            def _():
                zero_block(pad_lo_ref[e]).start()

        def fill_tail(b, carry):
            zero_block(b * bm).start()
            return carry

        lax.fori_loop(nused_ref[0], n_blk, fill_tail, 0)
        for e in range(n_exp):
            @pl.when(pad_lo_ref[e] >= 0)
            def _():
                zero_block(0).wait()

        def wait_tail(b, carry):
            zero_block(0).wait()
            return carry

        lax.fori_loop(nused_ref[0], n_blk, wait_tail, 0)

    _load_tile_indices(dest_hbm, idx_smem, idx_sem)
    shift = mod_ref[0:1, :]
    scale = mod_ref[1:2, :]
    a_ref[...] = x_ref[...] * (1.0 + scale) + shift

    def issue(r, carry):
        for k in range(TOP_K):
            row = idx_smem[r * TOP_K + k]
            pltpu.make_async_copy(a_ref.at[pl.ds(r, 1)], xs_hbm.at[pl.ds(row, 1)], row_sem).start()
        return carry

    lax.fori_loop(0, TM, issue, 0, unroll=8)
    for k in range(TOP_K):
        pltpu.make_async_copy(a_ref, xs_hbm.at[pl.ds(0, TM)], row_sem).wait()


def _dispatch(pad_lo, n_used, dest_tiles, x, mods, ls, n_rows, *, geom):
    M, D = x.shape
    tpb, ctx_tiles, nbatch = geom
    bm = EXPERT_BM
    return pl.pallas_call(
        functools.partial(_dispatch_kernel, bm=bm),
        out_shape=jax.ShapeDtypeStruct((n_rows, D), F32),
        grid_spec=pltpu.PrefetchScalarGridSpec(
            num_scalar_prefetch=2,
            grid=(M // TM,),
            in_specs=[
                pl.BlockSpec(memory_space=pl.ANY),
                pl.BlockSpec((TM, D), lambda i, p, n: (i, 0)),
                pl.BlockSpec((None, None, 3, D), lambda i, p, n: (ls, _mod_row(i, tpb, ctx_tiles, nbatch), 0, 0)),
            ],
            out_specs=pl.BlockSpec(memory_space=pl.ANY),
            scratch_shapes=[
                pltpu.SMEM((TM * TOP_K,), jnp.int32),
                pltpu.VMEM((TM, D), F32),
                pltpu.VMEM((bm, D), F32),
                pltpu.SemaphoreType.DMA,
                pltpu.SemaphoreType.DMA,
            ],
        ),
        compiler_params=_cparams(("arbitrary",), VMEM_LIMIT),
        name="moe_dispatch",
    )(pad_lo, n_used, dest_tiles, x, mods)


def _sigmoid(x):
    return 1.0 / (1.0 + jnp.exp(-x))


def _softplus(x):
    return jnp.maximum(x, 0.0) + jnp.log(1.0 + jnp.exp(-jnp.abs(x)))


def _gelu_tanh(x):
    c = math.sqrt(2.0 / math.pi)
    return 0.5 * x * (1.0 + jnp.tanh(c * (x + 0.044715 * (x * x * x))))


def _group_scan(a, b, row, reverse):
    for s in (1, 2, 4):
        if reverse:
            keep = row < SUBLANES - s
            shift = SUBLANES - s
        else:
            keep = row >= s
            shift = s
        a_sh = jnp.where(keep, pltpu.roll(a, shift, axis=0), 1.0)
        b_sh = jnp.where(keep, pltpu.roll(b, shift, axis=0), 0.0)
        b = b + a * b_sh
        a = a * a_sh
    return a, b


def _lru_kernel(ug_ref, ur_ref, cw_ref, cb_ref, wg_ref, bg_ref, lam_ref, o_ref,
                a0_ref, b0_ref, a1_ref, b1_ref, *, ctx_len):
    T, W = ur_ref.shape
    CH = 256
    n_chunks = T // CH
    ctx_chunks = ctx_len // CH
    row_c = lax.broadcasted_iota(jnp.int32, (CH, W), 0)
    sp = _softplus(-lam_ref[...])

    def coeff_chunk(c, carry):
        c0 = pl.multiple_of(c * CH, CH)
        cur = ur_ref[pl.ds(c0, CH), :]
        prev8 = ur_ref[pl.ds(pl.multiple_of(jnp.maximum(c0 - SUBLANES, 0), SUBLANES), SUBLANES), :]
        next8 = ur_ref[pl.ds(pl.multiple_of(jnp.minimum(c0 + CH, T - SUBLANES), SUBLANES), SUBLANES), :]
        seg_start = jnp.logical_or(c == 0, c == ctx_chunks)
        seg_end = jnp.logical_or(c == ctx_chunks - 1, c == n_chunks - 1)
        p_last = jnp.where(seg_start, 0.0, prev8[SUBLANES - 1:SUBLANES, :])
        n_first = jnp.where(seg_end, 0.0, next8[0:1, :])
        n_second = jnp.where(seg_end, 0.0, next8[1:2, :])
        u_m1 = jnp.where(row_c == 0, p_last, pltpu.roll(cur, 1, axis=0))
        u_p1 = jnp.where(row_c == CH - 1, n_first, pltpu.roll(cur, CH - 1, axis=0))
        u_p2 = jnp.where(row_c == CH - 2, n_first,
                         jnp.where(row_c == CH - 1, n_second, pltpu.roll(cur, CH - 2, axis=0)))
        r = cb_ref[...] + u_m1 * cw_ref[0:1, :]
        r = r + cur * cw_ref[1:2, :]
        r = r + u_p1 * cw_ref[2:3, :]
        r = r + u_p2 * cw_ref[3:4, :]
        gts = jnp.dot(r.astype(BF16), wg_ref[0].astype(BF16), preferred_element_type=F32) + bg_ref[0]
        for d, (a_ref, b_ref) in enumerate(((a0_ref, b0_ref), (a1_ref, b1_ref))):
            rg = _sigmoid(gts[:, (2 * d) * W:(2 * d + 1) * W])
            ig = _sigmoid(gts[:, (2 * d + 1) * W:(2 * d + 2) * W])
            log_a = -LRU_C * rg * sp[d:d + 1, :]
            a = jnp.exp(log_a)
            a_ref[pl.ds(c0, CH), :] = a
            b_ref[pl.ds(c0, CH), :] = jnp.sqrt(jnp.maximum(1.0 - a * a, 0.0)) * (ig * r)
        return carry

    lax.fori_loop(0, n_chunks, coeff_chunk, 0)

    row8 = lax.broadcasted_iota(jnp.int32, (SUBLANES, W), 0)
    n_groups = T // SUBLANES
    ctx_groups = ctx_len // SUBLANES

    def scan_step(k, carry):
        hf, hb = carry
        gf = pl.multiple_of(k * SUBLANES, SUBLANES)
        a, b = _group_scan(a0_ref[pl.ds(gf, SUBLANES), :], b0_ref[pl.ds(gf, SUBLANES), :], row8, False)
        h = a * hf + b
        a0_ref[pl.ds(gf, SUBLANES), :] = h
        hf = jnp.broadcast_to(h[SUBLANES - 1:SUBLANES, :], h.shape)
        g = jnp.where(k < ctx_groups, ctx_groups - 1 - k, n_groups - 1 - (k - ctx_groups))
        gb = pl.multiple_of(g * SUBLANES, SUBLANES)
        a, b = _group_scan(a1_ref[pl.ds(gb, SUBLANES), :], b1_ref[pl.ds(gb, SUBLANES), :], row8, True)
        h = a * hb + b
        a1_ref[pl.ds(gb, SUBLANES), :] = h
        hb = jnp.broadcast_to(h[0:1, :], h.shape)
        return hf, hb

    zero = jnp.zeros((SUBLANES, W), F32)
    lax.fori_loop(0, n_groups, scan_step, (zero, zero), unroll=4)

    def out_chunk(c, carry):
        c0 = pl.multiple_of(c * CH, CH)
        y = a0_ref[pl.ds(c0, CH), :] + a1_ref[pl.ds(c0, CH), :]
        o_ref[pl.ds(c0, CH), :] = (y * _gelu_tanh(ug_ref[pl.ds(c0, CH), :])).astype(o_ref.dtype)
        return carry

    lax.fori_loop(0, n_chunks, out_chunk, 0)


def _lru_mix(u, conv_w, conv_b, gate_w, gate_b, lam, *, nbatch, ctx_len):
    M, W2 = u.shape
    W = W2 // 2
    T = M // nbatch
    nblk = W // LRU_BLOCK_W
    bw = LRU_BLOCK_W
    wg = jnp.transpose(gate_w, (2, 3, 0, 1, 4)).reshape(nblk, bw, 4 * bw)
    bg = jnp.transpose(gate_b.reshape(2, 2, nblk, bw), (2, 0, 1, 3)).reshape(nblk, 1, 4 * bw)
    return pl.pallas_call(
        functools.partial(_lru_kernel, ctx_len=ctx_len),
        out_shape=jax.ShapeDtypeStruct((M, W), BF16),
        grid=(nbatch, nblk),
        in_specs=[
            pl.BlockSpec((T, bw), lambda b, n: (b, n)),
            pl.BlockSpec((T, bw), lambda b, n: (b, nblk + n)),
            pl.BlockSpec((CONV_W, bw), lambda b, n: (0, n)),
            pl.BlockSpec((1, bw), lambda b, n: (0, n)),
            pl.BlockSpec((1, bw, 4 * bw), lambda b, n: (n, 0, 0)),
            pl.BlockSpec((1, 1, 4 * bw), lambda b, n: (n, 0, 0)),
            pl.BlockSpec((2, bw), lambda b, n: (0, n)),
        ],
        out_specs=pl.BlockSpec((T, bw), lambda b, n: (b, n)),
        scratch_shapes=[pltpu.VMEM((T, bw), F32) for _ in range(4)],
        compiler_params=_cparams(("parallel", "parallel"), VMEM_LIMIT),
        name="lru_mix",
    )(u, u, conv_w, conv_b.reshape(1, W), wg, bg, lam)


def _attn_kernel(sink_ref, q_ref, k_ref, v_ref, o_ref, *, ctx_len, n_kv):
    QB = q_ref.shape[0]
    T = k_ref.shape[0]
    n_lat = (T - ctx_len) // QB
    ctx_blocks = ctx_len // QB
    band = 3 * QB
    qb = pl.program_id(1)
    j = qb - ctx_blocks
    start = jnp.clip(j - 1, 0, n_lat - 3)
    off = pl.multiple_of(ctx_len + start * QB, QB)
    qpos = j * QB + lax.broadcasted_iota(jnp.int32, (QB, band), 0)
    kpos = start * QB + lax.broadcasted_iota(jnp.int32, (QB, band), 1)
    valid = jnp.logical_and(jnp.abs(qpos - kpos) <= WINDOW, qb >= ctx_blocks)
    k_band = k_ref[pl.ds(off, band), :]
    v_band = v_ref[pl.ds(off, band), :]
    k_ctx = k_ref[0:ctx_len, :]
    v_ctx = v_ref[0:ctx_len, :]
    nt = (((1,), (1,)), ((), ()))
    for h in range(n_kv):
        hs = slice(h * HEAD_DIM, (h + 1) * HEAD_DIM)
        kb, vb, kc, vc = k_band[:, hs], v_band[:, hs], k_ctx[:, hs], v_ctx[:, hs]
        outs = []
        for g in range(GROUP):
            head = h * GROUP + g
            q = q_ref[:, head * HEAD_DIM:(head + 1) * HEAD_DIM]
            s_c = lax.dot_general(q, kc, nt, preferred_element_type=F32)
            s_b = lax.dot_general(q, kb, nt, preferred_element_type=F32)
            s_b = jnp.where(valid, s_b, NEG_INF)
            sink = sink_ref[head]
            m = jnp.maximum(jnp.maximum(jnp.max(s_c, axis=-1, keepdims=True),
                                        jnp.max(s_b, axis=-1, keepdims=True)), sink)
            p_c = jnp.exp(s_c - m)
            p_b = jnp.exp(s_b - m)
            denom = (jnp.sum(p_c, axis=-1, keepdims=True) + jnp.sum(p_b, axis=-1, keepdims=True)
                     + jnp.exp(sink - m))
            inv = 1.0 / denom
            acc = jnp.dot((p_c * inv).astype(BF16), vc, preferred_element_type=F32)
            acc = acc + jnp.dot((p_b * inv).astype(BF16), vb, preferred_element_type=F32)
            outs.append(acc)
        for g in range(0, GROUP, 2):
            col = (h * GROUP + g) * HEAD_DIM
            o_ref[:, col:col + 2 * HEAD_DIM] = jnp.concatenate(outs[g:g + 2], axis=1).astype(o_ref.dtype)


def _attention(qkv, sinks, *, nbatch, ctx_len, q_dim, kv_dim):
    M = qkv.shape[0]
    T = M // nbatch
    nqb = T // Q_BLOCK
    kcol = q_dim // kv_dim
    return pl.pallas_call(
        functools.partial(_attn_kernel, ctx_len=ctx_len, n_kv=kv_dim // HEAD_DIM),
        out_shape=jax.ShapeDtypeStruct((M, q_dim), BF16),
        grid_spec=pltpu.PrefetchScalarGridSpec(
            num_scalar_prefetch=1,
            grid=(nbatch, nqb),
            in_specs=[
                pl.BlockSpec((Q_BLOCK, q_dim), lambda b, q, s: (b * nqb + q, 0)),
                pl.BlockSpec((T, kv_dim), lambda b, q, s: (b, kcol)),
                pl.BlockSpec((T, kv_dim), lambda b, q, s: (b, kcol + 1)),
            ],
            out_specs=pl.BlockSpec((Q_BLOCK, q_dim), lambda b, q, s: (b * nqb + q, 0)),
        ),
        compiler_params=_cparams(("parallel", "parallel"), VMEM_LIMIT),
        name="swa_attention",
    )(sinks, qkv, qkv, qkv)


def _split_bf16(x):
    hi = x.astype(BF16)
    lo = (x - hi.astype(F32)).astype(BF16)
    return hi, lo


def _router_kernel(x_ref, mod_ref, wr_ref, br_ref, e_ref, g_ref, r_ref, cnt_ref, carry_ref):
    i = pl.program_id(0)

    @pl.when(i == 0)
    def _():
        carry_ref[...] = jnp.zeros_like(carry_ref)

    shift = mod_ref[0:1, :]
    scale = mod_ref[1:2, :]
    a = x_ref[...] * (1.0 + scale) + shift
    a_hi, a_lo = _split_bf16(a)
    w_hi, w_lo = _split_bf16(wr_ref[...])
    logits = (jnp.dot(a_hi, w_hi, preferred_element_type=F32)
              + (jnp.dot(a_lo, w_hi, preferred_element_type=F32)
                 + jnp.dot(a_hi, w_lo, preferred_element_type=F32))) + br_ref[...]
    tm, ne = logits.shape
    lane_e = lax.broadcasted_iota(jnp.int32, (tm, ne), 1)
    lane_o = lax.broadcasted_iota(jnp.int32, (tm, LANES), 1)
    vals, idxs, hots = [], [], []
    cur = logits
    for k in range(TOP_K):
        m = jnp.max(cur, axis=-1, keepdims=True)
        idx = jnp.min(jnp.where(cur == m, lane_e, ne), axis=-1, keepdims=True)
        hot = lane_e == idx
        vals.append(m)
        idxs.append(idx)
        hots.append(hot)
        cur = jnp.where(hot, -jnp.inf, cur)
    exps = [jnp.exp(v - vals[0]) for v in vals]
    inv = 1.0 / (exps[0] + exps[1] + exps[2] + exps[3])
    msum = jnp.zeros((tm, ne), F32)
    for hot in hots:
        msum = msum + hot.astype(F32)
    rr = lax.broadcasted_iota(jnp.int32, (tm, tm), 0)
    cc = lax.broadcasted_iota(jnp.int32, (tm, tm), 1)
    tri = (cc < rr).astype(BF16)
    before = jnp.dot(tri, msum.astype(BF16), preferred_element_type=F32) + carry_ref[...]
    e_out = jnp.zeros((tm, LANES), jnp.int32)
    g_out = jnp.zeros((tm, LANES), F32)
    r_out = jnp.zeros((tm, LANES), jnp.int32)
    for k in range(TOP_K):
        rank = jnp.sum(jnp.where(hots[k], before, 0.0), axis=-1, keepdims=True).astype(jnp.int32)
        e_out = jnp.where(lane_o == k, idxs[k], e_out)
        g_out = jnp.where(lane_o == k, exps[k] * inv, g_out)
        r_out = jnp.where(lane_o == k, rank, r_out)
    e_ref[...] = e_out
    g_ref[...] = g_out
    r_ref[...] = r_out
    carry_ref[...] = carry_ref[...] + jnp.sum(msum, axis=0, keepdims=True)
    cnt_ref[...] = carry_ref[...].astype(jnp.int32)


def _router(x, mods, ls, w_r, b_r, *, geom):
    M, D = x.shape
    ne = w_r.shape[1]
    tpb, ctx_tiles, nbatch = geom
    return pl.pallas_call(
        _router_kernel,
        out_shape=(
            jax.ShapeDtypeStruct((M, LANES), jnp.int32),
            jax.ShapeDtypeStruct((M, LANES), F32),
            jax.ShapeDtypeStruct((M, LANES), jnp.int32),
            jax.ShapeDtypeStruct((1, ne), jnp.int32),
        ),
        grid=(M // TM,),
        in_specs=[
            pl.BlockSpec((TM, D), lambda i: (i, 0)),
            pl.BlockSpec((None, None, 3, D), lambda i: (ls, _mod_row(i, tpb, ctx_tiles, nbatch), 0, 0)),
            pl.BlockSpec((D, ne), lambda i: (0, 0)),
            pl.BlockSpec((1, ne), lambda i: (0, 0)),
        ],
        out_specs=(
            pl.BlockSpec((TM, LANES), lambda i: (i, 0)),
            pl.BlockSpec((TM, LANES), lambda i: (i, 0)),
            pl.BlockSpec((TM, LANES), lambda i: (i, 0)),
            pl.BlockSpec((1, ne), lambda i: (0, 0)),
        ),
        scratch_shapes=[pltpu.VMEM((1, ne), F32)],
        compiler_params=_cparams(("arbitrary",), VMEM_LIMIT),
        name="moe_router",
    )(x, mods, w_r, b_r.reshape(1, ne))


def _expert_kernel(blk_e_ref, nused_ref, xs_ref, wgu_ref, bgu_ref, wdn_ref, bdn_ref, o_ref):
    b = pl.program_id(0)
    de = wdn_ref.shape[0]
    d = wdn_ref.shape[1]

    @pl.when(b < nused_ref[0])
    def _():
        x = xs_ref[...].astype(BF16)
        hs = []
        for c in range(2 * de // 256):
            sl = slice(c * 256, (c + 1) * 256)
            hs.append(jnp.dot(x, wgu_ref[:, sl].astype(BF16), preferred_element_type=F32) + bgu_ref[:, sl])
        h = jnp.concatenate(hs, axis=1)
        g = jnp.minimum(h[:, :de], SWIGLU_LIMIT)
        u = jnp.clip(h[:, de:], -SWIGLU_LIMIT, SWIGLU_LIMIT)
        act = ((u + 1.0) * (g * _sigmoid(SWIGLU_ALPHA * g))).astype(BF16)
        for c in range(d // 512):
            sl = slice(c * 512, (c + 1) * 512)
            o_ref[:, sl] = jnp.dot(act, wdn_ref[:, sl].astype(BF16), preferred_element_type=F32) + bdn_ref[:, sl]

    @pl.when(b >= nused_ref[0])
    def _():
        o_ref[...] = jnp.zeros_like(o_ref)


def _expert_ffn(xs, blk_e, n_used, layer, w_gu, b_gu, w_dn, b_dn):
    P, D = xs.shape
    _, ne, _, de2 = w_gu.shape
    de = de2 // 2
    n_blk = P // EXPERT_BM
    return pl.pallas_call(
        _expert_kernel,
        out_shape=jax.ShapeDtypeStruct((P, D), F32),
        grid_spec=pltpu.PrefetchScalarGridSpec(
            num_scalar_prefetch=2,
            grid=(n_blk,),
            in_specs=[
                pl.BlockSpec((EXPERT_BM, D), lambda b, e, n: (jnp.minimum(b, n[0] - 1), 0)),
                pl.BlockSpec((None, None, D, de2), lambda b, e, n: (layer, e[b], 0, 0)),
                pl.BlockSpec((None, None, 1, de2), lambda b, e, n: (layer, e[b], 0, 0)),
                pl.BlockSpec((None, None, de, D), lambda b, e, n: (layer, e[b], 0, 0)),
                pl.BlockSpec((None, None, 1, D), lambda b, e, n: (layer, e[b], 0, 0)),
            ],
            out_specs=pl.BlockSpec((EXPERT_BM, D), lambda b, e, n: (b, 0)),
        ),
        compiler_params=_cparams(("arbitrary",), VMEM_LIMIT),
        name="expert_ffn",
    )(blk_e, n_used, xs, w_gu, b_gu.reshape(b_gu.shape[0], ne, 1, de2), w_dn,
      b_dn.reshape(b_dn.shape[0], ne, 1, D))


def _moe_layer(x, mods, ls, layer, w_r, b_r, w_gu, b_gu, w_dn, b_dn, ln_g, ln_b, *, alpha, geom):
    M, D = x.shape
    ne = w_r.shape[1]
    bm = EXPERT_BM
    e128, g128, r128, counts = _router(x, mods, ls, w_r, b_r, geom=geom)
    top_e = e128[:, :TOP_K]
    rank = r128[:, :TOP_K]
    counts = counts[0]
    padded = ((counts + bm - 1) // bm) * bm
    ends = jnp.cumsum(padded)
    pstart = ends - padded
    experts = jnp.arange(ne, dtype=jnp.int32)
    dest = jnp.sum(jnp.where(top_e[..., None] == experts, pstart, 0), axis=-1) + rank
    dest_tiles = dest.astype(jnp.int32).reshape(M // TM, TM * TOP_K)
    A = M * TOP_K
    P = (-(-A // bm)) * bm + ne * bm
    n_blk = P // bm
    n_used = (ends[-1] // bm).astype(jnp.int32).reshape(1)
    blk_row = jnp.minimum(jnp.arange(n_blk, dtype=jnp.int32), n_used[0] - 1) * bm
    blk_e = jnp.minimum(jnp.sum(ends[None, :] <= blk_row[:, None], axis=1), ne - 1).astype(jnp.int32)
    pad_lo = jnp.where(padded > 0, ends - bm, -1).astype(jnp.int32)
    xs = _dispatch(pad_lo, n_used, dest_tiles, x, mods, ls, P, geom=geom)
    ys = _expert_ffn(xs, blk_e, n_used, layer, w_gu, b_gu, w_dn, b_dn)
    return _combine_ln(dest_tiles, ys, g128, x, mods, ls, ln_g, ln_b, alpha=alpha, geom=geom)


def _rope_tables(seq, ctx_len):
    rows = seq // GRID_W
    row_pos = jnp.repeat(jnp.arange(rows), GRID_W).astype(F32)
    col_pos = jnp.tile(jnp.arange(GRID_W), rows).astype(F32)
    n = HEAD_DIM // 2
    freqs = ROPE_BASE ** (-jnp.arange(0, n, 2, dtype=F32) / n)
    ang_r = row_pos[:, None] * freqs[None, :]
    ang_c = col_pos[:, None] * freqs[None, :]
    cos = jnp.concatenate([jnp.cos(ang_r), jnp.cos(ang_r), jnp.cos(ang_c), jnp.cos(ang_c)], axis=1)
    sin = jnp.concatenate([-jnp.sin(ang_r), jnp.sin(ang_r), -jnp.sin(ang_c), jnp.sin(ang_c)], axis=1)
    cos = jnp.concatenate([jnp.ones((ctx_len, HEAD_DIM), F32), cos], axis=0)
    sin = jnp.concatenate([jnp.zeros((ctx_len, HEAD_DIM), F32), sin], axis=0)
    cos = jnp.tile(cos, (1, LANES // HEAD_DIM))
    sin = jnp.tile(sin, (1, LANES // HEAD_DIM))
    scale = HEAD_DIM ** -0.5
    return jnp.stack([cos * scale, sin * scale, cos, sin], axis=0)


def kernel(x, c, ctx, c_ctx, ada_w, ada_b, ln_g, ln_b, lru_w_in, lru_conv_w, lru_conv_b, lru_gate_w, lru_gate_b, lru_lambda, lru_w_out, attn_w_qkv, attn_sinks, attn_w_o, router_w, router_b, moe_w_gu, moe_b_gu, moe_w_down, moe_b_down):
    B, S, D = x.shape
    C = ctx.shape[1]
    depth = ada_w.shape[0]
    T = C + S
    assert C % TM == 0 and S % TM == 0 and C % Q_BLOCK == 0
    geom = (T // TM, C // TM, B)
    alpha = (2.0 * depth) ** 0.25
    kv_dim = (attn_w_qkv.shape[2] - D) // 2

    act = jax.nn.silu(jnp.concatenate([c, c_ctx[None, :]], axis=0))
    mods = _ada_params(act, ada_w.reshape(depth * 2, D, 3 * D), ada_b.reshape(depth * 2, 3 * D))
    mods = mods.reshape(depth * 2, SUBLANES, 3, D)
    tab = _rope_tables(S, C)

    z = jnp.concatenate([ctx, x], axis=1).reshape(B * T, D)
    for i in range(depth):
        j = i // 2
        if i % 2 == 0:
            u = _mod_matmul(z, mods, 2 * i, lru_w_in[j].astype(BF16), tn=D, out_dtype=F32, geom=geom)
            y = _lru_mix(u, lru_conv_w[j], lru_conv_b[j], lru_gate_w[j], lru_gate_b[j], lru_lambda[j],
                         nbatch=B, ctx_len=C)
            w_out = lru_w_out[j]
        else:
            qkv = _mod_matmul(z, mods, 2 * i, attn_w_qkv[j].astype(BF16), tn=D + 2 * kv_dim, out_dtype=BF16,
                              geom=geom, rope=(tab, (D, kv_dim)))
            y = _attention(qkv, attn_sinks[j], nbatch=B, ctx_len=C, q_dim=D, kv_dim=kv_dim)
            w_out = attn_w_o[j]
        z = _proj_ln(y, w_out.astype(BF16), z, mods, 2 * i, ln_g[i, 0], ln_b[i, 0], alpha=alpha, geom=geom)
        z = _moe_layer(z, mods, 2 * i + 1, i, router_w[i], router_b[i], moe_w_gu, moe_b_gu,
                       moe_w_down, moe_b_down, ln_g[i, 1], ln_b[i, 1], alpha=alpha, geom=geom)
    return z.reshape(B, T, D)[:, C:, :]
```

```python
import functools
import math

import jax
import jax.numpy as jnp
from jax import lax
from jax.experimental import pallas as pl
from jax.experimental.pallas import tpu as pltpu

GRID_W = 64
LN_EPS = 1e-5
LRU_BLOCK_W = 256
CONV_W = 4
LRU_C = 8.0
HEAD_DIM = 64
GROUP = 8
WINDOW = 128
Q_BLOCK = 128
ROPE_BASE = 10000.0
NEG_INF = -1e30
TOP_K = 4
SWIGLU_LIMIT = 7.0
SWIGLU_ALPHA = 1.702

LANES = 128
SUBLANES = 8
TM = 256
EXPERT_BM = 256
VMEM_LIMIT = 56 * 1024 * 1024

BF16 = jnp.bfloat16
F32 = jnp.float32


def _cparams(sem, vmem=None):
    return pltpu.CompilerParams(dimension_semantics=sem, vmem_limit_bytes=vmem)


def _ada_kernel(cb_ref, w_ref, b_ref, o_ref, acc_ref, *, nrows):
    k = pl.program_id(2)

    @pl.when(k == 0)
    def _():
        acc_ref[...] = jnp.zeros_like(acc_ref)

    tk, tn = w_ref.shape[-2:]
    for j in range(tn // LANES):
        w = w_ref[0, :, j * LANES:(j + 1) * LANES]
        for r in range(nrows):
            p = w * cb_ref[r]
            acc_ref[r, :, j * LANES:(j + 1) * LANES] += p.reshape(tk // SUBLANES, SUBLANES, LANES).sum(axis=0)

    @pl.when(k == pl.num_programs(2) - 1)
    def _():
        o_ref[0] = jnp.zeros(o_ref.shape[1:], F32)
        for r in range(nrows):
            o_ref[0, r:r + 1, :] = acc_ref[r].sum(axis=0, keepdims=True) + b_ref[0]


def _ada_params(act, w, bias):
    R, D = act.shape
    L, _, N = w.shape
    tk, tn = 256, 2048
    cb = jnp.broadcast_to(act[:, :, None], (R, D, LANES))
    return pl.pallas_call(
        functools.partial(_ada_kernel, nrows=R),
        out_shape=jax.ShapeDtypeStruct((L, SUBLANES, N), F32),
        grid=(L, N // tn, D // tk),
        in_specs=[
            pl.BlockSpec((R, tk, LANES), lambda l, n, k: (0, k, 0)),
            pl.BlockSpec((1, tk, tn), lambda l, n, k: (l, k, n)),
            pl.BlockSpec((1, 1, tn), lambda l, n, k: (l, 0, n)),
        ],
        out_specs=pl.BlockSpec((1, SUBLANES, tn), lambda l, n, k: (l, 0, n)),
        scratch_shapes=[pltpu.VMEM((R, SUBLANES, tn), F32)],
        compiler_params=_cparams(("parallel", "parallel", "arbitrary")),
        name="ada_params",
    )(cb, w, bias.reshape(L, 1, N))


def _mod_row(i, tiles_per_batch, ctx_tiles, nbatch):
    return jnp.where(i % tiles_per_batch < ctx_tiles, nbatch, i // tiles_per_batch)


def _rope_chunk(y, cos, sin_signed, first_half):
    rot = jnp.where(first_half, pltpu.roll(y, LANES - 16, axis=1), pltpu.roll(y, 16, axis=1))
    return y * cos + rot * sin_signed


def _modmm_kernel(x_ref, mod_ref, w_ref, *rest, rope_cols):
    if rope_cols is None:
        (o_ref,) = rest
    else:
        tab_ref, o_ref = rest
    shift = mod_ref[0:1, :]
    scale = mod_ref[1:2, :]
    a = (x_ref[...] * (1.0 + scale) + shift).astype(BF16)
    tn = o_ref.shape[1]
    ch = 512 if tn % 512 == 0 else 256
    for c in range(tn // ch):
        acc = jnp.dot(a, w_ref[:, c * ch:(c + 1) * ch].astype(BF16), preferred_element_type=F32)
        if rope_cols is None:
            o_ref[:, c * ch:(c + 1) * ch] = acc.astype(o_ref.dtype)
        else:
            q_cols, k_cols = rope_cols
            lane = lax.broadcasted_iota(jnp.int32, (acc.shape[0], LANES), 1)
            first_half = (lane % 32) < 16
            for s in range(ch // LANES):
                col = c * ch + s * LANES
                y = acc[:, s * LANES:(s + 1) * LANES]
                if col < q_cols:
                    y = _rope_chunk(y, tab_ref[0], tab_ref[1], first_half)
                elif col < q_cols + k_cols:
                    y = _rope_chunk(y, tab_ref[2], tab_ref[3], first_half)
                o_ref[:, col:col + LANES] = y.astype(o_ref.dtype)


def _mod_matmul(x, mods, ls, w, *, tn, out_dtype, geom, rope=None):
    M, K = x.shape
    N = w.shape[1]
    tpb, ctx_tiles, nbatch = geom
    in_specs = [
        pl.BlockSpec((TM, K), lambda j, i: (i, 0)),
        pl.BlockSpec((None, None, 3, K), lambda j, i: (ls, _mod_row(i, tpb, ctx_tiles, nbatch), 0, 0)),
        pl.BlockSpec((K, tn), lambda j, i: (0, j)),
    ]
    args = [x, mods, w]
    rope_cols = None
    if rope is not None:
        tab, rope_cols = rope
        in_specs.append(pl.BlockSpec((4, TM, LANES), lambda j, i: (0, i % tpb, 0)))
        args.append(tab)
    return pl.pallas_call(
        functools.partial(_modmm_kernel, rope_cols=rope_cols),
        out_shape=jax.ShapeDtypeStruct((M, N), out_dtype),
        grid=(N // tn, M // TM),
        in_specs=in_specs,
        out_specs=pl.BlockSpec((TM, tn), lambda j, i: (i, j)),
        compiler_params=_cparams(("parallel", "parallel"), VMEM_LIMIT),
        name="mod_matmul",
    )(*args)


def _layer_norm_rows(z, g, b):
    mu = jnp.mean(z, axis=-1, keepdims=True)
    zc = z - mu
    var = jnp.mean(zc * zc, axis=-1, keepdims=True)
    return zc * lax.rsqrt(var + LN_EPS) * g + b


def _proj_ln_kernel(y_ref, w_ref, x_ref, mod_ref, g_ref, b_ref, o_ref, z_ref, *, alpha):
    gate = mod_ref[2:3, :]
    y = y_ref[...]
    n = o_ref.shape[1]
    ch = 512
    for c in range(n // ch):
        sl = slice(c * ch, (c + 1) * ch)
        acc = jnp.dot(y, w_ref[:, sl].astype(BF16), preferred_element_type=F32)
        z_ref[:, sl] = alpha * x_ref[:, sl] + gate[:, sl] * acc
    o_ref[...] = _layer_norm_rows(z_ref[...], g_ref[...], b_ref[...])


def _proj_ln(y, w, x, mods, ls, ln_g, ln_b, *, alpha, geom):
    M, K = y.shape
    D = w.shape[1]
    tpb, ctx_tiles, nbatch = geom
    return pl.pallas_call(
        functools.partial(_proj_ln_kernel, alpha=alpha),
        out_shape=jax.ShapeDtypeStruct((M, D), F32),
        grid=(M // TM,),
        in_specs=[
            pl.BlockSpec((TM, K), lambda i: (i, 0)),
            pl.BlockSpec((K, D), lambda i: (0, 0)),
            pl.BlockSpec((TM, D), lambda i: (i, 0)),
            pl.BlockSpec((None, None, 3, D), lambda i: (ls, _mod_row(i, tpb, ctx_tiles, nbatch), 0, 0)),
            pl.BlockSpec((1, D), lambda i: (0, 0)),
            pl.BlockSpec((1, D), lambda i: (0, 0)),
        ],
        out_specs=pl.BlockSpec((TM, D), lambda i: (i, 0)),
        scratch_shapes=[pltpu.VMEM((TM, D), F32)],
        compiler_params=_cparams(("parallel",), VMEM_LIMIT),
        name="proj_ln",
    )(y, w, x, mods, ln_g.reshape(1, D), ln_b.reshape(1, D))


def _load_tile_indices(idx_hbm, idx_smem, sem):
    cp = pltpu.make_async_copy(idx_hbm.at[pl.program_id(0)], idx_smem, sem)
    cp.start()
    cp.wait()


def _combine_ln_kernel(dest_hbm, ys_hbm, gate4_ref, x_ref, mod_ref, g_ref, b_ref, o_ref,
                       idx_smem, buf_ref, idx_sem, row_sem, *, alpha):
    _load_tile_indices(dest_hbm, idx_smem, idx_sem)

    def issue(r, carry):
        for k in range(TOP_K):
            row = idx_smem[r * TOP_K + k]
            pltpu.make_async_copy(ys_hbm.at[pl.ds(row, 1)], buf_ref.at[k, pl.ds(r, 1)], row_sem).start()
        return carry

    lax.fori_loop(0, TM, issue, 0, unroll=8)
    for k in range(TOP_K):
        pltpu.make_async_copy(ys_hbm.at[pl.ds(0, TM)], buf_ref.at[k], row_sem).wait()
    gate = mod_ref[2:3, :]
    y = buf_ref[0] * gate4_ref[:, 0:1]
    for k in range(1, TOP_K):
        y = y + buf_ref[k] * gate4_ref[:, k:k + 1]
    z = alpha * x_ref[...] + gate * y
    o_ref[...] = _layer_norm_rows(z, g_ref[...], b_ref[...])


def _combine_ln(dest_tiles, ys, gate4, x, mods, ls, ln_g, ln_b, *, alpha, geom):
    M, D = x.shape
    tpb, ctx_tiles, nbatch = geom
    return pl.pallas_call(
        functools.partial(_combine_ln_kernel, alpha=alpha),
        out_shape=jax.ShapeDtypeStruct((M, D), F32),
        grid=(M // TM,),
        in_specs=[
            pl.BlockSpec(memory_space=pl.ANY),
            pl.BlockSpec(memory_space=pl.ANY),
            pl.BlockSpec((TM, LANES), lambda i: (i, 0)),
            pl.BlockSpec((TM, D), lambda i: (i, 0)),
            pl.BlockSpec((None, None, 3, D), lambda i: (ls, _mod_row(i, tpb, ctx_tiles, nbatch), 0, 0)),
            pl.BlockSpec((1, D), lambda i: (0, 0)),
            pl.BlockSpec((1, D), lambda i: (0, 0)),
        ],
        out_specs=pl.BlockSpec((TM, D), lambda i: (i, 0)),
        scratch_shapes=[
            pltpu.SMEM((TM * TOP_K,), jnp.int32),
            pltpu.VMEM((TOP_K, TM, D), F32),
            pltpu.SemaphoreType.DMA,
            pltpu.SemaphoreType.DMA,
        ],
        compiler_params=_cparams(("arbitrary",), VMEM_LIMIT),
        name="combine_ln",
    )(dest_tiles, ys, gate4, x, mods, ln_g.reshape(1, D), ln_b.reshape(1, D))


def _pack_bf16_halves(a):
    half = a.shape[1] // 2
    lo = lax.bitcast_convert_type(a[:, :half].astype(BF16).astype(F32), jnp.uint32) >> 16
    hi = lax.bitcast_convert_type(a[:, half:].astype(BF16).astype(F32), jnp.uint32) & jnp.uint32(0xFFFF0000)
    return lo | hi


def _unpack_bf16_halves(w):
    lo = lax.bitcast_convert_type(w << 16, F32).astype(BF16)
    hi = lax.bitcast_convert_type(w & jnp.uint32(0xFFFF0000), F32).astype(BF16)
    return lo, hi


def _dispatch_kernel(pad_lo_ref, nused_ref, dest_hbm, x_ref, mod_ref, xs_hbm, idx_smem, a_ref, zero_ref,
                     idx_sem, row_sem, *, bm):
    i = pl.program_id(0)

    @pl.when(i == 0)
    def _():
        zero_ref[...] = jnp.zeros_like(zero_ref)
        n_exp = pad_lo_ref.shape[0]
        n_blk = xs_hbm.shape[0] // bm

        def zero_block(row0):
            return pltpu.make_async_copy(zero_ref, xs_hbm.at[pl.ds(pl.multiple_of(row0, bm), bm)], row_sem)

        for e in range(n_exp):
            @pl.when(pad_lo_ref[e] >= 0)
            def _():
                zero_block(pad_lo_ref[e]).start()

        def fill_tail(b, carry):
            zero_block(b * bm).start()
            return carry

        lax.fori_loop(nused_ref[0], n_blk, fill_tail, 0)
        for e in range(n_exp):
            @pl.when(pad_lo_ref[e] >= 0)
            def _():
                zero_block(0).wait()

        def wait_tail(b, carry):
            zero_block(0).wait()
            return carry

        lax.fori_loop(nused_ref[0], n_blk, wait_tail, 0)

    _load_tile_indices(dest_hbm, idx_smem, idx_sem)
    shift = mod_ref[0:1, :]
    scale = mod_ref[1:2, :]
    a_ref[...] = _pack_bf16_halves(x_ref[...] * (1.0 + scale) + shift)

    def issue(r, carry):
        for k in range(TOP_K):
            row = idx_smem[r * TOP_K + k]
            pltpu.make_async_copy(a_ref.at[pl.ds(r, 1)], xs_hbm.at[pl.ds(row, 1)], row_sem).start()
        return carry

    lax.fori_loop(0, TM, issue, 0, unroll=8)
    for k in range(TOP_K):
        pltpu.make_async_copy(a_ref, xs_hbm.at[pl.ds(0, TM)], row_sem).wait()


def _dispatch(pad_lo, n_used, dest_tiles, x, mods, ls, n_rows, *, geom):
    M, D = x.shape
    tpb, ctx_tiles, nbatch = geom
    bm = EXPERT_BM
    return pl.pallas_call(
        functools.partial(_dispatch_kernel, bm=bm),
        out_shape=jax.ShapeDtypeStruct((n_rows, D // 2), jnp.uint32),
        grid_spec=pltpu.PrefetchScalarGridSpec(
            num_scalar_prefetch=2,
            grid=(M // TM,),
            in_specs=[
                pl.BlockSpec(memory_space=pl.ANY),
                pl.BlockSpec((TM, D), lambda i, p, n: (i, 0)),
                pl.BlockSpec((None, None, 3, D), lambda i, p, n: (ls, _mod_row(i, tpb, ctx_tiles, nbatch), 0, 0)),
            ],
            out_specs=pl.BlockSpec(memory_space=pl.ANY),
            scratch_shapes=[
                pltpu.SMEM((TM * TOP_K,), jnp.int32),
                pltpu.VMEM((TM, D // 2), jnp.uint32),
                pltpu.VMEM((bm, D // 2), jnp.uint32),
                pltpu.SemaphoreType.DMA,
                pltpu.SemaphoreType.DMA,
            ],
        ),
        compiler_params=_cparams(("arbitrary",), VMEM_LIMIT),
        name="moe_dispatch",
    )(pad_lo, n_used, dest_tiles, x, mods)


def _sigmoid(x):
    return 1.0 / (1.0 + jnp.exp(-x))


def _softplus(x):
    return jnp.maximum(x, 0.0) + jnp.log(1.0 + jnp.exp(-jnp.abs(x)))


def _gelu_tanh(x):
    c = math.sqrt(2.0 / math.pi)
    return 0.5 * x * (1.0 + jnp.tanh(c * (x + 0.044715 * (x * x * x))))


def _group_scan(a, b, row, reverse):
    for s in (1, 2, 4):
        if reverse:
            keep = row < SUBLANES - s
            shift = SUBLANES - s
        else:
            keep = row >= s
            shift = s
        a_sh = jnp.where(keep, pltpu.roll(a, shift, axis=0), 1.0)
        b_sh = jnp.where(keep, pltpu.roll(b, shift, axis=0), 0.0)
        b = b + a * b_sh
        a = a * a_sh
    return a, b


def _lru_kernel(ug_ref, ur_ref, cw_ref, cb_ref, wg_ref, bg_ref, lam_ref, o_ref,
                a0_ref, b0_ref, a1_ref, b1_ref, *, ctx_len):
    T, W = ur_ref.shape
    CH = 256
    n_chunks = T // CH
    ctx_chunks = ctx_len // CH
    row_c = lax.broadcasted_iota(jnp.int32, (CH, W), 0)
    sp = _softplus(-lam_ref[...])

    def coeff_chunk(c, carry):
        c0 = pl.multiple_of(c * CH, CH)
        cur = ur_ref[pl.ds(c0, CH), :]
        prev8 = ur_ref[pl.ds(pl.multiple_of(jnp.maximum(c0 - SUBLANES, 0), SUBLANES), SUBLANES), :]
        next8 = ur_ref[pl.ds(pl.multiple_of(jnp.minimum(c0 + CH, T - SUBLANES), SUBLANES), SUBLANES), :]
        seg_start = jnp.logical_or(c == 0, c == ctx_chunks)
        seg_end = jnp.logical_or(c == ctx_chunks - 1, c == n_chunks - 1)
        p_last = jnp.where(seg_start, 0.0, prev8[SUBLANES - 1:SUBLANES, :])
        n_first = jnp.where(seg_end, 0.0, next8[0:1, :])
        n_second = jnp.where(seg_end, 0.0, next8[1:2, :])
        u_m1 = jnp.where(row_c == 0, p_last, pltpu.roll(cur, 1, axis=0))
        u_p1 = jnp.where(row_c == CH - 1, n_first, pltpu.roll(cur, CH - 1, axis=0))
        u_p2 = jnp.where(row_c == CH - 2, n_first,
                         jnp.where(row_c == CH - 1, n_second, pltpu.roll(cur, CH - 2, axis=0)))
        r = cb_ref[...] + u_m1 * cw_ref[0:1, :]
        r = r + cur * cw_ref[1:2, :]
        r = r + u_p1 * cw_ref[2:3, :]
        r = r + u_p2 * cw_ref[3:4, :]
        gts = jnp.dot(r.astype(BF16), wg_ref[0].astype(BF16), preferred_element_type=F32) + bg_ref[0]
        for d, (a_ref, b_ref) in enumerate(((a0_ref, b0_ref), (a1_ref, b1_ref))):
            rg = _sigmoid(gts[:, (2 * d) * W:(2 * d + 1) * W])
            ig = _sigmoid(gts[:, (2 * d + 1) * W:(2 * d + 2) * W])
            log_a = -LRU_C * rg * sp[d:d + 1, :]
            a = jnp.exp(log_a)
            a_ref[pl.ds(c0, CH), :] = a
            b_ref[pl.ds(c0, CH), :] = jnp.sqrt(jnp.maximum(1.0 - a * a, 0.0)) * (ig * r)
        return carry

    lax.fori_loop(0, n_chunks, coeff_chunk, 0)

    row8 = lax.broadcasted_iota(jnp.int32, (SUBLANES, W), 0)
    n_groups = T // SUBLANES
    ctx_groups = ctx_len // SUBLANES

    def scan_step(k, carry):
        hf, hb = carry
        gf = pl.multiple_of(k * SUBLANES, SUBLANES)
        a, b = _group_scan(a0_ref[pl.ds(gf, SUBLANES), :], b0_ref[pl.ds(gf, SUBLANES), :], row8, False)
        h = a * hf + b
        a0_ref[pl.ds(gf, SUBLANES), :] = h
        hf = jnp.broadcast_to(h[SUBLANES - 1:SUBLANES, :], h.shape)
        g = jnp.where(k < ctx_groups, ctx_groups - 1 - k, n_groups - 1 - (k - ctx_groups))
        gb = pl.multiple_of(g * SUBLANES, SUBLANES)
        a, b = _group_scan(a1_ref[pl.ds(gb, SUBLANES), :], b1_ref[pl.ds(gb, SUBLANES), :], row8, True)
        h = a * hb + b
        a1_ref[pl.ds(gb, SUBLANES), :] = h
        hb = jnp.broadcast_to(h[0:1, :], h.shape)
        return hf, hb

    zero = jnp.zeros((SUBLANES, W), F32)
    lax.fori_loop(0, n_groups, scan_step, (zero, zero), unroll=4)

    def out_chunk(c, carry):
        c0 = pl.multiple_of(c * CH, CH)
        y = a0_ref[pl.ds(c0, CH), :] + a1_ref[pl.ds(c0, CH), :]
        o_ref[pl.ds(c0, CH), :] = (y * _gelu_tanh(ug_ref[pl.ds(c0, CH), :])).astype(o_ref.dtype)
        return carry

    lax.fori_loop(0, n_chunks, out_chunk, 0)


def _lru_mix(u, conv_w, conv_b, gate_w, gate_b, lam, *, nbatch, ctx_len):
    M, W2 = u.shape
    W = W2 // 2
    T = M // nbatch
    nblk = W // LRU_BLOCK_W
    bw = LRU_BLOCK_W
    wg = jnp.transpose(gate_w, (2, 3, 0, 1, 4)).reshape(nblk, bw, 4 * bw)
    bg = jnp.transpose(gate_b.reshape(2, 2, nblk, bw), (2, 0, 1, 3)).reshape(nblk, 1, 4 * bw)
    return pl.pallas_call(
        functools.partial(_lru_kernel, ctx_len=ctx_len),
        out_shape=jax.ShapeDtypeStruct((M, W), BF16),
        grid=(nbatch, nblk),
        in_specs=[
            pl.BlockSpec((T, bw), lambda b, n: (b, n)),
            pl.BlockSpec((T, bw), lambda b, n: (b, nblk + n)),
            pl.BlockSpec((CONV_W, bw), lambda b, n: (0, n)),
            pl.BlockSpec((1, bw), lambda b, n: (0, n)),
            pl.BlockSpec((1, bw, 4 * bw), lambda b, n: (n, 0, 0)),
            pl.BlockSpec((1, 1, 4 * bw), lambda b, n: (n, 0, 0)),
            pl.BlockSpec((2, bw), lambda b, n: (0, n)),
        ],
        out_specs=pl.BlockSpec((T, bw), lambda b, n: (b, n)),
        scratch_shapes=[pltpu.VMEM((T, bw), F32) for _ in range(4)],
        compiler_params=_cparams(("parallel", "parallel"), VMEM_LIMIT),
        name="lru_mix",
    )(u, u, conv_w, conv_b.reshape(1, W), wg, bg, lam)


def _attn_kernel(sink_ref, q_ref, k_ref, v_ref, o_ref, k2_ref, v2_ref, p_ref, *, ctx_len, n_kv):
    QB = q_ref.shape[0]
    T = k_ref.shape[0]
    n_lat = (T - ctx_len) // QB
    ctx_blocks = ctx_len // QB
    band = 3 * QB
    nk = ctx_len + band
    pairs = GROUP // 2
    qb = pl.program_id(1)
    j = qb - ctx_blocks
    start = jnp.clip(j - 1, 0, n_lat - 3)
    off = pl.multiple_of(ctx_len + start * QB, QB)
    qpos = j * QB + lax.broadcasted_iota(jnp.int32, (QB, band), 0)
    kpos = start * QB + lax.broadcasted_iota(jnp.int32, (QB, band), 1)
    valid = jnp.logical_and(jnp.abs(qpos - kpos) <= WINDOW, qb >= ctx_blocks)
    lo_half = lax.broadcasted_iota(jnp.int32, (1, LANES), 1) < HEAD_DIM
    nt = (((1,), (1,)), ((), ()))
    for h in range(n_kv):
        slab = slice((h // 2) * LANES, (h // 2 + 1) * LANES)

        def two_blocks(x):
            swapped = jnp.concatenate([x[:, HEAD_DIM:], x[:, :HEAD_DIM]], axis=1)
            own, other = (x, swapped) if h % 2 == 0 else (swapped, x)
            zero = jnp.zeros_like(x)
            return jnp.where(lo_half, own, zero), jnp.where(lo_half, zero, other)

        for dst, src in ((k2_ref, k_ref), (v2_ref, v_ref)):
            top_c, bot_c = two_blocks(src[0:ctx_len, slab])
            top_b, bot_b = two_blocks(src[pl.ds(off, band), slab])
            dst[0:ctx_len, :] = top_c
            dst[ctx_len:nk, :] = top_b
            dst[nk:nk + ctx_len, :] = bot_c
            dst[nk + ctx_len:2 * nk, :] = bot_b
        c_q = h * GROUP * HEAD_DIM
        qs = jnp.concatenate([q_ref[:, c_q + p * LANES:c_q + (p + 1) * LANES] for p in range(pairs)], axis=0)
        s = lax.dot_general(qs, k2_ref[...], nt, preferred_element_type=F32)
        invs = []
        for p in range(pairs):
            rows = slice(p * QB, (p + 1) * QB)
            inv_pair = []
            for half in range(2):
                c0 = half * nk
                sink = sink_ref[h * GROUP + 2 * p + half]
                s_c = s[rows, c0:c0 + ctx_len]
                s_b = jnp.where(valid, s[rows, c0 + ctx_len:c0 + nk], NEG_INF)
                m = jnp.maximum(jnp.maximum(jnp.max(s_c, axis=-1, keepdims=True),
                                            jnp.max(s_b, axis=-1, keepdims=True)), sink)
                p_c = jnp.exp(s_c - m)
                p_b = jnp.exp(s_b - m)
                denom = (jnp.sum(p_c, axis=-1, keepdims=True) + jnp.sum(p_b, axis=-1, keepdims=True)
                         + jnp.exp(sink - m))
                p_ref[rows, c0:c0 + ctx_len] = p_c.astype(BF16)
                p_ref[rows, c0 + ctx_len:c0 + nk] = p_b.astype(BF16)
                inv_pair.append(1.0 / denom)
            invs.append(inv_pair)
        o = jnp.dot(p_ref[...], v2_ref[...], preferred_element_type=F32)
        for p in range(pairs):
            inv = jnp.where(lo_half, invs[p][0], invs[p][1])
            o_ref[:, c_q + p * LANES:c_q + (p + 1) * LANES] = (o[p * QB:(p + 1) * QB] * inv).astype(o_ref.dtype)


def _attention(qkv, sinks, *, nbatch, ctx_len, q_dim, kv_dim):
    M = qkv.shape[0]
    T = M // nbatch
    nqb = T // Q_BLOCK
    kcol = q_dim // kv_dim
    return pl.pallas_call(
        functools.partial(_attn_kernel, ctx_len=ctx_len, n_kv=kv_dim // HEAD_DIM),
        out_shape=jax.ShapeDtypeStruct((M, q_dim), BF16),
        grid_spec=pltpu.PrefetchScalarGridSpec(
            num_scalar_prefetch=1,
            grid=(nbatch, nqb),
            in_specs=[
                pl.BlockSpec((Q_BLOCK, q_dim), lambda b, q, s: (b * nqb + q, 0)),
                pl.BlockSpec((T, kv_dim), lambda b, q, s: (b, kcol)),
                pl.BlockSpec((T, kv_dim), lambda b, q, s: (b, kcol + 1)),
            ],
            out_specs=pl.BlockSpec((Q_BLOCK, q_dim), lambda b, q, s: (b * nqb + q, 0)),
            scratch_shapes=[
                pltpu.VMEM((2 * (ctx_len + 3 * Q_BLOCK), LANES), BF16),
                pltpu.VMEM((2 * (ctx_len + 3 * Q_BLOCK), LANES), BF16),
                pltpu.VMEM((GROUP // 2 * Q_BLOCK, 2 * (ctx_len + 3 * Q_BLOCK)), BF16),
            ],
        ),
        compiler_params=_cparams(("parallel", "parallel"), VMEM_LIMIT),
        name="swa_attention",
    )(sinks, qkv, qkv, qkv)


def _split_bf16(x):
    hi = x.astype(BF16)
    lo = (x - hi.astype(F32)).astype(BF16)
    return hi, lo


def _router_kernel(x_ref, mod_ref, wr_ref, br_ref, e_ref, g_ref, r_ref, cnt_ref, carry_ref):
    i = pl.program_id(0)

    @pl.when(i == 0)
    def _():
        carry_ref[...] = jnp.zeros_like(carry_ref)

    shift = mod_ref[0:1, :]
    scale = mod_ref[1:2, :]
    a = x_ref[...] * (1.0 + scale) + shift
    a_hi, a_lo = _split_bf16(a)
    w_hi, w_lo = _split_bf16(wr_ref[...])
    logits = (jnp.dot(a_hi, w_hi, preferred_element_type=F32)
              + (jnp.dot(a_lo, w_hi, preferred_element_type=F32)
                 + jnp.dot(a_hi, w_lo, preferred_element_type=F32))) + br_ref[...]
    tm, ne = logits.shape
    lane_e = lax.broadcasted_iota(jnp.int32, (tm, ne), 1)
    lane_o = lax.broadcasted_iota(jnp.int32, (tm, LANES), 1)
    vals, idxs, hots = [], [], []
    cur = logits
    for k in range(TOP_K):
        m = jnp.max(cur, axis=-1, keepdims=True)
        idx = jnp.min(jnp.where(cur == m, lane_e, ne), axis=-1, keepdims=True)
        hot = lane_e == idx
        vals.append(m)
        idxs.append(idx)
        hots.append(hot)
        cur = jnp.where(hot, -jnp.inf, cur)
    exps = [jnp.exp(v - vals[0]) for v in vals]
    inv = 1.0 / (exps[0] + exps[1] + exps[2] + exps[3])
    msum = jnp.zeros((tm, ne), F32)
    for hot in hots:
        msum = msum + hot.astype(F32)
    rr = lax.broadcasted_iota(jnp.int32, (tm, tm), 0)
    cc = lax.broadcasted_iota(jnp.int32, (tm, tm), 1)
    tri = (cc < rr).astype(BF16)
    before = jnp.dot(tri, msum.astype(BF16), preferred_element_type=F32) + carry_ref[...]
    e_out = jnp.zeros((tm, LANES), jnp.int32)
    g_out = jnp.zeros((tm, LANES), F32)
    r_out = jnp.zeros((tm, LANES), jnp.int32)
    for k in range(TOP_K):
        rank = jnp.sum(jnp.where(hots[k], before, 0.0), axis=-1, keepdims=True).astype(jnp.int32)
        e_out = jnp.where(lane_o == k, idxs[k], e_out)
        g_out = jnp.where(lane_o == k, exps[k] * inv, g_out)
        r_out = jnp.where(lane_o == k, rank, r_out)
    e_ref[...] = e_out
    g_ref[...] = g_out
    r_ref[...] = r_out
    carry_ref[...] = carry_ref[...] + jnp.sum(msum, axis=0, keepdims=True)
    cnt_ref[...] = carry_ref[...].astype(jnp.int32)


def _router(x, mods, ls, w_r, b_r, *, geom):
    M, D = x.shape
    ne = w_r.shape[1]
    tpb, ctx_tiles, nbatch = geom
    return pl.pallas_call(
        _router_kernel,
        out_shape=(
            jax.ShapeDtypeStruct((M, LANES), jnp.int32),
            jax.ShapeDtypeStruct((M, LANES), F32),
            jax.ShapeDtypeStruct((M, LANES), jnp.int32),
            jax.ShapeDtypeStruct((1, ne), jnp.int32),
        ),
        grid=(M // TM,),
        in_specs=[
            pl.BlockSpec((TM, D), lambda i: (i, 0)),
            pl.BlockSpec((None, None, 3, D), lambda i: (ls, _mod_row(i, tpb, ctx_tiles, nbatch), 0, 0)),
            pl.BlockSpec((D, ne), lambda i: (0, 0)),
            pl.BlockSpec((1, ne), lambda i: (0, 0)),
        ],
        out_specs=(
            pl.BlockSpec((TM, LANES), lambda i: (i, 0)),
            pl.BlockSpec((TM, LANES), lambda i: (i, 0)),
            pl.BlockSpec((TM, LANES), lambda i: (i, 0)),
            pl.BlockSpec((1, ne), lambda i: (0, 0)),
        ),
        scratch_shapes=[pltpu.VMEM((1, ne), F32)],
        compiler_params=_cparams(("arbitrary",), VMEM_LIMIT),
        name="moe_router",
    )(x, mods, w_r, b_r.reshape(1, ne))


def _expert_kernel(blk_e_ref, slot_ref, first_ref, next_e_ref, nused_ref, xs_ref, bgu_ref, bdn_ref,
                   wgu_hbm, wdn_hbm, o_ref, wgu_buf, wdn_buf, sem_gu, sem_dn, *, layer):
    b = pl.program_id(0)
    de, d = wdn_buf.shape[1:]
    half = d // 2

    def fetch(expert, slot):
        return (pltpu.make_async_copy(wgu_hbm.at[layer, expert], wgu_buf.at[slot], sem_gu.at[slot]),
                pltpu.make_async_copy(wdn_hbm.at[layer, expert], wdn_buf.at[slot], sem_dn.at[slot]))

    @pl.when(b < nused_ref[0])
    def _():
        slot = slot_ref[b]

        @pl.when(b == 0)
        def _():
            for cp in fetch(blk_e_ref[0], 0):
                cp.start()

        @pl.when(first_ref[b] == 1)
        def _():
            @pl.when(next_e_ref[b] >= 0)
            def _():
                for cp in fetch(next_e_ref[b], 1 - slot):
                    cp.start()
            for cp in fetch(blk_e_ref[b], slot):
                cp.wait()

        x_lo, x_hi = _unpack_bf16_halves(xs_ref[...])
        hs = []
        for c in range(2 * de // 256):
            sl = slice(c * 256, (c + 1) * 256)
            acc = jnp.dot(x_lo, wgu_buf[slot, 0:half, sl].astype(BF16), preferred_element_type=F32)
            acc = acc + jnp.dot(x_hi, wgu_buf[slot, half:d, sl].astype(BF16), preferred_element_type=F32)
            hs.append(acc + bgu_ref[:, sl])
        h = jnp.concatenate(hs, axis=1)
        g = jnp.minimum(h[:, :de], SWIGLU_LIMIT)
        u = jnp.clip(h[:, de:], -SWIGLU_LIMIT, SWIGLU_LIMIT)
        act = ((u + 1.0) * (g * _sigmoid(SWIGLU_ALPHA * g))).astype(BF16)
        for c in range(d // 512):
            sl = slice(c * 512, (c + 1) * 512)
            o_ref[:, sl] = jnp.dot(act, wdn_buf[slot, :, sl].astype(BF16), preferred_element_type=F32) + bdn_ref[:, sl]

    @pl.when(b >= nused_ref[0])
    def _():
        o_ref[...] = jnp.zeros_like(o_ref)


def _expert_ffn(xs, blk_tabs, n_used, layer, w_gu, b_gu, w_dn, b_dn):
    P = xs.shape[0]
    _, ne, D, de2 = w_gu.shape
    de = de2 // 2
    n_blk = P // EXPERT_BM
    blk_e, slot, first, next_e = blk_tabs
    return pl.pallas_call(
        functools.partial(_expert_kernel, layer=layer),
        out_shape=jax.ShapeDtypeStruct((P, D), F32),
        grid_spec=pltpu.PrefetchScalarGridSpec(
            num_scalar_prefetch=5,
            grid=(n_blk,),
            in_specs=[
                pl.BlockSpec((EXPERT_BM, D // 2), lambda b, e, s, f, x, n: (jnp.minimum(b, n[0] - 1), 0)),
                pl.BlockSpec((None, None, 1, de2), lambda b, e, s, f, x, n: (layer, e[b], 0, 0)),
                pl.BlockSpec((None, None, 1, D), lambda b, e, s, f, x, n: (layer, e[b], 0, 0)),
                pl.BlockSpec(memory_space=pl.ANY),
                pl.BlockSpec(memory_space=pl.ANY),
            ],
            out_specs=pl.BlockSpec((EXPERT_BM, D), lambda b, e, s, f, x, n: (b, 0)),
            scratch_shapes=[
                pltpu.VMEM((2, D, de2), F32),
                pltpu.VMEM((2, de, D), F32),
                pltpu.SemaphoreType.DMA((2,)),
                pltpu.SemaphoreType.DMA((2,)),
            ],
        ),
        compiler_params=_cparams(("arbitrary",), VMEM_LIMIT),
        name="expert_ffn",
    )(blk_e, slot, first, next_e, n_used, xs, b_gu.reshape(b_gu.shape[0], ne, 1, de2),
      b_dn.reshape(b_dn.shape[0], ne, 1, D), w_gu, w_dn)


def _moe_layer(x, mods, ls, layer, w_r, b_r, w_gu, b_gu, w_dn, b_dn, ln_g, ln_b, *, alpha, geom):
    M, D = x.shape
    ne = w_r.shape[1]
    bm = EXPERT_BM
    e128, g128, r128, counts = _router(x, mods, ls, w_r, b_r, geom=geom)
    top_e = e128[:, :TOP_K]
    rank = r128[:, :TOP_K]
    counts = counts[0]
    padded = ((counts + bm - 1) // bm) * bm
    ends = jnp.cumsum(padded)
    pstart = ends - padded
    experts = jnp.arange(ne, dtype=jnp.int32)
    dest = jnp.sum(jnp.where(top_e[..., None] == experts, pstart, 0), axis=-1) + rank
    dest_tiles = dest.astype(jnp.int32).reshape(M // TM, TM * TOP_K)
    A = M * TOP_K
    P = (-(-A // bm)) * bm + ne * bm
    n_blk = P // bm
    n_used = (ends[-1] // bm).astype(jnp.int32).reshape(1)
    blk = jnp.arange(n_blk, dtype=jnp.int32)
    blk_row = jnp.minimum(blk, n_used[0] - 1) * bm
    blk_e = jnp.minimum(jnp.sum(ends[None, :] <= blk_row[:, None], axis=1), ne - 1).astype(jnp.int32)
    first = jnp.logical_and(blk < n_used[0],
                            jnp.concatenate([jnp.ones((1,), bool), blk_e[1:] != blk_e[:-1]])).astype(jnp.int32)
    slot = ((jnp.cumsum(first) - 1) % 2).astype(jnp.int32)
    later = jnp.logical_and(experts[None, :] > experts[:, None], (padded > 0)[None, :])
    next_nonempty = jnp.min(jnp.where(later, experts[None, :], ne), axis=1)
    next_nonempty = jnp.where(next_nonempty == ne, -1, next_nonempty)
    next_e = jnp.sum(jnp.where(blk_e[:, None] == experts[None, :], next_nonempty[None, :], 0), axis=1).astype(jnp.int32)
    pad_lo = jnp.where(padded > 0, ends - bm, -1).astype(jnp.int32)
    xs = _dispatch(pad_lo, n_used, dest_tiles, x, mods, ls, P, geom=geom)
    ys = _expert_ffn(xs, (blk_e, slot, first, next_e), n_used, layer, w_gu, b_gu, w_dn, b_dn)
    return _combine_ln(dest_tiles, ys, g128, x, mods, ls, ln_g, ln_b, alpha=alpha, geom=geom)


def _rope_tables(seq, ctx_len):
    rows = seq // GRID_W
    row_pos = jnp.repeat(jnp.arange(rows), GRID_W).astype(F32)
    col_pos = jnp.tile(jnp.arange(GRID_W), rows).astype(F32)
    n = HEAD_DIM // 2
    freqs = ROPE_BASE ** (-jnp.arange(0, n, 2, dtype=F32) / n)
    ang_r = row_pos[:, None] * freqs[None, :]
    ang_c = col_pos[:, None] * freqs[None, :]
    cos = jnp.concatenate([jnp.cos(ang_r), jnp.cos(ang_r), jnp.cos(ang_c), jnp.cos(ang_c)], axis=1)
    sin = jnp.concatenate([-jnp.sin(ang_r), jnp.sin(ang_r), -jnp.sin(ang_c), jnp.sin(ang_c)], axis=1)
    cos = jnp.concatenate([jnp.ones((ctx_len, HEAD_DIM), F32), cos], axis=0)
    sin = jnp.concatenate([jnp.zeros((ctx_len, HEAD_DIM), F32), sin], axis=0)
    cos = jnp.tile(cos, (1, LANES // HEAD_DIM))
    sin = jnp.tile(sin, (1, LANES // HEAD_DIM))
    scale = HEAD_DIM ** -0.5
    return jnp.stack([cos * scale, sin * scale, cos, sin], axis=0)


def kernel(x, c, ctx, c_ctx, ada_w, ada_b, ln_g, ln_b, lru_w_in, lru_conv_w, lru_conv_b, lru_gate_w, lru_gate_b, lru_lambda, lru_w_out, attn_w_qkv, attn_sinks, attn_w_o, router_w, router_b, moe_w_gu, moe_b_gu, moe_w_down, moe_b_down):
    B, S, D = x.shape
    C = ctx.shape[1]
    depth = ada_w.shape[0]
    T = C + S
    assert C % TM == 0 and S % TM == 0 and C % Q_BLOCK == 0
    geom = (T // TM, C // TM, B)
    alpha = (2.0 * depth) ** 0.25
    kv_dim = (attn_w_qkv.shape[2] - D) // 2

    act = jax.nn.silu(jnp.concatenate([c, c_ctx[None, :]], axis=0))
    mods = _ada_params(act, ada_w.reshape(depth * 2, D, 3 * D), ada_b.reshape(depth * 2, 3 * D))
    mods = mods.reshape(depth * 2, SUBLANES, 3, D)
    tab = _rope_tables(S, C)

    z = jnp.concatenate([ctx, x], axis=1).reshape(B * T, D)
    for i in range(depth):
        j = i // 2
        if i % 2 == 0:
            u = _mod_matmul(z, mods, 2 * i, lru_w_in[j].astype(BF16), tn=D, out_dtype=F32, geom=geom)
            y = _lru_mix(u, lru_conv_w[j], lru_conv_b[j], lru_gate_w[j], lru_gate_b[j], lru_lambda[j],
                         nbatch=B, ctx_len=C)
            w_out = lru_w_out[j]
        else:
            qkv = _mod_matmul(z, mods, 2 * i, attn_w_qkv[j].astype(BF16), tn=D + 2 * kv_dim, out_dtype=BF16,
                              geom=geom, rope=(tab, (D, kv_dim)))
            y = _attention(qkv, attn_sinks[j], nbatch=B, ctx_len=C, q_dim=D, kv_dim=kv_dim)
            w_out = attn_w_o[j]
        z = _proj_ln(y, w_out.astype(BF16), z, mods, 2 * i, ln_g[i, 0], ln_b[i, 0], alpha=alpha, geom=geom)
        z = _moe_layer(z, mods, 2 * i + 1, i, router_w[i], router_b[i], moe_w_gu, moe_b_gu,
                       moe_w_down, moe_b_down, ln_g[i, 1], ln_b[i, 1], alpha=alpha, geom=geom)
    return z.reshape(B, T, D)[:, C:, :]
```

```python
import functools
import math

import jax
import jax.numpy as jnp
from jax import lax
from jax.experimental import pallas as pl
from jax.experimental.pallas import tpu as pltpu

GRID_W = 64
LN_EPS = 1e-5
LRU_BLOCK_W = 256
CONV_W = 4
LRU_C = 8.0
HEAD_DIM = 64
GROUP = 8
WINDOW = 128
Q_BLOCK = 128
ROPE_BASE = 10000.0
NEG_INF = -1e30
TOP_K = 4
SWIGLU_LIMIT = 7.0
SWIGLU_ALPHA = 1.702

LANES = 128
SUBLANES = 8
TM = 256
EXPERT_BM = 256
VMEM_LIMIT = 56 * 1024 * 1024

BF16 = jnp.bfloat16
F32 = jnp.float32


def _cparams(sem, vmem=None):
    return pltpu.CompilerParams(dimension_semantics=sem, vmem_limit_bytes=vmem)


def _ada_kernel(a_ref, w_ref, b_ref, o_ref):
    acc = jnp.dot(a_ref[...], w_ref[0].astype(BF16), preferred_element_type=F32)
    o_ref[0] = acc[0:SUBLANES] + acc[SUBLANES:2 * SUBLANES] + b_ref[0]


def _ada_params(act, w, bias):
    R, D = act.shape
    L, _, N = w.shape
    tn = 1024
    a8 = jnp.zeros((SUBLANES, D), F32).at[:R].set(act)
    hi = a8.astype(BF16)
    lo = (a8 - hi.astype(F32)).astype(BF16)
    a16 = jnp.concatenate([hi, lo], axis=0)
    return pl.pallas_call(
        _ada_kernel,
        out_shape=jax.ShapeDtypeStruct((L, SUBLANES, N), F32),
        grid=(L, N // tn),
        in_specs=[
            pl.BlockSpec((2 * SUBLANES, D), lambda l, n: (0, 0)),
            pl.BlockSpec((1, D, tn), lambda l, n: (l, 0, n)),
            pl.BlockSpec((1, 1, tn), lambda l, n: (l, 0, n)),
        ],
        out_specs=pl.BlockSpec((1, SUBLANES, tn), lambda l, n: (l, 0, n)),
        compiler_params=_cparams(("parallel", "parallel"), VMEM_LIMIT),
        name="ada_params",
    )(a16, w, bias.reshape(L, 1, N))


def _mod_row(i, tiles_per_batch, ctx_tiles, nbatch):
    return jnp.where(i % tiles_per_batch < ctx_tiles, nbatch, i // tiles_per_batch)


def _rope_chunk(y, cos, sin_signed, first_half):
    rot = jnp.where(first_half, pltpu.roll(y, LANES - 16, axis=1), pltpu.roll(y, 16, axis=1))
    return y * cos + rot * sin_signed


def _modmm_kernel(x_ref, mod_ref, w_ref, *rest, rope_cols):
    if rope_cols is None:
        (o_ref,) = rest
    else:
        tab_ref, o_ref = rest
    shift = mod_ref[0:1, :]
    scale = mod_ref[1:2, :]
    a = (x_ref[...] * (1.0 + scale) + shift).astype(BF16)
    tn = o_ref.shape[1]
    ch = 512 if tn % 512 == 0 else 256
    for c in range(tn // ch):
        acc = jnp.dot(a, w_ref[:, c * ch:(c + 1) * ch].astype(BF16), preferred_element_type=F32)
        if rope_cols is None:
            o_ref[:, c * ch:(c + 1) * ch] = acc.astype(o_ref.dtype)
        else:
            q_cols, k_cols = rope_cols
            lane = lax.broadcasted_iota(jnp.int32, (acc.shape[0], LANES), 1)
            first_half = (lane % 32) < 16
            for s in range(ch // LANES):
                col = c * ch + s * LANES
                y = acc[:, s * LANES:(s + 1) * LANES]
                if col < q_cols:
                    y = _rope_chunk(y, tab_ref[0], tab_ref[1], first_half)
                elif col < q_cols + k_cols:
                    y = _rope_chunk(y, tab_ref[2], tab_ref[3], first_half)
                o_ref[:, col:col + LANES] = y.astype(o_ref.dtype)


def _mod_matmul(x, mods, ls, w, *, tn, out_dtype, geom, rope=None):
    M, K = x.shape
    N = w.shape[1]
    tpb, ctx_tiles, nbatch = geom
    in_specs = [
        pl.BlockSpec((TM, K), lambda j, i: (i, 0)),
        pl.BlockSpec((None, None, 3, K), lambda j, i: (ls, _mod_row(i, tpb, ctx_tiles, nbatch), 0, 0)),
        pl.BlockSpec((K, tn), lambda j, i: (0, j)),
    ]
    args = [x, mods, w]
    rope_cols = None
    if rope is not None:
        tab, rope_cols = rope
        in_specs.append(pl.BlockSpec((4, TM, LANES), lambda j, i: (0, i % tpb, 0)))
        args.append(tab)
    return pl.pallas_call(
        functools.partial(_modmm_kernel, rope_cols=rope_cols),
        out_shape=jax.ShapeDtypeStruct((M, N), out_dtype),
        grid=(N // tn, M // TM),
        in_specs=in_specs,
        out_specs=pl.BlockSpec((TM, tn), lambda j, i: (i, j)),
        compiler_params=_cparams(("parallel", "parallel"), VMEM_LIMIT),
        name="mod_matmul",
    )(*args)


def _layer_norm_rows(z, g, b):
    mu = jnp.mean(z, axis=-1, keepdims=True)
    zc = z - mu
    var = jnp.mean(zc * zc, axis=-1, keepdims=True)
    return zc * lax.rsqrt(var + LN_EPS) * g + b


def _proj_ln_kernel(y_ref, w_ref, x_ref, mod_ref, g_ref, b_ref, o_ref, z_ref, *, alpha):
    gate = mod_ref[2:3, :]
    y = y_ref[...]
    n = o_ref.shape[1]
    ch = 512
    for c in range(n // ch):
        sl = slice(c * ch, (c + 1) * ch)
        acc = jnp.dot(y, w_ref[:, sl].astype(BF16), preferred_element_type=F32)
        z_ref[:, sl] = alpha * x_ref[:, sl] + gate[:, sl] * acc
    o_ref[...] = _layer_norm_rows(z_ref[...], g_ref[...], b_ref[...])


def _proj_ln(y, w, x, mods, ls, ln_g, ln_b, *, alpha, geom):
    M, K = y.shape
    D = w.shape[1]
    tpb, ctx_tiles, nbatch = geom
    return pl.pallas_call(
        functools.partial(_proj_ln_kernel, alpha=alpha),
        out_shape=jax.ShapeDtypeStruct((M, D), F32),
        grid=(M // TM,),
        in_specs=[
            pl.BlockSpec((TM, K), lambda i: (i, 0)),
            pl.BlockSpec((K, D), lambda i: (0, 0)),
            pl.BlockSpec((TM, D), lambda i: (i, 0)),
            pl.BlockSpec((None, None, 3, D), lambda i: (ls, _mod_row(i, tpb, ctx_tiles, nbatch), 0, 0)),
            pl.BlockSpec((1, D), lambda i: (0, 0)),
            pl.BlockSpec((1, D), lambda i: (0, 0)),
        ],
        out_specs=pl.BlockSpec((TM, D), lambda i: (i, 0)),
        scratch_shapes=[pltpu.VMEM((TM, D), F32)],
        compiler_params=_cparams(("parallel",), VMEM_LIMIT),
        name="proj_ln",
    )(y, w, x, mods, ln_g.reshape(1, D), ln_b.reshape(1, D))


def _pack_bf16_pair(lo, hi):
    lo_w = lax.bitcast_convert_type(lo.astype(BF16).astype(F32), jnp.uint32) >> 16
    hi_w = lax.bitcast_convert_type(hi.astype(BF16).astype(F32), jnp.uint32) & jnp.uint32(0xFFFF0000)
    return lo_w | hi_w


def _unpack_bf16_pair(w):
    lo = lax.bitcast_convert_type(w << 16, F32)
    hi = lax.bitcast_convert_type(w & jnp.uint32(0xFFFF0000), F32)
    return lo, hi


def _index_copy(idx_hbm, idx_smem, sems, tile):
    return pltpu.make_async_copy(idx_hbm.at[tile], idx_smem.at[tile % 2], sems.at[tile % 2])


def _combine_ln_kernel(dest_hbm, ys_hbm, gate4_ref, x_ref, mod_ref, g_ref, b_ref, o_ref,
                       idx_smem, buf_ref, idx_sems, row_sems, *, alpha):
    i = pl.program_id(0)
    n = pl.num_programs(0)

    def gather_rows(tile):
        slot = tile % 2

        def issue(r, carry):
            for k in range(TOP_K):
                row = idx_smem[slot, r * TOP_K + k]
                pltpu.make_async_copy(ys_hbm.at[pl.ds(row, 1)], buf_ref.at[slot, k, pl.ds(r, 1)],
                                      row_sems.at[slot]).start()
            return carry

        lax.fori_loop(0, TM, issue, 0, unroll=8)

    @pl.when(i == 0)
    def _():
        cp = _index_copy(dest_hbm, idx_smem, idx_sems, 0)
        cp.start()
        cp.wait()
        gather_rows(0)

        @pl.when(n > 1)
        def _():
            _index_copy(dest_hbm, idx_smem, idx_sems, 1).start()

    @pl.when(i + 1 < n)
    def _():
        _index_copy(dest_hbm, idx_smem, idx_sems, i + 1).wait()
        gather_rows(i + 1)

        @pl.when(i + 2 < n)
        def _():
            _index_copy(dest_hbm, idx_smem, idx_sems, i + 2).start()

    slot = i % 2
    for k in range(TOP_K):
        pltpu.make_async_copy(ys_hbm.at[pl.ds(0, TM)], buf_ref.at[slot, k], row_sems.at[slot]).wait()
    y_lo = y_hi = None
    for k in range(TOP_K):
        lo, hi = _unpack_bf16_pair(buf_ref[slot, k])
        gk = gate4_ref[:, k:k + 1]
        y_lo = lo * gk if y_lo is None else y_lo + lo * gk
        y_hi = hi * gk if y_hi is None else y_hi + hi * gk
    y = jnp.concatenate([y_lo, y_hi], axis=1)
    z = alpha * x_ref[...] + mod_ref[2:3, :] * y
    o_ref[...] = _layer_norm_rows(z, g_ref[...], b_ref[...])


def _combine_ln(dest_tiles, ys, gate4, x, mods, ls, ln_g, ln_b, *, alpha, geom):
    M, D = x.shape
    tpb, ctx_tiles, nbatch = geom
    return pl.pallas_call(
        functools.partial(_combine_ln_kernel, alpha=alpha),
        out_shape=jax.ShapeDtypeStruct((M, D), F32),
        grid=(M // TM,),
        in_specs=[
            pl.BlockSpec(memory_space=pl.ANY),
            pl.BlockSpec(memory_space=pl.ANY),
            pl.BlockSpec((TM, LANES), lambda i: (i, 0)),
            pl.BlockSpec((TM, D), lambda i: (i, 0)),
            pl.BlockSpec((None, None, 3, D), lambda i: (ls, _mod_row(i, tpb, ctx_tiles, nbatch), 0, 0)),
            pl.BlockSpec((1, D), lambda i: (0, 0)),
            pl.BlockSpec((1, D), lambda i: (0, 0)),
        ],
        out_specs=pl.BlockSpec((TM, D), lambda i: (i, 0)),
        scratch_shapes=[
            pltpu.SMEM((2, TM * TOP_K), jnp.int32),
            pltpu.VMEM((2, TOP_K, TM, D // 2), jnp.uint32),
            pltpu.SemaphoreType.DMA((2,)),
            pltpu.SemaphoreType.DMA((2,)),
        ],
        compiler_params=_cparams(("arbitrary",), VMEM_LIMIT),
        name="combine_ln",
    )(dest_tiles, ys, gate4, x, mods, ln_g.reshape(1, D), ln_b.reshape(1, D))


def _dispatch_kernel(pad_lo_ref, nused_ref, dest_hbm, x_ref, mod_ref, xs_hbm, idx_smem, a_ref, zero_ref,
                     idx_sems, row_sems, *, bm):
    i = pl.program_id(0)
    n = pl.num_programs(0)
    slot = i % 2

    def wait_rows(s):
        for _ in range(TOP_K):
            pltpu.make_async_copy(a_ref.at[s], xs_hbm.at[pl.ds(0, TM)], row_sems.at[s]).wait()

    @pl.when(i == 0)
    def _():
        _index_copy(dest_hbm, idx_smem, idx_sems, 0).start()
        zero_ref[...] = jnp.zeros_like(zero_ref)
        n_exp = pad_lo_ref.shape[0]
        n_blk = xs_hbm.shape[0] // bm

        def zero_block(row0):
            return pltpu.make_async_copy(zero_ref, xs_hbm.at[pl.ds(pl.multiple_of(row0, bm), bm)],
                                         row_sems.at[0])

        for e in range(n_exp):
            @pl.when(pad_lo_ref[e] >= 0)
            def _():
                zero_block(pad_lo_ref[e]).start()

        def fill_tail(b, carry):
            zero_block(b * bm).start()
            return carry

        lax.fori_loop(nused_ref[0], n_blk, fill_tail, 0)
        for e in range(n_exp):
            @pl.when(pad_lo_ref[e] >= 0)
            def _():
                zero_block(0).wait()

        def wait_tail(b, carry):
            zero_block(0).wait()
            return carry

        lax.fori_loop(nused_ref[0], n_blk, wait_tail, 0)

    @pl.when(i + 1 < n)
    def _():
        _index_copy(dest_hbm, idx_smem, idx_sems, i + 1).start()

    @pl.when(i >= 2)
    def _():
        wait_rows(slot)

    shift = mod_ref[0:1, :]
    scale = mod_ref[1:2, :]
    a = x_ref[...] * (1.0 + scale) + shift
    half = a.shape[1] // 2
    a_ref[slot] = _pack_bf16_pair(a[:, :half], a[:, half:])
    _index_copy(dest_hbm, idx_smem, idx_sems, i).wait()

    def issue(r, carry):
        for k in range(TOP_K):
            row = idx_smem[slot, r * TOP_K + k]
            pltpu.make_async_copy(a_ref.at[slot, pl.ds(r, 1)], xs_hbm.at[pl.ds(row, 1)],
                                  row_sems.at[slot]).start()
        return carry

    lax.fori_loop(0, TM, issue, 0, unroll=8)

    @pl.when(i == n - 1)
    def _():
        wait_rows(slot)

        @pl.when(n > 1)
        def _():
            wait_rows(1 - slot)


def _dispatch(pad_lo, n_used, dest_tiles, x, mods, ls, n_rows, *, geom):
    M, D = x.shape
    tpb, ctx_tiles, nbatch = geom
    bm = EXPERT_BM
    return pl.pallas_call(
        functools.partial(_dispatch_kernel, bm=bm),
        out_shape=jax.ShapeDtypeStruct((n_rows, D // 2), jnp.uint32),
        grid_spec=pltpu.PrefetchScalarGridSpec(
            num_scalar_prefetch=2,
            grid=(M // TM,),
            in_specs=[
                pl.BlockSpec(memory_space=pl.ANY),
                pl.BlockSpec((TM, D), lambda i, p, n: (i, 0)),
                pl.BlockSpec((None, None, 3, D), lambda i, p, n: (ls, _mod_row(i, tpb, ctx_tiles, nbatch), 0, 0)),
            ],
            out_specs=pl.BlockSpec(memory_space=pl.ANY),
            scratch_shapes=[
                pltpu.SMEM((2, TM * TOP_K), jnp.int32),
                pltpu.VMEM((2, TM, D // 2), jnp.uint32),
                pltpu.VMEM((bm, D // 2), jnp.uint32),
                pltpu.SemaphoreType.DMA((2,)),
                pltpu.SemaphoreType.DMA((2,)),
            ],
        ),
        compiler_params=_cparams(("arbitrary",), VMEM_LIMIT),
        name="moe_dispatch",
    )(pad_lo, n_used, dest_tiles, x, mods)


def _sigmoid(x):
    return 0.5 * jnp.tanh(0.5 * x) + 0.5


def _softplus(x):
    return jnp.maximum(x, 0.0) + jnp.log(1.0 + jnp.exp(-jnp.abs(x)))


def _gelu_tanh(x):
    c = math.sqrt(2.0 / math.pi)
    return 0.5 * x * (1.0 + jnp.tanh(c * (x + 0.044715 * (x * x * x))))


def _group_scan(a, b, row, reverse):
    for s in (1, 2, 4):
        if reverse:
            keep = row < SUBLANES - s
            shift = SUBLANES - s
        else:
            keep = row >= s
            shift = s
        a_sh = jnp.where(keep, pltpu.roll(a, shift, axis=0), 1.0)
        b_sh = jnp.where(keep, pltpu.roll(b, shift, axis=0), 0.0)
        b = b + a * b_sh
        a = a * a_sh
    return a, b


def _lru_kernel(ug_ref, ur_ref, cw_ref, cb_ref, wg_ref, bg_ref, lam_ref, o_ref,
                a0_ref, b0_ref, a1_ref, b1_ref, *, ctx_len):
    T, W = ur_ref.shape
    CH = 256
    n_chunks = T // CH
    ctx_chunks = ctx_len // CH
    row_c = lax.broadcasted_iota(jnp.int32, (CH, W), 0)
    sp = _softplus(-lam_ref[...])

    def coeff_chunk(c, carry):
        c0 = pl.multiple_of(c * CH, CH)
        cur = ur_ref[pl.ds(c0, CH), :]
        prev8 = ur_ref[pl.ds(pl.multiple_of(jnp.maximum(c0 - SUBLANES, 0), SUBLANES), SUBLANES), :]
        next8 = ur_ref[pl.ds(pl.multiple_of(jnp.minimum(c0 + CH, T - SUBLANES), SUBLANES), SUBLANES), :]
        seg_start = jnp.logical_or(c == 0, c == ctx_chunks)
        seg_end = jnp.logical_or(c == ctx_chunks - 1, c == n_chunks - 1)
        p_last = jnp.where(seg_start, 0.0, prev8[SUBLANES - 1:SUBLANES, :])
        n_first = jnp.where(seg_end, 0.0, next8[0:1, :])
        n_second = jnp.where(seg_end, 0.0, next8[1:2, :])
        u_m1 = jnp.where(row_c == 0, p_last, pltpu.roll(cur, 1, axis=0))
        u_p1 = jnp.where(row_c == CH - 1, n_first, pltpu.roll(cur, CH - 1, axis=0))
        u_p2 = jnp.where(row_c == CH - 2, n_first,
                         jnp.where(row_c == CH - 1, n_second, pltpu.roll(cur, CH - 2, axis=0)))
        r = cb_ref[...] + u_m1 * cw_ref[0:1, :]
        r = r + cur * cw_ref[1:2, :]
        r = r + u_p1 * cw_ref[2:3, :]
        r = r + u_p2 * cw_ref[3:4, :]
        gts = jnp.dot(r.astype(BF16), wg_ref[0].astype(BF16), preferred_element_type=F32) + bg_ref[0]
        for d, (a_ref, b_ref) in enumerate(((a0_ref, b0_ref), (a1_ref, b1_ref))):
            rg = _sigmoid(gts[:, (2 * d) * W:(2 * d + 1) * W])
            ig = _sigmoid(gts[:, (2 * d + 1) * W:(2 * d + 2) * W])
            log_a = -LRU_C * rg * sp[d:d + 1, :]
            a = jnp.exp(log_a)
            a_ref[pl.ds(c0, CH), :] = a
            b_ref[pl.ds(c0, CH), :] = jnp.sqrt(jnp.maximum(1.0 - a * a, 0.0)) * (ig * r)
        return carry

    lax.fori_loop(0, n_chunks, coeff_chunk, 0)

    row8 = lax.broadcasted_iota(jnp.int32, (SUBLANES, W), 0)
    n_groups = T // SUBLANES
    ctx_groups = ctx_len // SUBLANES

    def scan_step(k, carry):
        hf, hb = carry
        gf = pl.multiple_of(k * SUBLANES, SUBLANES)
        a, b = _group_scan(a0_ref[pl.ds(gf, SUBLANES), :], b0_ref[pl.ds(gf, SUBLANES), :], row8, False)
        h = a * hf + b
        a0_ref[pl.ds(gf, SUBLANES), :] = h
        hf = jnp.broadcast_to(h[SUBLANES - 1:SUBLANES, :], h.shape)
        g = jnp.where(k < ctx_groups, ctx_groups - 1 - k, n_groups - 1 - (k - ctx_groups))
        gb = pl.multiple_of(g * SUBLANES, SUBLANES)
        a, b = _group_scan(a1_ref[pl.ds(gb, SUBLANES), :], b1_ref[pl.ds(gb, SUBLANES), :], row8, True)
        h = a * hb + b
        a1_ref[pl.ds(gb, SUBLANES), :] = h
        hb = jnp.broadcast_to(h[0:1, :], h.shape)
        return hf, hb

    zero = jnp.zeros((SUBLANES, W), F32)
    lax.fori_loop(0, n_groups, scan_step, (zero, zero), unroll=4)

    def out_chunk(c, carry):
        c0 = pl.multiple_of(c * CH, CH)
        y = a0_ref[pl.ds(c0, CH), :] + a1_ref[pl.ds(c0, CH), :]
        o_ref[pl.ds(c0, CH), :] = (y * _gelu_tanh(ug_ref[pl.ds(c0, CH), :])).astype(o_ref.dtype)
        return carry

    lax.fori_loop(0, n_chunks, out_chunk, 0)


def _lru_mix(u, conv_w, conv_b, gate_w, gate_b, lam, *, nbatch, ctx_len):
    M, W2 = u.shape
    W = W2 // 2
    T = M // nbatch
    nblk = W // LRU_BLOCK_W
    bw = LRU_BLOCK_W
    wg = jnp.transpose(gate_w, (2, 3, 0, 1, 4)).reshape(nblk, bw, 4 * bw)
    bg = jnp.transpose(gate_b.reshape(2, 2, nblk, bw), (2, 0, 1, 3)).reshape(nblk, 1, 4 * bw)
    return pl.pallas_call(
        functools.partial(_lru_kernel, ctx_len=ctx_len),
        out_shape=jax.ShapeDtypeStruct((M, W), BF16),
        grid=(nbatch, nblk),
        in_specs=[
            pl.BlockSpec((T, bw), lambda b, n: (b, n)),
            pl.BlockSpec((T, bw), lambda b, n: (b, nblk + n)),
            pl.BlockSpec((CONV_W, bw), lambda b, n: (0, n)),
            pl.BlockSpec((1, bw), lambda b, n: (0, n)),
            pl.BlockSpec((1, bw, 4 * bw), lambda b, n: (n, 0, 0)),
            pl.BlockSpec((1, 1, 4 * bw), lambda b, n: (n, 0, 0)),
            pl.BlockSpec((2, bw), lambda b, n: (0, n)),
        ],
        out_specs=pl.BlockSpec((T, bw), lambda b, n: (b, n)),
        scratch_shapes=[pltpu.VMEM((T, bw), F32) for _ in range(4)],
        compiler_params=_cparams(("parallel", "parallel"), VMEM_LIMIT),
        name="lru_mix",
    )(u, u, conv_w, conv_b.reshape(1, W), wg, bg, lam)


def _attn_kernel(sink_ref, q_ref, k_ref, v_ref, o_ref, k2_ref, v2_ref, p_ref, *, ctx_len, n_kv):
    QB = q_ref.shape[0]
    T = k_ref.shape[0]
    n_lat = (T - ctx_len) // QB
    ctx_blocks = ctx_len // QB
    band = 3 * QB
    nk = ctx_len + band
    pairs = GROUP // 2
    qb = pl.program_id(1)
    j = qb - ctx_blocks
    start = jnp.clip(j - 1, 0, n_lat - 3)
    off = pl.multiple_of(ctx_len + start * QB, QB)
    qpos = j * QB + lax.broadcasted_iota(jnp.int32, (QB, band), 0)
    kpos = start * QB + lax.broadcasted_iota(jnp.int32, (QB, band), 1)
    valid = jnp.logical_and(jnp.abs(qpos - kpos) <= WINDOW, qb >= ctx_blocks)
    lo_half = lax.broadcasted_iota(jnp.int32, (1, LANES), 1) < HEAD_DIM
    nt = (((1,), (1,)), ((), ()))
    for h in range(n_kv):
        slab = slice((h // 2) * LANES, (h // 2 + 1) * LANES)

        def two_blocks(x):
            swapped = jnp.concatenate([x[:, HEAD_DIM:], x[:, :HEAD_DIM]], axis=1)
            own, other = (x, swapped) if h % 2 == 0 else (swapped, x)
            zero = jnp.zeros_like(x)
            return jnp.where(lo_half, own, zero), jnp.where(lo_half, zero, other)

        for dst, src in ((k2_ref, k_ref), (v2_ref, v_ref)):
            top_c, bot_c = two_blocks(src[0:ctx_len, slab])
            top_b, bot_b = two_blocks(src[pl.ds(off, band), slab])
            dst[0:ctx_len, :] = top_c
            dst[ctx_len:nk, :] = top_b
            dst[nk:nk + ctx_len, :] = bot_c
            dst[nk + ctx_len:2 * nk, :] = bot_b
        c_q = h * GROUP * HEAD_DIM
        qs = jnp.concatenate([q_ref[:, c_q + p * LANES:c_q + (p + 1) * LANES] for p in range(pairs)], axis=0)
        s = lax.dot_general(qs, k2_ref[...], nt, preferred_element_type=F32)
        invs = []
        for p in range(pairs):
            rows = slice(p * QB, (p + 1) * QB)
            inv_pair = []
            for half in range(2):
                c0 = half * nk
                sink = sink_ref[h * GROUP + 2 * p + half]
                s_c = s[rows, c0:c0 + ctx_len]
                s_b = jnp.where(valid, s[rows, c0 + ctx_len:c0 + nk], NEG_INF)
                m = jnp.maximum(jnp.maximum(jnp.max(s_c, axis=-1, keepdims=True),
                                            jnp.max(s_b, axis=-1, keepdims=True)), sink)
                p_c = jnp.exp(s_c - m)
                p_b = jnp.exp(s_b - m)
                denom = (jnp.sum(p_c, axis=-1, keepdims=True) + jnp.sum(p_b, axis=-1, keepdims=True)
                         + jnp.exp(sink - m))
                p_ref[rows, c0:c0 + ctx_len] = p_c.astype(BF16)
                p_ref[rows, c0 + ctx_len:c0 + nk] = p_b.astype(BF16)
                inv_pair.append(1.0 / denom)
            invs.append(inv_pair)
        o = jnp.dot(p_ref[...], v2_ref[...], preferred_element_type=F32)
        for p in range(pairs):
            inv = jnp.where(lo_half, invs[p][0], invs[p][1])
            o_ref[:, c_q + p * LANES:c_q + (p + 1) * LANES] = (o[p * QB:(p + 1) * QB] * inv).astype(o_ref.dtype)


def _attention(qkv, sinks, *, nbatch, ctx_len, q_dim, kv_dim):
    M = qkv.shape[0]
    T = M // nbatch
    nqb = T // Q_BLOCK
    kcol = q_dim // kv_dim
    return pl.pallas_call(
        functools.partial(_attn_kernel, ctx_len=ctx_len, n_kv=kv_dim // HEAD_DIM),
        out_shape=jax.ShapeDtypeStruct((M, q_dim), BF16),
        grid_spec=pltpu.PrefetchScalarGridSpec(
            num_scalar_prefetch=1,
            grid=(nbatch, nqb),
            in_specs=[
                pl.BlockSpec((Q_BLOCK, q_dim), lambda b, q, s: (b * nqb + q, 0)),
                pl.BlockSpec((T, kv_dim), lambda b, q, s: (b, kcol)),
                pl.BlockSpec((T, kv_dim), lambda b, q, s: (b, kcol + 1)),
            ],
            out_specs=pl.BlockSpec((Q_BLOCK, q_dim), lambda b, q, s: (b * nqb + q, 0)),
            scratch_shapes=[
                pltpu.VMEM((2 * (ctx_len + 3 * Q_BLOCK), LANES), BF16),
                pltpu.VMEM((2 * (ctx_len + 3 * Q_BLOCK), LANES), BF16),
                pltpu.VMEM((GROUP // 2 * Q_BLOCK, 2 * (ctx_len + 3 * Q_BLOCK)), BF16),
            ],
        ),
        compiler_params=_cparams(("parallel", "parallel"), VMEM_LIMIT),
        name="swa_attention",
    )(sinks, qkv, qkv, qkv)


def _split_bf16(x):
    hi = x.astype(BF16)
    lo = (x - hi.astype(F32)).astype(BF16)
    return hi, lo


def _router_kernel(x_ref, mod_ref, wr_ref, br_ref, e_ref, g_ref, r_ref, cnt_ref, carry_ref):
    i = pl.program_id(0)

    @pl.when(i == 0)
    def _():
        carry_ref[...] = jnp.zeros_like(carry_ref)

    shift = mod_ref[0:1, :]
    scale = mod_ref[1:2, :]
    a = x_ref[...] * (1.0 + scale) + shift
    a_hi, a_lo = _split_bf16(a)
    w_hi, w_lo = _split_bf16(wr_ref[...])
    logits = (jnp.dot(a_hi, w_hi, preferred_element_type=F32)
              + (jnp.dot(a_lo, w_hi, preferred_element_type=F32)
                 + jnp.dot(a_hi, w_lo, preferred_element_type=F32))) + br_ref[...]
    tm, ne = logits.shape
    lane_e = lax.broadcasted_iota(jnp.int32, (tm, ne), 1)
    lane_o = lax.broadcasted_iota(jnp.int32, (tm, LANES), 1)
    vals, idxs, hots = [], [], []
    cur = logits
    for k in range(TOP_K):
        m = jnp.max(cur, axis=-1, keepdims=True)
        idx = jnp.min(jnp.where(cur == m, lane_e, ne), axis=-1, keepdims=True)
        hot = lane_e == idx
        vals.append(m)
        idxs.append(idx)
        hots.append(hot)
        cur = jnp.where(hot, -jnp.inf, cur)
    exps = [jnp.exp(v - vals[0]) for v in vals]
    inv = 1.0 / (exps[0] + exps[1] + exps[2] + exps[3])
    msum = jnp.zeros((tm, ne), F32)
    for hot in hots:
        msum = msum + hot.astype(F32)
    rr = lax.broadcasted_iota(jnp.int32, (tm, tm), 0)
    cc = lax.broadcasted_iota(jnp.int32, (tm, tm), 1)
    tri = (cc < rr).astype(BF16)
    before = jnp.dot(tri, msum.astype(BF16), preferred_element_type=F32) + carry_ref[...]
    e_out = jnp.zeros((tm, LANES), jnp.int32)
    g_out = jnp.zeros((tm, LANES), F32)
    r_out = jnp.zeros((tm, LANES), jnp.int32)
    for k in range(TOP_K):
        rank = jnp.sum(jnp.where(hots[k], before, 0.0), axis=-1, keepdims=True).astype(jnp.int32)
        e_out = jnp.where(lane_o == k, idxs[k], e_out)
        g_out = jnp.where(lane_o == k, exps[k] * inv, g_out)
        r_out = jnp.where(lane_o == k, rank, r_out)
    e_ref[...] = e_out
    g_ref[...] = g_out
    r_ref[...] = r_out
    carry_ref[...] = carry_ref[...] + jnp.sum(msum, axis=0, keepdims=True)
    cnt_ref[...] = carry_ref[...].astype(jnp.int32)


def _router(x, mods, ls, w_r, b_r, *, geom):
    M, D = x.shape
    ne = w_r.shape[1]
    tpb, ctx_tiles, nbatch = geom
    return pl.pallas_call(
        _router_kernel,
        out_shape=(
            jax.ShapeDtypeStruct((M, LANES), jnp.int32),
            jax.ShapeDtypeStruct((M, LANES), F32),
            jax.ShapeDtypeStruct((M, LANES), jnp.int32),
            jax.ShapeDtypeStruct((1, ne), jnp.int32),
        ),
        grid=(M // TM,),
        in_specs=[
            pl.BlockSpec((TM, D), lambda i: (i, 0)),
            pl.BlockSpec((None, None, 3, D), lambda i: (ls, _mod_row(i, tpb, ctx_tiles, nbatch), 0, 0)),
            pl.BlockSpec((D, ne), lambda i: (0, 0)),
            pl.BlockSpec((1, ne), lambda i: (0, 0)),
        ],
        out_specs=(
            pl.BlockSpec((TM, LANES), lambda i: (i, 0)),
            pl.BlockSpec((TM, LANES), lambda i: (i, 0)),
            pl.BlockSpec((TM, LANES), lambda i: (i, 0)),
            pl.BlockSpec((1, ne), lambda i: (0, 0)),
        ),
        scratch_shapes=[pltpu.VMEM((1, ne), F32)],
        compiler_params=_cparams(("arbitrary",), VMEM_LIMIT),
        name="moe_router",
    )(x, mods, w_r, b_r.reshape(1, ne))


def _expert_kernel(blk_e_ref, slot_ref, first_ref, next_e_ref, nused_ref, xs_ref, bgu_ref, bdn_ref,
                   wgu_hbm, wdn_hbm, o_ref, wgu_buf, wdn_buf, sem_gu, sem_dn, *, layer):
    b = pl.program_id(0)
    de, d = wdn_buf.shape[1:]
    half = d // 2

    def fetch(expert, slot):
        return (pltpu.make_async_copy(wgu_hbm.at[layer, expert], wgu_buf.at[slot], sem_gu.at[slot]),
                pltpu.make_async_copy(wdn_hbm.at[layer, expert], wdn_buf.at[slot], sem_dn.at[slot]))

    @pl.when(b < nused_ref[0])
    def _():
        slot = slot_ref[b]

        @pl.when(b == 0)
        def _():
            for cp in fetch(blk_e_ref[0], 0):
                cp.start()

        @pl.when(first_ref[b] == 1)
        def _():
            @pl.when(next_e_ref[b] >= 0)
            def _():
                for cp in fetch(next_e_ref[b], 1 - slot):
                    cp.start()
            for cp in fetch(blk_e_ref[b], slot):
                cp.wait()

        x_lo, x_hi = (v.astype(BF16) for v in _unpack_bf16_pair(xs_ref[...]))
        hs = []
        for c in range(2 * de // 256):
            sl = slice(c * 256, (c + 1) * 256)
            acc = jnp.dot(x_lo, wgu_buf[slot, 0:half, sl].astype(BF16), preferred_element_type=F32)
            acc = acc + jnp.dot(x_hi, wgu_buf[slot, half:d, sl].astype(BF16), preferred_element_type=F32)
            hs.append(acc + bgu_ref[:, sl])
        h = jnp.concatenate(hs, axis=1)
        g = jnp.minimum(h[:, :de], SWIGLU_LIMIT)
        u = jnp.clip(h[:, de:], -SWIGLU_LIMIT, SWIGLU_LIMIT)
        act = ((u + 1.0) * (g * _sigmoid(SWIGLU_ALPHA * g))).astype(BF16)
        ys = []
        for c in range(d // 512):
            sl = slice(c * 512, (c + 1) * 512)
            ys.append(jnp.dot(act, wdn_buf[slot, :, sl].astype(BF16), preferred_element_type=F32) + bdn_ref[:, sl])
        nh = len(ys) // 2
        for c in range(nh):
            o_ref[:, c * 512:(c + 1) * 512] = _pack_bf16_pair(ys[c], ys[c + nh])

    @pl.when(b >= nused_ref[0])
    def _():
        o_ref[...] = jnp.zeros_like(o_ref)


def _expert_ffn(xs, blk_tabs, n_used, layer, w_gu, b_gu, w_dn, b_dn):
    P = xs.shape[0]
    _, ne, D, de2 = w_gu.shape
    de = de2 // 2
    n_blk = P // EXPERT_BM
    blk_e, slot, first, next_e = blk_tabs
    return pl.pallas_call(
        functools.partial(_expert_kernel, layer=layer),
        out_shape=jax.ShapeDtypeStruct((P, D // 2), jnp.uint32),
        grid_spec=pltpu.PrefetchScalarGridSpec(
            num_scalar_prefetch=5,
            grid=(n_blk,),
            in_specs=[
                pl.BlockSpec((EXPERT_BM, D // 2), lambda b, e, s, f, x, n: (jnp.minimum(b, n[0] - 1), 0)),
                pl.BlockSpec((None, None, 1, de2), lambda b, e, s, f, x, n: (layer, e[b], 0, 0)),
                pl.BlockSpec((None, None, 1, D), lambda b, e, s, f, x, n: (layer, e[b], 0, 0)),
                pl.BlockSpec(memory_space=pl.ANY),
                pl.BlockSpec(memory_space=pl.ANY),
            ],
            out_specs=pl.BlockSpec((EXPERT_BM, D // 2), lambda b, e, s, f, x, n: (b, 0)),
            scratch_shapes=[
                pltpu.VMEM((2, D, de2), F32),
                pltpu.VMEM((2, de, D), F32),
                pltpu.SemaphoreType.DMA((2,)),
                pltpu.SemaphoreType.DMA((2,)),
            ],
        ),
        compiler_params=_cparams(("arbitrary",), VMEM_LIMIT),
        name="expert_ffn",
    )(blk_e, slot, first, next_e, n_used, xs, b_gu.reshape(b_gu.shape[0], ne, 1, de2),
      b_dn.reshape(b_dn.shape[0], ne, 1, D), w_gu, w_dn)


def _moe_layer(x, mods, ls, layer, w_r, b_r, w_gu, b_gu, w_dn, b_dn, ln_g, ln_b, *, alpha, geom):
    M, D = x.shape
    ne = w_r.shape[1]
    bm = EXPERT_BM
    e128, g128, r128, counts = _router(x, mods, ls, w_r, b_r, geom=geom)
    top_e = e128[:, :TOP_K]
    rank = r128[:, :TOP_K]
    counts = counts[0]
    padded = ((counts + bm - 1) // bm) * bm
    ends = jnp.cumsum(padded)
    pstart = ends - padded
    experts = jnp.arange(ne, dtype=jnp.int32)
    dest = jnp.sum(jnp.where(top_e[..., None] == experts, pstart, 0), axis=-1) + rank
    dest_tiles = dest.astype(jnp.int32).reshape(M // TM, TM * TOP_K)
    A = M * TOP_K
    P = (-(-A // bm)) * bm + ne * bm
    n_blk = P // bm
    n_used = (ends[-1] // bm).astype(jnp.int32).reshape(1)
    blk = jnp.arange(n_blk, dtype=jnp.int32)
    blk_row = jnp.minimum(blk, n_used[0] - 1) * bm
    blk_e = jnp.minimum(jnp.sum(ends[None, :] <= blk_row[:, None], axis=1), ne - 1).astype(jnp.int32)
    first = jnp.logical_and(blk < n_used[0],
                            jnp.concatenate([jnp.ones((1,), bool), blk_e[1:] != blk_e[:-1]])).astype(jnp.int32)
    slot = ((jnp.cumsum(first) - 1) % 2).astype(jnp.int32)
    later = jnp.logical_and(experts[None, :] > experts[:, None], (padded > 0)[None, :])
    next_nonempty = jnp.min(jnp.where(later, experts[None, :], ne), axis=1)
    next_nonempty = jnp.where(next_nonempty == ne, -1, next_nonempty)
    next_e = jnp.sum(jnp.where(blk_e[:, None] == experts[None, :], next_nonempty[None, :], 0), axis=1).astype(jnp.int32)
    pad_lo = jnp.where(padded > 0, ends - bm, -1).astype(jnp.int32)
    xs = _dispatch(pad_lo, n_used, dest_tiles, x, mods, ls, P, geom=geom)
    ys = _expert_ffn(xs, (blk_e, slot, first, next_e), n_used, layer, w_gu, b_gu, w_dn, b_dn)
    return _combine_ln(dest_tiles, ys, g128, x, mods, ls, ln_g, ln_b, alpha=alpha, geom=geom)


def _rope_tables(seq, ctx_len):
    rows = seq // GRID_W
    row_pos = jnp.repeat(jnp.arange(rows), GRID_W).astype(F32)
    col_pos = jnp.tile(jnp.arange(GRID_W), rows).astype(F32)
    n = HEAD_DIM // 2
    freqs = ROPE_BASE ** (-jnp.arange(0, n, 2, dtype=F32) / n)
    ang_r = row_pos[:, None] * freqs[None, :]
    ang_c = col_pos[:, None] * freqs[None, :]
    cos = jnp.concatenate([jnp.cos(ang_r), jnp.cos(ang_r), jnp.cos(ang_c), jnp.cos(ang_c)], axis=1)
    sin = jnp.concatenate([-jnp.sin(ang_r), jnp.sin(ang_r), -jnp.sin(ang_c), jnp.sin(ang_c)], axis=1)
    cos = jnp.concatenate([jnp.ones((ctx_len, HEAD_DIM), F32), cos], axis=0)
    sin = jnp.concatenate([jnp.zeros((ctx_len, HEAD_DIM), F32), sin], axis=0)
    cos = jnp.tile(cos, (1, LANES // HEAD_DIM))
    sin = jnp.tile(sin, (1, LANES // HEAD_DIM))
    scale = HEAD_DIM ** -0.5
    return jnp.stack([cos * scale, sin * scale, cos, sin], axis=0)


def kernel(x, c, ctx, c_ctx, ada_w, ada_b, ln_g, ln_b, lru_w_in, lru_conv_w, lru_conv_b, lru_gate_w, lru_gate_b, lru_lambda, lru_w_out, attn_w_qkv, attn_sinks, attn_w_o, router_w, router_b, moe_w_gu, moe_b_gu, moe_w_down, moe_b_down):
    B, S, D = x.shape
    C = ctx.shape[1]
    depth = ada_w.shape[0]
    T = C + S
    assert C % TM == 0 and S % TM == 0 and C % Q_BLOCK == 0
    geom = (T // TM, C // TM, B)
    alpha = (2.0 * depth) ** 0.25
    kv_dim = (attn_w_qkv.shape[2] - D) // 2

    act = jax.nn.silu(jnp.concatenate([c, c_ctx[None, :]], axis=0))
    mods = _ada_params(act, ada_w.reshape(depth * 2, D, 3 * D), ada_b.reshape(depth * 2, 3 * D))
    mods = mods.reshape(depth * 2, SUBLANES, 3, D)
    tab = _rope_tables(S, C)

    z = jnp.concatenate([ctx, x], axis=1).reshape(B * T, D)
    for i in range(depth):
        j = i // 2
        if i % 2 == 0:
            u = _mod_matmul(z, mods, 2 * i, lru_w_in[j].astype(BF16), tn=D, out_dtype=F32, geom=geom)
            y = _lru_mix(u, lru_conv_w[j], lru_conv_b[j], lru_gate_w[j], lru_gate_b[j], lru_lambda[j],
                         nbatch=B, ctx_len=C)
            w_out = lru_w_out[j]
        else:
            qkv = _mod_matmul(z, mods, 2 * i, attn_w_qkv[j].astype(BF16), tn=D + 2 * kv_dim, out_dtype=BF16,
                              geom=geom, rope=(tab, (D, kv_dim)))
            y = _attention(qkv, attn_sinks[j], nbatch=B, ctx_len=C, q_dim=D, kv_dim=kv_dim)
            w_out = attn_w_o[j]
        z = _proj_ln(y, w_out.astype(BF16), z, mods, 2 * i, ln_g[i, 0], ln_b[i, 0], alpha=alpha, geom=geom)
        z = _moe_layer(z, mods, 2 * i + 1, i, router_w[i], router_b[i], moe_w_gu, moe_b_gu,
                       moe_w_down, moe_b_down, ln_g[i, 1], ln_b[i, 1], alpha=alpha, geom=geom)
    return z.reshape(B, T, D)[:, C:, :]
```

```python
import functools
import math

import jax
import jax.numpy as jnp
from jax import lax
from jax.experimental import pallas as pl
from jax.experimental.pallas import tpu as pltpu

GRID_W = 64
LN_EPS = 1e-5
LRU_BLOCK_W = 256
CONV_W = 4
LRU_C = 8.0
HEAD_DIM = 64
GROUP = 8
WINDOW = 128
Q_BLOCK = 128
ROPE_BASE = 10000.0
NEG_INF = -1e30
TOP_K = 4
SWIGLU_LIMIT = 7.0
SWIGLU_ALPHA = 1.702

LANES = 128
SUBLANES = 8
TM = 256
EXPERT_BM = 256
WEIGHT_DMA_CHUNKS = 4
VMEM_LIMIT = 56 * 1024 * 1024

BF16 = jnp.bfloat16
F32 = jnp.float32


def _cparams(sem, vmem=None):
    return pltpu.CompilerParams(dimension_semantics=sem, vmem_limit_bytes=vmem)


def _ada_kernel(a_ref, w_ref, b_ref, o_ref):
    acc = jnp.dot(a_ref[...], w_ref[0].astype(BF16), preferred_element_type=F32)
    o_ref[0] = acc[0:SUBLANES] + acc[SUBLANES:2 * SUBLANES] + b_ref[0]


def _ada_params(act, w, bias):
    R, D = act.shape
    L, _, N = w.shape
    tn = 1024
    a8 = jnp.zeros((SUBLANES, D), F32).at[:R].set(act)
    hi = a8.astype(BF16)
    lo = (a8 - hi.astype(F32)).astype(BF16)
    a16 = jnp.concatenate([hi, lo], axis=0)
    return pl.pallas_call(
        _ada_kernel,
        out_shape=jax.ShapeDtypeStruct((L, SUBLANES, N), F32),
        grid=(L, N // tn),
        in_specs=[
            pl.BlockSpec((2 * SUBLANES, D), lambda l, n: (0, 0)),
            pl.BlockSpec((1, D, tn), lambda l, n: (l, 0, n)),
            pl.BlockSpec((1, 1, tn), lambda l, n: (l, 0, n)),
        ],
        out_specs=pl.BlockSpec((1, SUBLANES, tn), lambda l, n: (l, 0, n)),
        compiler_params=_cparams(("parallel", "parallel"), VMEM_LIMIT),
        name="ada_params",
    )(a16, w, bias.reshape(L, 1, N))


def _mod_row(i, tiles_per_batch, ctx_tiles, nbatch):
    return jnp.where(i % tiles_per_batch < ctx_tiles, nbatch, i // tiles_per_batch)


def _token_tiles(geom, latent_only):
    tpb, ctx_tiles, nbatch = geom
    if latent_only:
        lpb = tpb - ctx_tiles
        return lpb * nbatch, (lambda i: (i // lpb) * tpb + ctx_tiles + i % lpb), (lambda i: i // lpb)
    return tpb * nbatch, (lambda i: i), (lambda i: _mod_row(i, tpb, ctx_tiles, nbatch))


def _rope_chunk(y, cos, sin_signed, first_half):
    rot = jnp.where(first_half, pltpu.roll(y, LANES - 16, axis=1), pltpu.roll(y, 16, axis=1))
    return y * cos + rot * sin_signed


def _modmm_kernel(x_ref, mod_ref, w_ref, *rest, rope_cols):
    if rope_cols is None:
        (o_ref,) = rest
    else:
        tab_ref, o_ref = rest
    shift = mod_ref[0:1, :]
    scale = mod_ref[1:2, :]
    a = (x_ref[...] * (1.0 + scale) + shift).astype(BF16)
    tn = o_ref.shape[1]
    ch = 512 if tn % 512 == 0 else 256
    for c in range(tn // ch):
        acc = jnp.dot(a, w_ref[:, c * ch:(c + 1) * ch].astype(BF16), preferred_element_type=F32)
        if rope_cols is None:
            o_ref[:, c * ch:(c + 1) * ch] = acc.astype(o_ref.dtype)
        else:
            q_cols, k_cols = rope_cols
            lane = lax.broadcasted_iota(jnp.int32, (acc.shape[0], LANES), 1)
            first_half = (lane % 32) < 16
            for s in range(ch // LANES):
                col = c * ch + s * LANES
                y = acc[:, s * LANES:(s + 1) * LANES]
                if col < q_cols:
                    y = _rope_chunk(y, tab_ref[0], tab_ref[1], first_half)
                elif col < q_cols + k_cols:
                    y = _rope_chunk(y, tab_ref[2], tab_ref[3], first_half)
                o_ref[:, col:col + LANES] = y.astype(o_ref.dtype)


def _mod_matmul(x, mods, ls, w, *, tn, out_dtype, geom, rope=None):
    M, K = x.shape
    N = w.shape[1]
    tpb, ctx_tiles, nbatch = geom
    in_specs = [
        pl.BlockSpec((TM, K), lambda j, i: (i, 0)),
        pl.BlockSpec((None, None, 3, K), lambda j, i: (ls, _mod_row(i, tpb, ctx_tiles, nbatch), 0, 0)),
        pl.BlockSpec((K, tn), lambda j, i: (0, j)),
    ]
    args = [x, mods, w]
    rope_cols = None
    if rope is not None:
        tab, rope_cols = rope
        in_specs.append(pl.BlockSpec((4, TM, LANES), lambda j, i: (0, i % tpb, 0)))
        args.append(tab)
    return pl.pallas_call(
        functools.partial(_modmm_kernel, rope_cols=rope_cols),
        out_shape=jax.ShapeDtypeStruct((M, N), out_dtype),
        grid=(N // tn, M // TM),
        in_specs=in_specs,
        out_specs=pl.BlockSpec((TM, tn), lambda j, i: (i, j)),
        compiler_params=_cparams(("parallel", "parallel"), VMEM_LIMIT),
        name="mod_matmul",
    )(*args)


def _layer_norm_rows(z, g, b):
    mu = jnp.mean(z, axis=-1, keepdims=True)
    zc = z - mu
    var = jnp.mean(zc * zc, axis=-1, keepdims=True)
    return zc * lax.rsqrt(var + LN_EPS) * g + b


def _proj_ln_kernel(y_ref, w_ref, x_ref, mod_ref, g_ref, b_ref, o_ref, z_ref, *, alpha):
    gate = mod_ref[2:3, :]
    y = y_ref[...]
    n = o_ref.shape[1]
    ch = 512
    for c in range(n // ch):
        sl = slice(c * ch, (c + 1) * ch)
        acc = jnp.dot(y, w_ref[:, sl].astype(BF16), preferred_element_type=F32)
        z_ref[:, sl] = alpha * x_ref[:, sl] + gate[:, sl] * acc
    o_ref[...] = _layer_norm_rows(z_ref[...], g_ref[...], b_ref[...])


def _proj_ln(y, w, x, mods, ls, ln_g, ln_b, *, alpha, geom):
    M, K = y.shape
    D = w.shape[1]
    tpb, ctx_tiles, nbatch = geom
    return pl.pallas_call(
        functools.partial(_proj_ln_kernel, alpha=alpha),
        out_shape=jax.ShapeDtypeStruct((M, D), F32),
        grid=(M // TM,),
        in_specs=[
            pl.BlockSpec((TM, K), lambda i: (i, 0)),
            pl.BlockSpec((K, D), lambda i: (0, 0)),
            pl.BlockSpec((TM, D), lambda i: (i, 0)),
            pl.BlockSpec((None, None, 3, D), lambda i: (ls, _mod_row(i, tpb, ctx_tiles, nbatch), 0, 0)),
            pl.BlockSpec((1, D), lambda i: (0, 0)),
            pl.BlockSpec((1, D), lambda i: (0, 0)),
        ],
        out_specs=pl.BlockSpec((TM, D), lambda i: (i, 0)),
        scratch_shapes=[pltpu.VMEM((TM, D), F32)],
        compiler_params=_cparams(("parallel",), VMEM_LIMIT),
        name="proj_ln",
    )(y, w, x, mods, ln_g.reshape(1, D), ln_b.reshape(1, D))


def _pack_bf16_pair(lo, hi):
    lo_w = lax.bitcast_convert_type(lo.astype(BF16).astype(F32), jnp.uint32) >> 16
    hi_w = lax.bitcast_convert_type(hi.astype(BF16).astype(F32), jnp.uint32) & jnp.uint32(0xFFFF0000)
    return lo_w | hi_w


def _unpack_bf16_pair(w):
    lo = lax.bitcast_convert_type(w << 16, F32)
    hi = lax.bitcast_convert_type(w & jnp.uint32(0xFFFF0000), F32)
    return lo, hi


def _index_copy(idx_hbm, idx_smem, sems, tile):
    return pltpu.make_async_copy(idx_hbm.at[tile], idx_smem.at[tile % 2], sems.at[tile % 2])


def _combine_ln_kernel(dest_hbm, ys_hbm, gate4_ref, x_ref, mod_ref, g_ref, b_ref, o_ref,
                       idx_smem, buf_ref, idx_sems, row_sems, *, alpha):
    i = pl.program_id(0)
    n = pl.num_programs(0)

    def gather_rows(tile):
        slot = tile % 2

        def issue(r, carry):
            for k in range(TOP_K):
                row = idx_smem[slot, r * TOP_K + k]
                pltpu.make_async_copy(ys_hbm.at[pl.ds(row, 1)], buf_ref.at[slot, k, pl.ds(r, 1)],
                                      row_sems.at[slot]).start(priority=k % 2)
            return carry

        lax.fori_loop(0, TM, issue, 0, unroll=8)

    @pl.when(i == 0)
    def _():
        cp = _index_copy(dest_hbm, idx_smem, idx_sems, 0)
        cp.start()
        cp.wait()
        gather_rows(0)

        @pl.when(n > 1)
        def _():
            _index_copy(dest_hbm, idx_smem, idx_sems, 1).start()

    @pl.when(i + 1 < n)
    def _():
        _index_copy(dest_hbm, idx_smem, idx_sems, i + 1).wait()
        gather_rows(i + 1)

        @pl.when(i + 2 < n)
        def _():
            _index_copy(dest_hbm, idx_smem, idx_sems, i + 2).start()

    slot = i % 2
    for k in range(TOP_K):
        pltpu.make_async_copy(ys_hbm.at[pl.ds(0, TM)], buf_ref.at[slot, k], row_sems.at[slot]).wait()
    y_lo = y_hi = None
    for k in range(TOP_K):
        lo, hi = _unpack_bf16_pair(buf_ref[slot, k])
        gk = gate4_ref[:, k:k + 1]
        y_lo = lo * gk if y_lo is None else y_lo + lo * gk
        y_hi = hi * gk if y_hi is None else y_hi + hi * gk
    y = jnp.concatenate([y_lo, y_hi], axis=1)
    z = alpha * x_ref[...] + mod_ref[2:3, :] * y
    o_ref[...] = _layer_norm_rows(z, g_ref[...], b_ref[...])


def _combine_ln(dest_tiles, ys, gate4, x, mods, ls, ln_g, ln_b, *, alpha, geom, latent_only):
    D = x.shape[1]
    n_tiles, row_tile, mod_row = _token_tiles(geom, latent_only)
    return pl.pallas_call(
        functools.partial(_combine_ln_kernel, alpha=alpha),
        out_shape=jax.ShapeDtypeStruct((n_tiles * TM, D), F32),
        grid=(n_tiles,),
        in_specs=[
            pl.BlockSpec(memory_space=pl.ANY),
            pl.BlockSpec(memory_space=pl.ANY),
            pl.BlockSpec((TM, LANES), lambda i: (i, 0)),
            pl.BlockSpec((TM, D), lambda i: (row_tile(i), 0)),
            pl.BlockSpec((None, None, 3, D), lambda i: (ls, mod_row(i), 0, 0)),
            pl.BlockSpec((1, D), lambda i: (0, 0)),
            pl.BlockSpec((1, D), lambda i: (0, 0)),
        ],
        out_specs=pl.BlockSpec((TM, D), lambda i: (i, 0)),
        scratch_shapes=[
            pltpu.SMEM((2, TM * TOP_K), jnp.int32),
            pltpu.VMEM((2, TOP_K, TM, D // 2), jnp.uint32),
            pltpu.SemaphoreType.DMA((2,)),
            pltpu.SemaphoreType.DMA((2,)),
        ],
        compiler_params=_cparams(("arbitrary",), VMEM_LIMIT),
        name="combine_ln",
    )(dest_tiles, ys, gate4, x, mods, ln_g.reshape(1, D), ln_b.reshape(1, D))


def _dispatch_kernel(pad_lo_ref, nused_ref, dest_hbm, x_ref, mod_ref, xs_hbm, idx_smem, a_ref, zero_ref,
                     idx_sems, row_sems, *, bm):
    i = pl.program_id(0)
    n = pl.num_programs(0)
    slot = i % 2

    def wait_rows(s):
        for _ in range(TOP_K):
            pltpu.make_async_copy(a_ref.at[s], xs_hbm.at[pl.ds(0, TM)], row_sems.at[s]).wait()

    @pl.when(i == 0)
    def _():
        _index_copy(dest_hbm, idx_smem, idx_sems, 0).start()
        zero_ref[...] = jnp.zeros_like(zero_ref)
        n_exp = pad_lo_ref.shape[0]
        n_blk = xs_hbm.shape[0] // bm

        def zero_block(row0):
            return pltpu.make_async_copy(zero_ref, xs_hbm.at[pl.ds(pl.multiple_of(row0, bm), bm)],
                                         row_sems.at[0])

        for e in range(n_exp):
            @pl.when(pad_lo_ref[e] >= 0)
            def _():
                zero_block(pad_lo_ref[e]).start()

        def fill_tail(b, carry):
            zero_block(b * bm).start()
            return carry

        lax.fori_loop(nused_ref[0], n_blk, fill_tail, 0)
        for e in range(n_exp):
            @pl.when(pad_lo_ref[e] >= 0)
            def _():
                zero_block(0).wait()

        def wait_tail(b, carry):
            zero_block(0).wait()
            return carry

        lax.fori_loop(nused_ref[0], n_blk, wait_tail, 0)

    @pl.when(i + 1 < n)
    def _():
        _index_copy(dest_hbm, idx_smem, idx_sems, i + 1).start()

    @pl.when(i >= 2)
    def _():
        wait_rows(slot)

    shift = mod_ref[0:1, :]
    scale = mod_ref[1:2, :]
    a = x_ref[...] * (1.0 + scale) + shift
    half = a.shape[1] // 2
    a_ref[slot] = _pack_bf16_pair(a[:, :half], a[:, half:])
    _index_copy(dest_hbm, idx_smem, idx_sems, i).wait()

    def issue(r, carry):
        for k in range(TOP_K):
            row = idx_smem[slot, r * TOP_K + k]
            pltpu.make_async_copy(a_ref.at[slot, pl.ds(r, 1)], xs_hbm.at[pl.ds(row, 1)],
                                  row_sems.at[slot]).start(priority=k % 2)
        return carry

    lax.fori_loop(0, TM, issue, 0, unroll=8)

    @pl.when(i == n - 1)
    def _():
        wait_rows(slot)

        @pl.when(n > 1)
        def _():
            wait_rows(1 - slot)


def _dispatch(pad_lo, n_used, dest_tiles, x, mods, ls, n_rows, *, geom, latent_only):
    D = x.shape[1]
    n_tiles, row_tile, mod_row = _token_tiles(geom, latent_only)
    bm = EXPERT_BM
    return pl.pallas_call(
        functools.partial(_dispatch_kernel, bm=bm),
        out_shape=jax.ShapeDtypeStruct((n_rows, D // 2), jnp.uint32),
        grid_spec=pltpu.PrefetchScalarGridSpec(
            num_scalar_prefetch=2,
            grid=(n_tiles,),
            in_specs=[
                pl.BlockSpec(memory_space=pl.ANY),
                pl.BlockSpec((TM, D), lambda i, p, n: (row_tile(i), 0)),
                pl.BlockSpec((None, None, 3, D), lambda i, p, n: (ls, mod_row(i), 0, 0)),
            ],
            out_specs=pl.BlockSpec(memory_space=pl.ANY),
            scratch_shapes=[
                pltpu.SMEM((2, TM * TOP_K), jnp.int32),
                pltpu.VMEM((2, TM, D // 2), jnp.uint32),
                pltpu.VMEM((bm, D // 2), jnp.uint32),
                pltpu.SemaphoreType.DMA((2,)),
                pltpu.SemaphoreType.DMA((2,)),
            ],
        ),
        compiler_params=_cparams(("arbitrary",), VMEM_LIMIT),
        name="moe_dispatch",
    )(pad_lo, n_used, dest_tiles, x, mods)


def _sigmoid(x):
    return 0.5 * jnp.tanh(0.5 * x) + 0.5


def _softplus(x):
    return jnp.maximum(x, 0.0) + jnp.log(1.0 + jnp.exp(-jnp.abs(x)))


def _gelu_tanh(x):
    c = math.sqrt(2.0 / math.pi)
    return 0.5 * x * (1.0 + jnp.tanh(c * (x + 0.044715 * (x * x * x))))


def _group_scan(a, b, row, reverse):
    for s in (1, 2, 4):
        if reverse:
            keep = row < SUBLANES - s
            shift = SUBLANES - s
        else:
            keep = row >= s
            shift = s
        a_sh = jnp.where(keep, pltpu.roll(a, shift, axis=0), 1.0)
        b_sh = jnp.where(keep, pltpu.roll(b, shift, axis=0), 0.0)
        b = b + a * b_sh
        a = a * a_sh
    return a, b


def _lru_kernel(ug_ref, ur_ref, cw_ref, cb_ref, wg_ref, bg_ref, lam_ref, o_ref,
                a0_ref, b0_ref, a1_ref, b1_ref, *, ctx_len):
    T, W = ur_ref.shape
    CH = 256
    n_chunks = T // CH
    ctx_chunks = ctx_len // CH
    row_c = lax.broadcasted_iota(jnp.int32, (CH, W), 0)
    kexp = (-0.5 * LRU_C * math.log2(math.e)) * _softplus(-lam_ref[...])

    def coeff_chunk(c, carry):
        c0 = pl.multiple_of(c * CH, CH)
        cur = ur_ref[pl.ds(c0, CH), :]
        prev8 = ur_ref[pl.ds(pl.multiple_of(jnp.maximum(c0 - SUBLANES, 0), SUBLANES), SUBLANES), :]
        next8 = ur_ref[pl.ds(pl.multiple_of(jnp.minimum(c0 + CH, T - SUBLANES), SUBLANES), SUBLANES), :]
        seg_start = jnp.logical_or(c == 0, c == ctx_chunks)
        seg_end = jnp.logical_or(c == ctx_chunks - 1, c == n_chunks - 1)
        p_last = jnp.where(seg_start, 0.0, prev8[SUBLANES - 1:SUBLANES, :])
        n_first = jnp.where(seg_end, 0.0, next8[0:1, :])
        n_second = jnp.where(seg_end, 0.0, next8[1:2, :])
        u_m1 = jnp.where(row_c == 0, p_last, pltpu.roll(cur, 1, axis=0))
        u_p1 = jnp.where(row_c == CH - 1, n_first, pltpu.roll(cur, CH - 1, axis=0))
        u_p2 = jnp.where(row_c == CH - 2, n_first,
                         jnp.where(row_c == CH - 1, n_second, pltpu.roll(cur, CH - 2, axis=0)))
        r = cb_ref[...] + u_m1 * cw_ref[0:1, :]
        r = r + cur * cw_ref[1:2, :]
        r = r + u_p1 * cw_ref[2:3, :]
        r = r + u_p2 * cw_ref[3:4, :]
        gts = jnp.dot(r.astype(BF16), wg_ref[0].astype(BF16), preferred_element_type=F32) + bg_ref[0]
        half_r = 0.5 * r
        for d, (a_ref, b_ref) in enumerate(((a0_ref, b0_ref), (a1_ref, b1_ref))):
            t_rec = jnp.tanh(gts[:, (2 * d) * W:(2 * d + 1) * W])
            t_in = jnp.tanh(gts[:, (2 * d + 1) * W:(2 * d + 2) * W])
            kd = kexp[d:d + 1, :]
            a = jnp.exp2(kd * t_rec + kd)
            a_ref[pl.ds(c0, CH), :] = a
            b_ref[pl.ds(c0, CH), :] = jnp.sqrt(jnp.maximum(1.0 - a * a, 0.0)) * (half_r * t_in + half_r)
        return carry

    lax.fori_loop(0, n_chunks, coeff_chunk, 0)

    row8 = lax.broadcasted_iota(jnp.int32, (SUBLANES, W), 0)
    n_groups = T // SUBLANES
    ctx_groups = ctx_len // SUBLANES

    def scan_step(k, carry):
        hf, hb = carry
        gf = pl.multiple_of(k * SUBLANES, SUBLANES)
        a, b = _group_scan(a0_ref[pl.ds(gf, SUBLANES), :], b0_ref[pl.ds(gf, SUBLANES), :], row8, False)
        h = a * hf + b
        a0_ref[pl.ds(gf, SUBLANES), :] = h
        hf = jnp.broadcast_to(h[SUBLANES - 1:SUBLANES, :], h.shape)
        g = jnp.where(k < ctx_groups, ctx_groups - 1 - k, n_groups - 1 - (k - ctx_groups))
        gb = pl.multiple_of(g * SUBLANES, SUBLANES)
        a, b = _group_scan(a1_ref[pl.ds(gb, SUBLANES), :], b1_ref[pl.ds(gb, SUBLANES), :], row8, True)
        h = a * hb + b
        a1_ref[pl.ds(gb, SUBLANES), :] = h
        hb = jnp.broadcast_to(h[0:1, :], h.shape)
        return hf, hb

    zero = jnp.zeros((SUBLANES, W), F32)
    lax.fori_loop(0, n_groups, scan_step, (zero, zero), unroll=4)

    def out_chunk(c, carry):
        c0 = pl.multiple_of(c * CH, CH)
        y = a0_ref[pl.ds(c0, CH), :] + a1_ref[pl.ds(c0, CH), :]
        o_ref[pl.ds(c0, CH), :] = (y * _gelu_tanh(ug_ref[pl.ds(c0, CH), :])).astype(o_ref.dtype)
        return carry

    lax.fori_loop(0, n_chunks, out_chunk, 0)


def _lru_mix(u, conv_w, conv_b, gate_w, gate_b, lam, *, nbatch, ctx_len):
    M, W2 = u.shape
    W = W2 // 2
    T = M // nbatch
    nblk = W // LRU_BLOCK_W
    bw = LRU_BLOCK_W
    wg = 0.5 * jnp.transpose(gate_w, (2, 3, 0, 1, 4)).reshape(nblk, bw, 4 * bw)
    bg = 0.5 * jnp.transpose(gate_b.reshape(2, 2, nblk, bw), (2, 0, 1, 3)).reshape(nblk, 1, 4 * bw)
    return pl.pallas_call(
        functools.partial(_lru_kernel, ctx_len=ctx_len),
        out_shape=jax.ShapeDtypeStruct((M, W), BF16),
        grid=(nbatch, nblk),
        in_specs=[
            pl.BlockSpec((T, bw), lambda b, n: (b, n)),
            pl.BlockSpec((T, bw), lambda b, n: (b, nblk + n)),
            pl.BlockSpec((CONV_W, bw), lambda b, n: (0, n)),
            pl.BlockSpec((1, bw), lambda b, n: (0, n)),
            pl.BlockSpec((1, bw, 4 * bw), lambda b, n: (n, 0, 0)),
            pl.BlockSpec((1, 1, 4 * bw), lambda b, n: (n, 0, 0)),
            pl.BlockSpec((2, bw), lambda b, n: (0, n)),
        ],
        out_specs=pl.BlockSpec((T, bw), lambda b, n: (b, n)),
        scratch_shapes=[pltpu.VMEM((T, bw), F32) for _ in range(4)],
        compiler_params=_cparams(("parallel", "parallel"), VMEM_LIMIT),
        name="lru_mix",
    )(u, u, conv_w, conv_b.reshape(1, W), wg, bg, lam)


def _attn_kernel(sink_ref, q_ref, k_ref, v_ref, o_ref, k2_ref, v2_ref, p_ref, *, ctx_len, n_kv):
    QB = q_ref.shape[0]
    T = k_ref.shape[0]
    n_lat = (T - ctx_len) // QB
    ctx_blocks = ctx_len // QB
    band = 3 * QB
    nk = ctx_len + band
    pairs = GROUP // 2
    qb = pl.program_id(1)
    j = qb - ctx_blocks
    start = jnp.clip(j - 1, 0, n_lat - 3)
    off = pl.multiple_of(ctx_len + start * QB, QB)
    qpos = j * QB + lax.broadcasted_iota(jnp.int32, (QB, band), 0)
    kpos = start * QB + lax.broadcasted_iota(jnp.int32, (QB, band), 1)
    valid = jnp.logical_and(jnp.abs(qpos - kpos) <= WINDOW, qb >= ctx_blocks)
    lo_half = lax.broadcasted_iota(jnp.int32, (1, LANES), 1) < HEAD_DIM
    nt = (((1,), (1,)), ((), ()))
    for h in range(n_kv):
        slab = slice((h // 2) * LANES, (h // 2 + 1) * LANES)

        def two_blocks(x):
            swapped = jnp.concatenate([x[:, HEAD_DIM:], x[:, :HEAD_DIM]], axis=1)
            own, other = (x, swapped) if h % 2 == 0 else (swapped, x)
            zero = jnp.zeros_like(x)
            return jnp.where(lo_half, own, zero), jnp.where(lo_half, zero, other)

        for dst, src in ((k2_ref, k_ref), (v2_ref, v_ref)):
            top_c, bot_c = two_blocks(src[0:ctx_len, slab])
            top_b, bot_b = two_blocks(src[pl.ds(off, band), slab])
            dst[0:ctx_len, :] = top_c
            dst[ctx_len:nk, :] = top_b
            dst[nk:nk + ctx_len, :] = bot_c
            dst[nk + ctx_len:2 * nk, :] = bot_b
        c_q = h * GROUP * HEAD_DIM
        qs = jnp.concatenate([q_ref[:, c_q + p * LANES:c_q + (p + 1) * LANES] for p in range(pairs)], axis=0)
        s = lax.dot_general(qs, k2_ref[...], nt, preferred_element_type=F32)
        invs = []
        for p in range(pairs):
            rows = slice(p * QB, (p + 1) * QB)
            inv_pair = []
            for half in range(2):
                c0 = half * nk
                sink = sink_ref[h * GROUP + 2 * p + half]
                s_c = s[rows, c0:c0 + ctx_len]
                s_b = jnp.where(valid, s[rows, c0 + ctx_len:c0 + nk], NEG_INF)
                m = jnp.maximum(jnp.maximum(jnp.max(s_c, axis=-1, keepdims=True),
                                            jnp.max(s_b, axis=-1, keepdims=True)), sink)
                p_c = jnp.exp(s_c - m)
                p_b = jnp.exp(s_b - m)
                denom = (jnp.sum(p_c, axis=-1, keepdims=True) + jnp.sum(p_b, axis=-1, keepdims=True)
                         + jnp.exp(sink - m))
                p_ref[rows, c0:c0 + ctx_len] = p_c.astype(BF16)
                p_ref[rows, c0 + ctx_len:c0 + nk] = p_b.astype(BF16)
                inv_pair.append(1.0 / denom)
            invs.append(inv_pair)
        o = jnp.dot(p_ref[...], v2_ref[...], preferred_element_type=F32)
        for p in range(pairs):
            inv = jnp.where(lo_half, invs[p][0], invs[p][1])
            o_ref[:, c_q + p * LANES:c_q + (p + 1) * LANES] = (o[p * QB:(p + 1) * QB] * inv).astype(o_ref.dtype)


def _attention(qkv, sinks, *, nbatch, ctx_len, q_dim, kv_dim):
    M = qkv.shape[0]
    T = M // nbatch
    nqb = T // Q_BLOCK
    kcol = q_dim // kv_dim
    return pl.pallas_call(
        functools.partial(_attn_kernel, ctx_len=ctx_len, n_kv=kv_dim // HEAD_DIM),
        out_shape=jax.ShapeDtypeStruct((M, q_dim), BF16),
        grid_spec=pltpu.PrefetchScalarGridSpec(
            num_scalar_prefetch=1,
            grid=(nbatch, nqb),
            in_specs=[
                pl.BlockSpec((Q_BLOCK, q_dim), lambda b, q, s: (b * nqb + q, 0)),
                pl.BlockSpec((T, kv_dim), lambda b, q, s: (b, kcol)),
                pl.BlockSpec((T, kv_dim), lambda b, q, s: (b, kcol + 1)),
            ],
            out_specs=pl.BlockSpec((Q_BLOCK, q_dim), lambda b, q, s: (b * nqb + q, 0)),
            scratch_shapes=[
                pltpu.VMEM((2 * (ctx_len + 3 * Q_BLOCK), LANES), BF16),
                pltpu.VMEM((2 * (ctx_len + 3 * Q_BLOCK), LANES), BF16),
                pltpu.VMEM((GROUP // 2 * Q_BLOCK, 2 * (ctx_len + 3 * Q_BLOCK)), BF16),
            ],
        ),
        compiler_params=_cparams(("parallel", "parallel"), VMEM_LIMIT),
        name="swa_attention",
    )(sinks, qkv, qkv, qkv)


def _split_bf16(x):
    hi = x.astype(BF16)
    lo = (x - hi.astype(F32)).astype(BF16)
    return hi, lo


def _router_kernel(x_ref, mod_ref, wr_ref, br_ref, e_ref, g_ref, r_ref, cnt_ref, carry_ref):
    i = pl.program_id(0)

    @pl.when(i == 0)
    def _():
        carry_ref[...] = jnp.zeros_like(carry_ref)

    shift = mod_ref[0:1, :]
    scale = mod_ref[1:2, :]
    a = x_ref[...] * (1.0 + scale) + shift
    a_hi, a_lo = _split_bf16(a)
    w_hi, w_lo = _split_bf16(wr_ref[...])
    logits = (jnp.dot(a_hi, w_hi, preferred_element_type=F32)
              + (jnp.dot(a_lo, w_hi, preferred_element_type=F32)
                 + jnp.dot(a_hi, w_lo, preferred_element_type=F32))) + br_ref[...]
    tm, ne = logits.shape
    lane_e = lax.broadcasted_iota(jnp.int32, (tm, ne), 1)
    lane_o = lax.broadcasted_iota(jnp.int32, (tm, LANES), 1)
    vals, idxs, hots = [], [], []
    cur = logits
    for k in range(TOP_K):
        m = jnp.max(cur, axis=-1, keepdims=True)
        idx = jnp.min(jnp.where(cur == m, lane_e, ne), axis=-1, keepdims=True)
        hot = lane_e == idx
        vals.append(m)
        idxs.append(idx)
        hots.append(hot)
        cur = jnp.where(hot, -jnp.inf, cur)
    exps = [jnp.exp(v - vals[0]) for v in vals]
    inv = 1.0 / (exps[0] + exps[1] + exps[2] + exps[3])
    msum = jnp.zeros((tm, ne), F32)
    for hot in hots:
        msum = msum + hot.astype(F32)
    rr = lax.broadcasted_iota(jnp.int32, (tm, tm), 0)
    cc = lax.broadcasted_iota(jnp.int32, (tm, tm), 1)
    tri = (cc < rr).astype(BF16)
    before = jnp.dot(tri, msum.astype(BF16), preferred_element_type=F32) + carry_ref[...]
    e_out = jnp.zeros((tm, LANES), jnp.int32)
    g_out = jnp.zeros((tm, LANES), F32)
    r_out = jnp.zeros((tm, LANES), jnp.int32)
    for k in range(TOP_K):
        rank = jnp.sum(jnp.where(hots[k], before, 0.0), axis=-1, keepdims=True).astype(jnp.int32)
        e_out = jnp.where(lane_o == k, idxs[k], e_out)
        g_out = jnp.where(lane_o == k, exps[k] * inv, g_out)
        r_out = jnp.where(lane_o == k, rank, r_out)
    e_ref[...] = e_out
    g_ref[...] = g_out
    r_ref[...] = r_out
    carry_ref[...] = carry_ref[...] + jnp.sum(msum, axis=0, keepdims=True)
    cnt_ref[...] = carry_ref[...].astype(jnp.int32)


def _router(x, mods, ls, w_r, b_r, *, geom, latent_only):
    D = x.shape[1]
    ne = w_r.shape[1]
    n_tiles, row_tile, mod_row = _token_tiles(geom, latent_only)
    M = n_tiles * TM
    return pl.pallas_call(
        _router_kernel,
        out_shape=(
            jax.ShapeDtypeStruct((M, LANES), jnp.int32),
            jax.ShapeDtypeStruct((M, LANES), F32),
            jax.ShapeDtypeStruct((M, LANES), jnp.int32),
            jax.ShapeDtypeStruct((1, ne), jnp.int32),
        ),
        grid=(n_tiles,),
        in_specs=[
            pl.BlockSpec((TM, D), lambda i: (row_tile(i), 0)),
            pl.BlockSpec((None, None, 3, D), lambda i: (ls, mod_row(i), 0, 0)),
            pl.BlockSpec((D, ne), lambda i: (0, 0)),
            pl.BlockSpec((1, ne), lambda i: (0, 0)),
        ],
        out_specs=(
            pl.BlockSpec((TM, LANES), lambda i: (i, 0)),
            pl.BlockSpec((TM, LANES), lambda i: (i, 0)),
            pl.BlockSpec((TM, LANES), lambda i: (i, 0)),
            pl.BlockSpec((1, ne), lambda i: (0, 0)),
        ),
        scratch_shapes=[pltpu.VMEM((1, ne), F32)],
        compiler_params=_cparams(("arbitrary",), VMEM_LIMIT),
        name="moe_router",
    )(x, mods, w_r, b_r.reshape(1, ne))


def _expert_kernel(blk_e_ref, slot_ref, first_ref, next_e_ref, nused_ref, xs_ref, bgu_ref, bdn_ref,
                   wgu_hbm, wdn_hbm, o_ref, wgu_buf, wdn_buf, sem_gu, sem_dn, *, layer):
    b = pl.program_id(0)
    de, d = wdn_buf.shape[1:]
    half = d // 2

    def fetch(expert, slot):
        cps = []
        for hbm, buf, sem, chunks in ((wgu_hbm, wgu_buf, sem_gu, WEIGHT_DMA_CHUNKS),
                                      (wdn_hbm, wdn_buf, sem_dn, WEIGHT_DMA_CHUNKS // 2)):
            rows = buf.shape[1] // chunks
            for c in range(chunks):
                rs = pl.ds(c * rows, rows)
                cps.append(pltpu.make_async_copy(hbm.at[layer, expert, rs], buf.at[slot, rs], sem.at[slot]))
        return cps

    @pl.when(b < nused_ref[0])
    def _():
        slot = slot_ref[b]

        @pl.when(b == 0)
        def _():
            for c, cp in enumerate(fetch(blk_e_ref[0], 0)):
                cp.start(priority=c % 2)

        @pl.when(first_ref[b] == 1)
        def _():
            @pl.when(next_e_ref[b] >= 0)
            def _():
                for c, cp in enumerate(fetch(next_e_ref[b], 1 - slot)):
                    cp.start(priority=c % 2)
            for cp in fetch(blk_e_ref[b], slot):
                cp.wait()

        x_lo, x_hi = (v.astype(BF16) for v in _unpack_bf16_pair(xs_ref[...]))
        hs = []
        for c in range(2 * de // 256):
            sl = slice(c * 256, (c + 1) * 256)
            acc = jnp.dot(x_lo, wgu_buf[slot, 0:half, sl].astype(BF16), preferred_element_type=F32)
            acc = acc + jnp.dot(x_hi, wgu_buf[slot, half:d, sl].astype(BF16), preferred_element_type=F32)
            hs.append(acc + bgu_ref[:, sl])
        h = jnp.concatenate(hs, axis=1)
        g = jnp.minimum(h[:, :de], SWIGLU_LIMIT)
        u = jnp.clip(h[:, de:], -SWIGLU_LIMIT, SWIGLU_LIMIT)
        act = ((u + 1.0) * (g * _sigmoid(SWIGLU_ALPHA * g))).astype(BF16)
        ys = []
        for c in range(d // 512):
            sl = slice(c * 512, (c + 1) * 512)
            ys.append(jnp.dot(act, wdn_buf[slot, :, sl].astype(BF16), preferred_element_type=F32) + bdn_ref[:, sl])
        nh = len(ys) // 2
        for c in range(nh):
            o_ref[:, c * 512:(c + 1) * 512] = _pack_bf16_pair(ys[c], ys[c + nh])

    @pl.when(b >= nused_ref[0])
    def _():
        o_ref[...] = jnp.zeros_like(o_ref)


def _expert_ffn(xs, blk_tabs, n_used, layer, w_gu, b_gu, w_dn, b_dn):
    P = xs.shape[0]
    _, ne, D, de2 = w_gu.shape
    de = de2 // 2
    n_blk = P // EXPERT_BM
    blk_e, slot, first, next_e = blk_tabs
    return pl.pallas_call(
        functools.partial(_expert_kernel, layer=layer),
        out_shape=jax.ShapeDtypeStruct((P, D // 2), jnp.uint32),
        grid_spec=pltpu.PrefetchScalarGridSpec(
            num_scalar_prefetch=5,
            grid=(n_blk,),
            in_specs=[
                pl.BlockSpec((EXPERT_BM, D // 2), lambda b, e, s, f, x, n: (jnp.minimum(b, n[0] - 1), 0)),
                pl.BlockSpec((None, None, 1, de2), lambda b, e, s, f, x, n: (layer, e[b], 0, 0)),
                pl.BlockSpec((None, None, 1, D), lambda b, e, s, f, x, n: (layer, e[b], 0, 0)),
                pl.BlockSpec(memory_space=pl.ANY),
                pl.BlockSpec(memory_space=pl.ANY),
            ],
            out_specs=pl.BlockSpec((EXPERT_BM, D // 2), lambda b, e, s, f, x, n: (b, 0)),
            scratch_shapes=[
                pltpu.VMEM((2, D, de2), F32),
                pltpu.VMEM((2, de, D), F32),
                pltpu.SemaphoreType.DMA((2,)),
                pltpu.SemaphoreType.DMA((2,)),
            ],
        ),
        compiler_params=_cparams(("arbitrary",), VMEM_LIMIT),
        name="expert_ffn",
    )(blk_e, slot, first, next_e, n_used, xs, b_gu.reshape(b_gu.shape[0], ne, 1, de2),
      b_dn.reshape(b_dn.shape[0], ne, 1, D), w_gu, w_dn)


def _moe_layer(x, mods, ls, layer, w_r, b_r, w_gu, b_gu, w_dn, b_dn, ln_g, ln_b, *, alpha, geom, latent_only):
    ne = w_r.shape[1]
    bm = EXPERT_BM
    e128, g128, r128, counts = _router(x, mods, ls, w_r, b_r, geom=geom, latent_only=latent_only)
    M = e128.shape[0]
    top_e = e128[:, :TOP_K]
    rank = r128[:, :TOP_K]
    counts = counts[0]
    padded = ((counts + bm - 1) // bm) * bm
    ends = jnp.cumsum(padded)
    pstart = ends - padded
    experts = jnp.arange(ne, dtype=jnp.int32)
    dest = jnp.sum(jnp.where(top_e[..., None] == experts, pstart, 0), axis=-1) + rank
    dest_tiles = dest.astype(jnp.int32).reshape(M // TM, TM * TOP_K)
    A = M * TOP_K
    P = (-(-A // bm)) * bm + ne * bm
    n_blk = P // bm
    n_used = (ends[-1] // bm).astype(jnp.int32).reshape(1)
    blk = jnp.arange(n_blk, dtype=jnp.int32)
    blk_row = jnp.minimum(blk, n_used[0] - 1) * bm
    blk_e = jnp.minimum(jnp.sum(ends[None, :] <= blk_row[:, None], axis=1), ne - 1).astype(jnp.int32)
    first = jnp.logical_and(blk < n_used[0],
                            jnp.concatenate([jnp.ones((1,), bool), blk_e[1:] != blk_e[:-1]])).astype(jnp.int32)
    slot = ((jnp.cumsum(first) - 1) % 2).astype(jnp.int32)
    later = jnp.logical_and(experts[None, :] > experts[:, None], (padded > 0)[None, :])
    next_nonempty = jnp.min(jnp.where(later, experts[None, :], ne), axis=1)
    next_nonempty = jnp.where(next_nonempty == ne, -1, next_nonempty)
    next_e = jnp.sum(jnp.where(blk_e[:, None] == experts[None, :], next_nonempty[None, :], 0), axis=1).astype(jnp.int32)
    pad_lo = jnp.where(padded > 0, ends - bm, -1).astype(jnp.int32)
    xs = _dispatch(pad_lo, n_used, dest_tiles, x, mods, ls, P, geom=geom, latent_only=latent_only)
    ys = _expert_ffn(xs, (blk_e, slot, first, next_e), n_used, layer, w_gu, b_gu, w_dn, b_dn)
    return _combine_ln(dest_tiles, ys, g128, x, mods, ls, ln_g, ln_b, alpha=alpha, geom=geom,
                       latent_only=latent_only)


def _rope_tables(seq, ctx_len):
    rows = seq // GRID_W
    row_pos = jnp.repeat(jnp.arange(rows), GRID_W).astype(F32)
    col_pos = jnp.tile(jnp.arange(GRID_W), rows).astype(F32)
    n = HEAD_DIM // 2
    freqs = ROPE_BASE ** (-jnp.arange(0, n, 2, dtype=F32) / n)
    ang_r = row_pos[:, None] * freqs[None, :]
    ang_c = col_pos[:, None] * freqs[None, :]
    cos = jnp.concatenate([jnp.cos(ang_r), jnp.cos(ang_r), jnp.cos(ang_c), jnp.cos(ang_c)], axis=1)
    sin = jnp.concatenate([-jnp.sin(ang_r), jnp.sin(ang_r), -jnp.sin(ang_c), jnp.sin(ang_c)], axis=1)
    cos = jnp.concatenate([jnp.ones((ctx_len, HEAD_DIM), F32), cos], axis=0)
    sin = jnp.concatenate([jnp.zeros((ctx_len, HEAD_DIM), F32), sin], axis=0)
    cos = jnp.tile(cos, (1, LANES // HEAD_DIM))
    sin = jnp.tile(sin, (1, LANES // HEAD_DIM))
    scale = HEAD_DIM ** -0.5
    return jnp.stack([cos * scale, sin * scale, cos, sin], axis=0)


def kernel(x, c, ctx, c_ctx, ada_w, ada_b, ln_g, ln_b, lru_w_in, lru_conv_w, lru_conv_b, lru_gate_w, lru_gate_b, lru_lambda, lru_w_out, attn_w_qkv, attn_sinks, attn_w_o, router_w, router_b, moe_w_gu, moe_b_gu, moe_w_down, moe_b_down):
    B, S, D = x.shape
    C = ctx.shape[1]
    depth = ada_w.shape[0]
    T = C + S
    assert C % TM == 0 and S % TM == 0 and C % Q_BLOCK == 0
    geom = (T // TM, C // TM, B)
    alpha = (2.0 * depth) ** 0.25
    kv_dim = (attn_w_qkv.shape[2] - D) // 2

    act = jax.nn.silu(jnp.concatenate([c, c_ctx[None, :]], axis=0))
    mods = _ada_params(act, ada_w.reshape(depth * 2, D, 3 * D), ada_b.reshape(depth * 2, 3 * D))
    mods = mods.reshape(depth * 2, SUBLANES, 3, D)
    tab = _rope_tables(S, C)

    z = jnp.concatenate([ctx, x], axis=1).reshape(B * T, D)
    for i in range(depth):
        j = i // 2
        if i % 2 == 0:
            u = _mod_matmul(z, mods, 2 * i, lru_w_in[j].astype(BF16), tn=D, out_dtype=F32, geom=geom)
            y = _lru_mix(u, lru_conv_w[j], lru_conv_b[j], lru_gate_w[j], lru_gate_b[j], lru_lambda[j],
                         nbatch=B, ctx_len=C)
            w_out = lru_w_out[j]
        else:
            qkv = _mod_matmul(z, mods, 2 * i, attn_w_qkv[j].astype(BF16), tn=D + 2 * kv_dim, out_dtype=BF16,
                              geom=geom, rope=(tab, (D, kv_dim)))
            y = _attention(qkv, attn_sinks[j], nbatch=B, ctx_len=C, q_dim=D, kv_dim=kv_dim)
            w_out = attn_w_o[j]
        z = _proj_ln(y, w_out.astype(BF16), z, mods, 2 * i, ln_g[i, 0], ln_b[i, 0], alpha=alpha, geom=geom)
        z = _moe_layer(z, mods, 2 * i + 1, i, router_w[i], router_b[i], moe_w_gu, moe_b_gu,
                       moe_w_down, moe_b_down, ln_g[i, 1], ln_b[i, 1], alpha=alpha, geom=geom,
                       latent_only=(i == depth - 1))
    return z.reshape(B, S, D)
```

```python
import functools
import math

import jax
import jax.numpy as jnp
from jax import lax
from jax.experimental import pallas as pl
from jax.experimental.pallas import tpu as pltpu

GRID_W = 64
LN_EPS = 1e-5
LRU_BLOCK_W = 256
CONV_W = 4
LRU_C = 8.0
HEAD_DIM = 64
GROUP = 8
WINDOW = 128
Q_BLOCK = 128
ROPE_BASE = 10000.0
NEG_INF = -1e30
TOP_K = 4
SWIGLU_LIMIT = 7.0
SWIGLU_ALPHA = 1.702

LANES = 128
SUBLANES = 8
TM = 256
EXPERT_BM = 256
WEIGHT_DMA_CHUNKS = 4
VMEM_LIMIT = 56 * 1024 * 1024

BF16 = jnp.bfloat16
F32 = jnp.float32


def _cparams(sem, vmem=None):
    return pltpu.CompilerParams(dimension_semantics=sem, vmem_limit_bytes=vmem)


def _ada_kernel(a_ref, w_ref, b_ref, o_ref):
    acc = jnp.dot(a_ref[...], w_ref[0].astype(BF16), preferred_element_type=F32)
    o_ref[0] = acc[0:SUBLANES] + acc[SUBLANES:2 * SUBLANES] + b_ref[0]


def _ada_params(act, w, bias):
    R, D = act.shape
    L, _, N = w.shape
    tn = 1024
    a8 = jnp.zeros((SUBLANES, D), F32).at[:R].set(act)
    hi = a8.astype(BF16)
    lo = (a8 - hi.astype(F32)).astype(BF16)
    a16 = jnp.concatenate([hi, lo], axis=0)
    return pl.pallas_call(
        _ada_kernel,
        out_shape=jax.ShapeDtypeStruct((L, SUBLANES, N), F32),
        grid=(L, N // tn),
        in_specs=[
            pl.BlockSpec((2 * SUBLANES, D), lambda l, n: (0, 0)),
            pl.BlockSpec((1, D, tn), lambda l, n: (l, 0, n)),
            pl.BlockSpec((1, 1, tn), lambda l, n: (l, 0, n)),
        ],
        out_specs=pl.BlockSpec((1, SUBLANES, tn), lambda l, n: (l, 0, n)),
        compiler_params=_cparams(("parallel", "parallel"), VMEM_LIMIT),
        name="ada_params",
    )(a16, w, bias.reshape(L, 1, N))


def _mod_row(i, tiles_per_batch, ctx_tiles, nbatch):
    return jnp.where(i % tiles_per_batch < ctx_tiles, nbatch, i // tiles_per_batch)


def _token_tiles(geom, latent_only):
    tpb, ctx_tiles, nbatch = geom
    if latent_only:
        lpb = tpb - ctx_tiles
        return lpb * nbatch, (lambda i: (i // lpb) * tpb + ctx_tiles + i % lpb), (lambda i: i // lpb)
    return tpb * nbatch, (lambda i: i), (lambda i: _mod_row(i, tpb, ctx_tiles, nbatch))


def _rope_chunk(y, cos, sin_signed, first_half):
    rot = jnp.where(first_half, pltpu.roll(y, LANES - 16, axis=1), pltpu.roll(y, 16, axis=1))
    return y * cos + rot * sin_signed


def _modmm_kernel(x_ref, mod_ref, w_ref, *rest, rope_cols):
    if rope_cols is None:
        (o_ref,) = rest
    else:
        tab_ref, o_ref = rest
    shift = mod_ref[0:1, :]
    scale = mod_ref[1:2, :]
    a = (x_ref[...] * (1.0 + scale) + shift).astype(BF16)
    tn = o_ref.shape[1]
    ch = 512 if tn % 512 == 0 else 256
    for c in range(tn // ch):
        acc = jnp.dot(a, w_ref[:, c * ch:(c + 1) * ch].astype(BF16), preferred_element_type=F32)
        if rope_cols is None:
            o_ref[:, c * ch:(c + 1) * ch] = acc.astype(o_ref.dtype)
        else:
            q_cols, k_cols = rope_cols
            lane = lax.broadcasted_iota(jnp.int32, (acc.shape[0], LANES), 1)
            first_half = (lane % 32) < 16
            for s in range(ch // LANES):
                col = c * ch + s * LANES
                y = acc[:, s * LANES:(s + 1) * LANES]
                if col < q_cols:
                    y = _rope_chunk(y, tab_ref[0], tab_ref[1], first_half)
                elif col < q_cols + k_cols:
                    y = _rope_chunk(y, tab_ref[2], tab_ref[3], first_half)
                o_ref[:, col:col + LANES] = y.astype(o_ref.dtype)


def _mod_matmul(x, mods, ls, w, *, tn, out_dtype, geom, rope=None):
    M, K = x.shape
    N = w.shape[1]
    tpb, ctx_tiles, nbatch = geom
    in_specs = [
        pl.BlockSpec((TM, K), lambda j, i: (i, 0)),
        pl.BlockSpec((None, None, 3, K), lambda j, i: (ls, _mod_row(i, tpb, ctx_tiles, nbatch), 0, 0)),
        pl.BlockSpec((K, tn), lambda j, i: (0, j)),
    ]
    args = [x, mods, w]
    rope_cols = None
    if rope is not None:
        tab, rope_cols = rope
        in_specs.append(pl.BlockSpec((4, TM, LANES), lambda j, i: (0, i % tpb, 0)))
        args.append(tab)
    return pl.pallas_call(
        functools.partial(_modmm_kernel, rope_cols=rope_cols),
        out_shape=jax.ShapeDtypeStruct((M, N), out_dtype),
        grid=(N // tn, M // TM),
        in_specs=in_specs,
        out_specs=pl.BlockSpec((TM, tn), lambda j, i: (i, j)),
        compiler_params=_cparams(("parallel", "parallel"), VMEM_LIMIT),
        name="mod_matmul",
    )(*args)


def _layer_norm_rows(z, g, b):
    mu = jnp.mean(z, axis=-1, keepdims=True)
    zc = z - mu
    var = jnp.mean(zc * zc, axis=-1, keepdims=True)
    return zc * lax.rsqrt(var + LN_EPS) * g + b


def _proj_ln_kernel(y_ref, w_ref, x_ref, mod_ref, g_ref, b_ref, o_ref, z_ref, *, alpha):
    gate = mod_ref[2:3, :]
    y = y_ref[...]
    n = o_ref.shape[1]
    ch = 512
    for c in range(n // ch):
        sl = slice(c * ch, (c + 1) * ch)
        acc = jnp.dot(y, w_ref[:, sl].astype(BF16), preferred_element_type=F32)
        z_ref[:, sl] = alpha * x_ref[:, sl] + gate[:, sl] * acc
    o_ref[...] = _layer_norm_rows(z_ref[...], g_ref[...], b_ref[...])


def _proj_ln(y, w, x, mods, ls, ln_g, ln_b, *, alpha, geom):
    M, K = y.shape
    D = w.shape[1]
    tpb, ctx_tiles, nbatch = geom
    return pl.pallas_call(
        functools.partial(_proj_ln_kernel, alpha=alpha),
        out_shape=jax.ShapeDtypeStruct((M, D), F32),
        grid=(M // TM,),
        in_specs=[
            pl.BlockSpec((TM, K), lambda i: (i, 0)),
            pl.BlockSpec((K, D), lambda i: (0, 0)),
            pl.BlockSpec((TM, D), lambda i: (i, 0)),
            pl.BlockSpec((None, None, 3, D), lambda i: (ls, _mod_row(i, tpb, ctx_tiles, nbatch), 0, 0)),
            pl.BlockSpec((1, D), lambda i: (0, 0)),
            pl.BlockSpec((1, D), lambda i: (0, 0)),
        ],
        out_specs=pl.BlockSpec((TM, D), lambda i: (i, 0)),
        scratch_shapes=[pltpu.VMEM((TM, D), F32)],
        compiler_params=_cparams(("parallel",), VMEM_LIMIT),
        name="proj_ln",
    )(y, w, x, mods, ln_g.reshape(1, D), ln_b.reshape(1, D))


def _pack_bf16_pair(lo, hi):
    lo_w = lax.bitcast_convert_type(lo.astype(BF16).astype(F32), jnp.uint32) >> 16
    hi_w = lax.bitcast_convert_type(hi.astype(BF16).astype(F32), jnp.uint32) & jnp.uint32(0xFFFF0000)
    return lo_w | hi_w


def _unpack_bf16_pair(w):
    lo = lax.bitcast_convert_type(w << 16, F32)
    hi = lax.bitcast_convert_type(w & jnp.uint32(0xFFFF0000), F32)
    return lo, hi


def _index_copy(idx_hbm, idx_smem, sems, tile):
    return pltpu.make_async_copy(idx_hbm.at[tile], idx_smem.at[tile % 2], sems.at[tile % 2])


def _combine_ln_kernel(dest_hbm, ys_hbm, gate4_ref, x_ref, mod_ref, g_ref, b_ref, o_ref,
                       idx_smem, buf_ref, rows2d_ref, idx_sems, row_sems, *, alpha):
    i = pl.program_id(0)
    n = pl.num_programs(0)

    def gather_rows(tile):
        slot = tile % 2

        def issue(r, carry):
            for k in range(TOP_K):
                row = idx_smem[slot, r * TOP_K + k]
                pltpu.make_async_copy(ys_hbm.at[row], buf_ref.at[slot, k, r],
                                      row_sems.at[slot]).start(priority=k % 2)
            return carry

        lax.fori_loop(0, TM, issue, 0, unroll=8)

    @pl.when(i == 0)
    def _():
        cp = _index_copy(dest_hbm, idx_smem, idx_sems, 0)
        cp.start()
        cp.wait()
        gather_rows(0)

        @pl.when(n > 1)
        def _():
            _index_copy(dest_hbm, idx_smem, idx_sems, 1).start()

    @pl.when(i + 1 < n)
    def _():
        _index_copy(dest_hbm, idx_smem, idx_sems, i + 1).wait()
        gather_rows(i + 1)

        @pl.when(i + 2 < n)
        def _():
            _index_copy(dest_hbm, idx_smem, idx_sems, i + 2).start()

    slot = i % 2
    for k in range(TOP_K):
        pltpu.make_async_copy(ys_hbm.at[pl.ds(0, TM)], buf_ref.at[slot, k], row_sems.at[slot]).wait()
    y_lo = y_hi = None
    for k in range(TOP_K):
        rows2d_ref[...] = buf_ref[slot, k].reshape(rows2d_ref.shape)
        lo, hi = _unpack_bf16_pair(rows2d_ref[...])
        gk = gate4_ref[:, k:k + 1]
        y_lo = lo * gk if y_lo is None else y_lo + lo * gk
        y_hi = hi * gk if y_hi is None else y_hi + hi * gk
    y = jnp.concatenate([y_lo, y_hi], axis=1)
    z = alpha * x_ref[...] + mod_ref[2:3, :] * y
    o_ref[...] = _layer_norm_rows(z, g_ref[...], b_ref[...])


def _combine_ln(dest_tiles, ys, gate4, x, mods, ls, ln_g, ln_b, *, alpha, geom, latent_only):
    D = x.shape[1]
    n_tiles, row_tile, mod_row = _token_tiles(geom, latent_only)
    return pl.pallas_call(
        functools.partial(_combine_ln_kernel, alpha=alpha),
        out_shape=jax.ShapeDtypeStruct((n_tiles * TM, D), F32),
        grid=(n_tiles,),
        in_specs=[
            pl.BlockSpec(memory_space=pl.ANY),
            pl.BlockSpec(memory_space=pl.ANY),
            pl.BlockSpec((TM, LANES), lambda i: (i, 0)),
            pl.BlockSpec((TM, D), lambda i: (row_tile(i), 0)),
            pl.BlockSpec((None, None, 3, D), lambda i: (ls, mod_row(i), 0, 0)),
            pl.BlockSpec((1, D), lambda i: (0, 0)),
            pl.BlockSpec((1, D), lambda i: (0, 0)),
        ],
        out_specs=pl.BlockSpec((TM, D), lambda i: (i, 0)),
        scratch_shapes=[
            pltpu.SMEM((2, TM * TOP_K), jnp.int32),
            pltpu.VMEM((2, TOP_K, TM, 1, D // 2), jnp.uint32),
            pltpu.VMEM((TM, D // 2), jnp.uint32),
            pltpu.SemaphoreType.DMA((2,)),
            pltpu.SemaphoreType.DMA((2,)),
        ],
        compiler_params=_cparams(("arbitrary",), VMEM_LIMIT),
        name="combine_ln",
    )(dest_tiles, ys, gate4, x, mods, ln_g.reshape(1, D), ln_b.reshape(1, D))


def _dispatch_kernel(pad_lo_ref, nused_ref, dest_hbm, x_ref, mod_ref, xs_hbm, idx_smem, a_ref, zero_ref,
                     idx_sems, row_sems, *, bm):
    i = pl.program_id(0)
    n = pl.num_programs(0)
    slot = i % 2

    def wait_rows(s):
        for _ in range(TOP_K):
            pltpu.make_async_copy(a_ref.at[s], xs_hbm.at[pl.ds(0, TM)], row_sems.at[s]).wait()

    @pl.when(i == 0)
    def _():
        _index_copy(dest_hbm, idx_smem, idx_sems, 0).start()
        zero_ref[...] = jnp.zeros_like(zero_ref)
        n_exp = pad_lo_ref.shape[0]
        n_blk = xs_hbm.shape[0] // bm

        def zero_block(row0):
            return pltpu.make_async_copy(zero_ref, xs_hbm.at[pl.ds(pl.multiple_of(row0, bm), bm)],
                                         row_sems.at[0])

        for e in range(n_exp):
            @pl.when(pad_lo_ref[e] >= 0)
            def _():
                zero_block(pad_lo_ref[e]).start()

        def fill_tail(b, carry):
            zero_block(b * bm).start()
            return carry

        lax.fori_loop(nused_ref[0], n_blk, fill_tail, 0)
        for e in range(n_exp):
            @pl.when(pad_lo_ref[e] >= 0)
            def _():
                zero_block(0).wait()

        def wait_tail(b, carry):
            zero_block(0).wait()
            return carry

        lax.fori_loop(nused_ref[0], n_blk, wait_tail, 0)

    @pl.when(i + 1 < n)
    def _():
        _index_copy(dest_hbm, idx_smem, idx_sems, i + 1).start()

    @pl.when(i >= 2)
    def _():
        wait_rows(slot)

    shift = mod_ref[0:1, :]
    scale = mod_ref[1:2, :]
    a = x_ref[...] * (1.0 + scale) + shift
    half = a.shape[1] // 2
    a_ref[slot] = _pack_bf16_pair(a[:, :half], a[:, half:]).reshape(a_ref.shape[1:])
    _index_copy(dest_hbm, idx_smem, idx_sems, i).wait()

    def issue(r, carry):
        for k in range(TOP_K):
            row = idx_smem[slot, r * TOP_K + k]
            pltpu.make_async_copy(a_ref.at[slot, r], xs_hbm.at[row], row_sems.at[slot]).start(priority=k % 2)
        return carry

    lax.fori_loop(0, TM, issue, 0, unroll=8)

    @pl.when(i == n - 1)
    def _():
        wait_rows(slot)

        @pl.when(n > 1)
        def _():
            wait_rows(1 - slot)


def _dispatch(pad_lo, n_used, dest_tiles, x, mods, ls, n_rows, *, geom, latent_only):
    D = x.shape[1]
    n_tiles, row_tile, mod_row = _token_tiles(geom, latent_only)
    bm = EXPERT_BM
    return pl.pallas_call(
        functools.partial(_dispatch_kernel, bm=bm),
        out_shape=jax.ShapeDtypeStruct((n_rows, 1, D // 2), jnp.uint32),
        grid_spec=pltpu.PrefetchScalarGridSpec(
            num_scalar_prefetch=2,
            grid=(n_tiles,),
            in_specs=[
                pl.BlockSpec(memory_space=pl.ANY),
                pl.BlockSpec((TM, D), lambda i, p, n: (row_tile(i), 0)),
                pl.BlockSpec((None, None, 3, D), lambda i, p, n: (ls, mod_row(i), 0, 0)),
            ],
            out_specs=pl.BlockSpec(memory_space=pl.ANY),
            scratch_shapes=[
                pltpu.SMEM((2, TM * TOP_K), jnp.int32),
                pltpu.VMEM((2, TM, 1, D // 2), jnp.uint32),
                pltpu.VMEM((bm, 1, D // 2), jnp.uint32),
                pltpu.SemaphoreType.DMA((2,)),
                pltpu.SemaphoreType.DMA((2,)),
            ],
        ),
        compiler_params=_cparams(("arbitrary",), VMEM_LIMIT),
        name="moe_dispatch",
    )(pad_lo, n_used, dest_tiles, x, mods)


def _sigmoid(x):
    return 0.5 * jnp.tanh(0.5 * x) + 0.5


def _softplus(x):
    return jnp.maximum(x, 0.0) + jnp.log(1.0 + jnp.exp(-jnp.abs(x)))


def _gelu_tanh(x):
    c = math.sqrt(2.0 / math.pi)
    return 0.5 * x * (1.0 + jnp.tanh(c * (x + 0.044715 * (x * x * x))))


def _group_scan(a, b, row, reverse):
    for s in (1, 2, 4):
        if reverse:
            keep = row < SUBLANES - s
            shift = SUBLANES - s
        else:
            keep = row >= s
            shift = s
        a_sh = jnp.where(keep, pltpu.roll(a, shift, axis=0), 1.0)
        b_sh = jnp.where(keep, pltpu.roll(b, shift, axis=0), 0.0)
        b = b + a * b_sh
        a = a * a_sh
    return a, b


def _lru_kernel(ug_ref, ur_ref, cw_ref, cb_ref, wg_ref, bg_ref, lam_ref, o_ref,
                a0_ref, b0_ref, a1_ref, b1_ref, *, ctx_len):
    T, W = ur_ref.shape
    CH = 256
    n_chunks = T // CH
    ctx_chunks = ctx_len // CH
    row_c = lax.broadcasted_iota(jnp.int32, (CH, W), 0)
    kexp = (-0.5 * LRU_C * math.log2(math.e)) * _softplus(-lam_ref[...])

    def coeff_chunk(c, carry):
        c0 = pl.multiple_of(c * CH, CH)
        cur = ur_ref[pl.ds(c0, CH), :]
        prev8 = ur_ref[pl.ds(pl.multiple_of(jnp.maximum(c0 - SUBLANES, 0), SUBLANES), SUBLANES), :]
        next8 = ur_ref[pl.ds(pl.multiple_of(jnp.minimum(c0 + CH, T - SUBLANES), SUBLANES), SUBLANES), :]
        seg_start = jnp.logical_or(c == 0, c == ctx_chunks)
        seg_end = jnp.logical_or(c == ctx_chunks - 1, c == n_chunks - 1)
        p_last = jnp.where(seg_start, 0.0, prev8[SUBLANES - 1:SUBLANES, :])
        n_first = jnp.where(seg_end, 0.0, next8[0:1, :])
        n_second = jnp.where(seg_end, 0.0, next8[1:2, :])
        u_m1 = jnp.where(row_c == 0, p_last, pltpu.roll(cur, 1, axis=0))
        u_p1 = jnp.where(row_c == CH - 1, n_first, pltpu.roll(cur, CH - 1, axis=0))
        u_p2 = jnp.where(row_c == CH - 2, n_first,
                         jnp.where(row_c == CH - 1, n_second, pltpu.roll(cur, CH - 2, axis=0)))
        r = cb_ref[...] + u_m1 * cw_ref[0:1, :]
        r = r + cur * cw_ref[1:2, :]
        r = r + u_p1 * cw_ref[2:3, :]
        r = r + u_p2 * cw_ref[3:4, :]
        gts = jnp.dot(r.astype(BF16), wg_ref[0].astype(BF16), preferred_element_type=F32) + bg_ref[0]
        half_r = 0.5 * r
        for d, (a_ref, b_ref) in enumerate(((a0_ref, b0_ref), (a1_ref, b1_ref))):
            t_rec = jnp.tanh(gts[:, (2 * d) * W:(2 * d + 1) * W])
            t_in = jnp.tanh(gts[:, (2 * d + 1) * W:(2 * d + 2) * W])
            kd = kexp[d:d + 1, :]
            a = jnp.exp2(kd * t_rec + kd)
            a_ref[pl.ds(c0, CH), :] = a
            b_ref[pl.ds(c0, CH), :] = jnp.sqrt(jnp.maximum(1.0 - a * a, 0.0)) * (half_r * t_in + half_r)
        return carry

    lax.fori_loop(0, n_chunks, coeff_chunk, 0)

    row8 = lax.broadcasted_iota(jnp.int32, (SUBLANES, W), 0)
    n_groups = T // SUBLANES
    ctx_groups = ctx_len // SUBLANES

    def scan_step(k, carry):
        hf, hb = carry
        gf = pl.multiple_of(k * SUBLANES, SUBLANES)
        a, b = _group_scan(a0_ref[pl.ds(gf, SUBLANES), :], b0_ref[pl.ds(gf, SUBLANES), :], row8, False)
        h = a * hf + b
        a0_ref[pl.ds(gf, SUBLANES), :] = h
        hf = jnp.broadcast_to(h[SUBLANES - 1:SUBLANES, :], h.shape)
        g = jnp.where(k < ctx_groups, ctx_groups - 1 - k, n_groups - 1 - (k - ctx_groups))
        gb = pl.multiple_of(g * SUBLANES, SUBLANES)
        a, b = _group_scan(a1_ref[pl.ds(gb, SUBLANES), :], b1_ref[pl.ds(gb, SUBLANES), :], row8, True)
        h = a * hb + b
        a1_ref[pl.ds(gb, SUBLANES), :] = h
        hb = jnp.broadcast_to(h[0:1, :], h.shape)
        return hf, hb

    zero = jnp.zeros((SUBLANES, W), F32)
    lax.fori_loop(0, n_groups, scan_step, (zero, zero), unroll=4)

    def out_chunk(c, carry):
        c0 = pl.multiple_of(c * CH, CH)
        y = a0_ref[pl.ds(c0, CH), :] + a1_ref[pl.ds(c0, CH), :]
        o_ref[pl.ds(c0, CH), :] = (y * _gelu_tanh(ug_ref[pl.ds(c0, CH), :])).astype(o_ref.dtype)
        return carry

    lax.fori_loop(0, n_chunks, out_chunk, 0)


def _lru_mix(u, conv_w, conv_b, gate_w, gate_b, lam, *, nbatch, ctx_len):
    M, W2 = u.shape
    W = W2 // 2
    T = M // nbatch
    nblk = W // LRU_BLOCK_W
    bw = LRU_BLOCK_W
    wg = 0.5 * jnp.transpose(gate_w, (2, 3, 0, 1, 4)).reshape(nblk, bw, 4 * bw)
    bg = 0.5 * jnp.transpose(gate_b.reshape(2, 2, nblk, bw), (2, 0, 1, 3)).reshape(nblk, 1, 4 * bw)
    return pl.pallas_call(
        functools.partial(_lru_kernel, ctx_len=ctx_len),
        out_shape=jax.ShapeDtypeStruct((M, W), BF16),
        grid=(nbatch, nblk),
        in_specs=[
            pl.BlockSpec((T, bw), lambda b, n: (b, n)),
            pl.BlockSpec((T, bw), lambda b, n: (b, nblk + n)),
            pl.BlockSpec((CONV_W, bw), lambda b, n: (0, n)),
            pl.BlockSpec((1, bw), lambda b, n: (0, n)),
            pl.BlockSpec((1, bw, 4 * bw), lambda b, n: (n, 0, 0)),
            pl.BlockSpec((1, 1, 4 * bw), lambda b, n: (n, 0, 0)),
            pl.BlockSpec((2, bw), lambda b, n: (0, n)),
        ],
        out_specs=pl.BlockSpec((T, bw), lambda b, n: (b, n)),
        scratch_shapes=[pltpu.VMEM((T, bw), F32) for _ in range(4)],
        compiler_params=_cparams(("parallel", "parallel"), VMEM_LIMIT),
        name="lru_mix",
    )(u, u, conv_w, conv_b.reshape(1, W), wg, bg, lam)


def _attn_kernel(sink_ref, q_ref, k_ref, v_ref, o_ref, k2_ref, v2_ref, p_ref, *, ctx_len, n_kv):
    QB = q_ref.shape[0]
    T = k_ref.shape[0]
    n_lat = (T - ctx_len) // QB
    ctx_blocks = ctx_len // QB
    band = 3 * QB
    nk = ctx_len + band
    pairs = GROUP // 2
    qb = pl.program_id(1)
    j = qb - ctx_blocks
    start = jnp.clip(j - 1, 0, n_lat - 3)
    off = pl.multiple_of(ctx_len + start * QB, QB)
    qpos = j * QB + lax.broadcasted_iota(jnp.int32, (QB, band), 0)
    kpos = start * QB + lax.broadcasted_iota(jnp.int32, (QB, band), 1)
    valid = jnp.logical_and(jnp.abs(qpos - kpos) <= WINDOW, qb >= ctx_blocks)
    lo_half = lax.broadcasted_iota(jnp.int32, (1, LANES), 1) < HEAD_DIM
    nt = (((1,), (1,)), ((), ()))
    for h in range(n_kv):
        slab = slice((h // 2) * LANES, (h // 2 + 1) * LANES)

        def two_blocks(x):
            swapped = jnp.concatenate([x[:, HEAD_DIM:], x[:, :HEAD_DIM]], axis=1)
            own, other = (x, swapped) if h % 2 == 0 else (swapped, x)
            zero = jnp.zeros_like(x)
            return jnp.where(lo_half, own, zero), jnp.where(lo_half, zero, other)

        for dst, src in ((k2_ref, k_ref), (v2_ref, v_ref)):
            top_c, bot_c = two_blocks(src[0:ctx_len, slab])
            top_b, bot_b = two_blocks(src[pl.ds(off, band), slab])
            dst[0:ctx_len, :] = top_c
            dst[ctx_len:nk, :] = top_b
            dst[nk:nk + ctx_len, :] = bot_c
            dst[nk + ctx_len:2 * nk, :] = bot_b
        c_q = h * GROUP * HEAD_DIM
        qs = jnp.concatenate([q_ref[:, c_q + p * LANES:c_q + (p + 1) * LANES] for p in range(pairs)], axis=0)
        s = lax.dot_general(qs, k2_ref[...], nt, preferred_element_type=F32)
        invs = []
        for p in range(pairs):
            rows = slice(p * QB, (p + 1) * QB)
            inv_pair = []
            for half in range(2):
                c0 = half * nk
                sink = sink_ref[h * GROUP + 2 * p + half]
                s_c = s[rows, c0:c0 + ctx_len]
                s_b = jnp.where(valid, s[rows, c0 + ctx_len:c0 + nk], NEG_INF)
                m = jnp.maximum(jnp.maximum(jnp.max(s_c, axis=-1, keepdims=True),
                                            jnp.max(s_b, axis=-1, keepdims=True)), sink)
                p_c = jnp.exp(s_c - m)
                p_b = jnp.exp(s_b - m)
                denom = (jnp.sum(p_c, axis=-1, keepdims=True) + jnp.sum(p_b, axis=-1, keepdims=True)
                         + jnp.exp(sink - m))
                p_ref[rows, c0:c0 + ctx_len] = p_c.astype(BF16)
                p_ref[rows, c0 + ctx_len:c0 + nk] = p_b.astype(BF16)
                inv_pair.append(1.0 / denom)
            invs.append(inv_pair)
        o = jnp.dot(p_ref[...], v2_ref[...], preferred_element_type=F32)
        for p in range(pairs):
            inv = jnp.where(lo_half, invs[p][0], invs[p][1])
            o_ref[:, c_q + p * LANES:c_q + (p + 1) * LANES] = (o[p * QB:(p + 1) * QB] * inv).astype(o_ref.dtype)


def _attention(qkv, sinks, *, nbatch, ctx_len, q_dim, kv_dim):
    M = qkv.shape[0]
    T = M // nbatch
    nqb = T // Q_BLOCK
    kcol = q_dim // kv_dim
    return pl.pallas_call(
        functools.partial(_attn_kernel, ctx_len=ctx_len, n_kv=kv_dim // HEAD_DIM),
        out_shape=jax.ShapeDtypeStruct((M, q_dim), BF16),
        grid_spec=pltpu.PrefetchScalarGridSpec(
            num_scalar_prefetch=1,
            grid=(nbatch, nqb),
            in_specs=[
                pl.BlockSpec((Q_BLOCK, q_dim), lambda b, q, s: (b * nqb + q, 0)),
                pl.BlockSpec((T, kv_dim), lambda b, q, s: (b, kcol)),
                pl.BlockSpec((T, kv_dim), lambda b, q, s: (b, kcol + 1)),
            ],
            out_specs=pl.BlockSpec((Q_BLOCK, q_dim), lambda b, q, s: (b * nqb + q, 0)),
            scratch_shapes=[
                pltpu.VMEM((2 * (ctx_len + 3 * Q_BLOCK), LANES), BF16),
                pltpu.VMEM((2 * (ctx_len + 3 * Q_BLOCK), LANES), BF16),
                pltpu.VMEM((GROUP // 2 * Q_BLOCK, 2 * (ctx_len + 3 * Q_BLOCK)), BF16),
            ],
        ),
        compiler_params=_cparams(("parallel", "parallel"), VMEM_LIMIT),
        name="swa_attention",
    )(sinks, qkv, qkv, qkv)


def _split_bf16(x):
    hi = x.astype(BF16)
    lo = (x - hi.astype(F32)).astype(BF16)
    return hi, lo


def _router_kernel(x_ref, mod_ref, wr_ref, br_ref, e_ref, g_ref, r_ref, cnt_ref, carry_ref):
    i = pl.program_id(0)

    @pl.when(i == 0)
    def _():
        carry_ref[...] = jnp.zeros_like(carry_ref)

    shift = mod_ref[0:1, :]
    scale = mod_ref[1:2, :]
    a = x_ref[...] * (1.0 + scale) + shift
    a_hi, a_lo = _split_bf16(a)
    w_hi, w_lo = _split_bf16(wr_ref[...])
    logits = (jnp.dot(a_hi, w_hi, preferred_element_type=F32)
              + (jnp.dot(a_lo, w_hi, preferred_element_type=F32)
                 + jnp.dot(a_hi, w_lo, preferred_element_type=F32))) + br_ref[...]
    tm, ne = logits.shape
    lane_e = lax.broadcasted_iota(jnp.int32, (tm, ne), 1)
    lane_o = lax.broadcasted_iota(jnp.int32, (tm, LANES), 1)
    vals, idxs, hots = [], [], []
    cur = logits
    for k in range(TOP_K):
        m = jnp.max(cur, axis=-1, keepdims=True)
        idx = jnp.min(jnp.where(cur == m, lane_e, ne), axis=-1, keepdims=True)
        hot = lane_e == idx
        vals.append(m)
        idxs.append(idx)
        hots.append(hot)
        cur = jnp.where(hot, -jnp.inf, cur)
    exps = [jnp.exp(v - vals[0]) for v in vals]
    inv = 1.0 / (exps[0] + exps[1] + exps[2] + exps[3])
    msum = jnp.zeros((tm, ne), F32)
    for hot in hots:
        msum = msum + hot.astype(F32)
    rr = lax.broadcasted_iota(jnp.int32, (tm, tm), 0)
    cc = lax.broadcasted_iota(jnp.int32, (tm, tm), 1)
    tri = (cc < rr).astype(BF16)
    before = jnp.dot(tri, msum.astype(BF16), preferred_element_type=F32) + carry_ref[...]
    e_out = jnp.zeros((tm, LANES), jnp.int32)
    g_out = jnp.zeros((tm, LANES), F32)
    r_out = jnp.zeros((tm, LANES), jnp.int32)
    for k in range(TOP_K):
        rank = jnp.sum(jnp.where(hots[k], before, 0.0), axis=-1, keepdims=True).astype(jnp.int32)
        e_out = jnp.where(lane_o == k, idxs[k], e_out)
        g_out = jnp.where(lane_o == k, exps[k] * inv, g_out)
        r_out = jnp.where(lane_o == k, rank, r_out)
    e_ref[...] = e_out
    g_ref[...] = g_out
    r_ref[...] = r_out
    carry_ref[...] = carry_ref[...] + jnp.sum(msum, axis=0, keepdims=True)
    cnt_ref[...] = carry_ref[...].astype(jnp.int32)


def _router(x, mods, ls, w_r, b_r, *, geom, latent_only):
    D = x.shape[1]
    ne = w_r.shape[1]
    n_tiles, row_tile, mod_row = _token_tiles(geom, latent_only)
    M = n_tiles * TM
    return pl.pallas_call(
        _router_kernel,
        out_shape=(
            jax.ShapeDtypeStruct((M, LANES), jnp.int32),
            jax.ShapeDtypeStruct((M, LANES), F32),
            jax.ShapeDtypeStruct((M, LANES), jnp.int32),
            jax.ShapeDtypeStruct((1, ne), jnp.int32),
        ),
        grid=(n_tiles,),
        in_specs=[
            pl.BlockSpec((TM, D), lambda i: (row_tile(i), 0)),
            pl.BlockSpec((None, None, 3, D), lambda i: (ls, mod_row(i), 0, 0)),
            pl.BlockSpec((D, ne), lambda i: (0, 0)),
            pl.BlockSpec((1, ne), lambda i: (0, 0)),
        ],
        out_specs=(
            pl.BlockSpec((TM, LANES), lambda i: (i, 0)),
            pl.BlockSpec((TM, LANES), lambda i: (i, 0)),
            pl.BlockSpec((TM, LANES), lambda i: (i, 0)),
            pl.BlockSpec((1, ne), lambda i: (0, 0)),
        ),
        scratch_shapes=[pltpu.VMEM((1, ne), F32)],
        compiler_params=_cparams(("arbitrary",), VMEM_LIMIT),
        name="moe_router",
    )(x, mods, w_r, b_r.reshape(1, ne))


def _expert_kernel(blk_e_ref, slot_ref, first_ref, next_e_ref, nused_ref, xs_ref, bgu_ref, bdn_ref,
                   wgu_hbm, wdn_hbm, o_ref, wgu_buf, wdn_buf, x2d_ref, sem_gu, sem_dn, *, layer):
    b = pl.program_id(0)
    de, d = wdn_buf.shape[1:]
    half = d // 2

    def fetch(expert, slot):
        cps = []
        for hbm, buf, sem, chunks in ((wgu_hbm, wgu_buf, sem_gu, WEIGHT_DMA_CHUNKS),
                                      (wdn_hbm, wdn_buf, sem_dn, WEIGHT_DMA_CHUNKS // 2)):
            rows = buf.shape[1] // chunks
            for c in range(chunks):
                rs = pl.ds(c * rows, rows)
                cps.append(pltpu.make_async_copy(hbm.at[layer, expert, rs], buf.at[slot, rs], sem.at[slot]))
        return cps

    @pl.when(b < nused_ref[0])
    def _():
        slot = slot_ref[b]

        @pl.when(b == 0)
        def _():
            for c, cp in enumerate(fetch(blk_e_ref[0], 0)):
                cp.start(priority=c % 2)

        @pl.when(first_ref[b] == 1)
        def _():
            @pl.when(next_e_ref[b] >= 0)
            def _():
                for c, cp in enumerate(fetch(next_e_ref[b], 1 - slot)):
                    cp.start(priority=c % 2)
            for cp in fetch(blk_e_ref[b], slot):
                cp.wait()

        x2d_ref[...] = xs_ref[...].reshape(x2d_ref.shape)
        x_lo, x_hi = (v.astype(BF16) for v in _unpack_bf16_pair(x2d_ref[...]))
        hs = []
        for c in range(2 * de // 256):
            sl = slice(c * 256, (c + 1) * 256)
            acc = jnp.dot(x_lo, wgu_buf[slot, 0:half, sl].astype(BF16), preferred_element_type=F32)
            acc = acc + jnp.dot(x_hi, wgu_buf[slot, half:d, sl].astype(BF16), preferred_element_type=F32)
            hs.append(acc + bgu_ref[:, sl])
        h = jnp.concatenate(hs, axis=1)
        g = jnp.minimum(h[:, :de], SWIGLU_LIMIT)
        u = jnp.clip(h[:, de:], -SWIGLU_LIMIT, SWIGLU_LIMIT)
        act = ((u + 1.0) * (g * _sigmoid(SWIGLU_ALPHA * g))).astype(BF16)
        ys = []
        for c in range(d // 512):
            sl = slice(c * 512, (c + 1) * 512)
            ys.append(jnp.dot(act, wdn_buf[slot, :, sl].astype(BF16), preferred_element_type=F32) + bdn_ref[:, sl])
        nh = len(ys) // 2
        for c in range(nh):
            packed = _pack_bf16_pair(ys[c], ys[c + nh])
            o_ref[:, :, c * 512:(c + 1) * 512] = packed.reshape(packed.shape[0], 1, 512)

    @pl.when(b >= nused_ref[0])
    def _():
        o_ref[...] = jnp.zeros_like(o_ref)


def _expert_ffn(xs, blk_tabs, n_used, layer, w_gu, b_gu, w_dn, b_dn):
    P = xs.shape[0]
    _, ne, D, de2 = w_gu.shape
    de = de2 // 2
    n_blk = P // EXPERT_BM
    blk_e, slot, first, next_e = blk_tabs
    return pl.pallas_call(
        functools.partial(_expert_kernel, layer=layer),
        out_shape=jax.ShapeDtypeStruct((P, 1, D // 2), jnp.uint32),
        grid_spec=pltpu.PrefetchScalarGridSpec(
            num_scalar_prefetch=5,
            grid=(n_blk,),
            in_specs=[
                pl.BlockSpec((EXPERT_BM, 1, D // 2), lambda b, e, s, f, x, n: (jnp.minimum(b, n[0] - 1), 0, 0)),
                pl.BlockSpec((None, None, 1, de2), lambda b, e, s, f, x, n: (layer, e[b], 0, 0)),
                pl.BlockSpec((None, None, 1, D), lambda b, e, s, f, x, n: (layer, e[b], 0, 0)),
                pl.BlockSpec(memory_space=pl.ANY),
                pl.BlockSpec(memory_space=pl.ANY),
            ],
            out_specs=pl.BlockSpec((EXPERT_BM, 1, D // 2), lambda b, e, s, f, x, n: (b, 0, 0)),
            scratch_shapes=[
                pltpu.VMEM((2, D, de2), F32),
                pltpu.VMEM((2, de, D), F32),
                pltpu.VMEM((EXPERT_BM, D // 2), jnp.uint32),
                pltpu.SemaphoreType.DMA((2,)),
                pltpu.SemaphoreType.DMA((2,)),
            ],
        ),
        compiler_params=_cparams(("arbitrary",), VMEM_LIMIT),
        name="expert_ffn",
    )(blk_e, slot, first, next_e, n_used, xs, b_gu.reshape(b_gu.shape[0], ne, 1, de2),
      b_dn.reshape(b_dn.shape[0], ne, 1, D), w_gu, w_dn)


def _moe_layer(x, mods, ls, layer, w_r, b_r, w_gu, b_gu, w_dn, b_dn, ln_g, ln_b, *, alpha, geom, latent_only):
    ne = w_r.shape[1]
    bm = EXPERT_BM
    e128, g128, r128, counts = _router(x, mods, ls, w_r, b_r, geom=geom, latent_only=latent_only)
    M = e128.shape[0]
    top_e = e128[:, :TOP_K]
    rank = r128[:, :TOP_K]
    counts = counts[0]
    padded = ((counts + bm - 1) // bm) * bm
    ends = jnp.cumsum(padded)
    pstart = ends - padded
    experts = jnp.arange(ne, dtype=jnp.int32)
    dest = jnp.sum(jnp.where(top_e[..., None] == experts, pstart, 0), axis=-1) + rank
    dest_tiles = dest.astype(jnp.int32).reshape(M // TM, TM * TOP_K)
    A = M * TOP_K
    P = (-(-A // bm)) * bm + ne * bm
    n_blk = P // bm
    n_used = (ends[-1] // bm).astype(jnp.int32).reshape(1)
    blk = jnp.arange(n_blk, dtype=jnp.int32)
    blk_row = jnp.minimum(blk, n_used[0] - 1) * bm
    blk_e = jnp.minimum(jnp.sum(ends[None, :] <= blk_row[:, None], axis=1), ne - 1).astype(jnp.int32)
    first = jnp.logical_and(blk < n_used[0],
                            jnp.concatenate([jnp.ones((1,), bool), blk_e[1:] != blk_e[:-1]])).astype(jnp.int32)
    slot = ((jnp.cumsum(first) - 1) % 2).astype(jnp.int32)
    later = jnp.logical_and(experts[None, :] > experts[:, None], (padded > 0)[None, :])
    next_nonempty = jnp.min(jnp.where(later, experts[None, :], ne), axis=1)
    next_nonempty = jnp.where(next_nonempty == ne, -1, next_nonempty)
    next_e = jnp.sum(jnp.where(blk_e[:, None] == experts[None, :], next_nonempty[None, :], 0), axis=1).astype(jnp.int32)
    pad_lo = jnp.where(padded > 0, ends - bm, -1).astype(jnp.int32)
    xs = _dispatch(pad_lo, n_used, dest_tiles, x, mods, ls, P, geom=geom, latent_only=latent_only)
    ys = _expert_ffn(xs, (blk_e, slot, first, next_e), n_used, layer, w_gu, b_gu, w_dn, b_dn)
    return _combine_ln(dest_tiles, ys, g128, x, mods, ls, ln_g, ln_b, alpha=alpha, geom=geom,
                       latent_only=latent_only)


def _rope_tables(seq, ctx_len):
    rows = seq // GRID_W
    row_pos = jnp.repeat(jnp.arange(rows), GRID_W).astype(F32)
    col_pos = jnp.tile(jnp.arange(GRID_W), rows).astype(F32)
    n = HEAD_DIM // 2
    freqs = ROPE_BASE ** (-jnp.arange(0, n, 2, dtype=F32) / n)
    ang_r = row_pos[:, None] * freqs[None, :]
    ang_c = col_pos[:, None] * freqs[None, :]
    cos = jnp.concatenate([jnp.cos(ang_r), jnp.cos(ang_r), jnp.cos(ang_c), jnp.cos(ang_c)], axis=1)
    sin = jnp.concatenate([-jnp.sin(ang_r), jnp.sin(ang_r), -jnp.sin(ang_c), jnp.sin(ang_c)], axis=1)
    cos = jnp.concatenate([jnp.ones((ctx_len, HEAD_DIM), F32), cos], axis=0)
    sin = jnp.concatenate([jnp.zeros((ctx_len, HEAD_DIM), F32), sin], axis=0)
    cos = jnp.tile(cos, (1, LANES // HEAD_DIM))
    sin = jnp.tile(sin, (1, LANES // HEAD_DIM))
    scale = HEAD_DIM ** -0.5
    return jnp.stack([cos * scale, sin * scale, cos, sin], axis=0)


def kernel(x, c, ctx, c_ctx, ada_w, ada_b, ln_g, ln_b, lru_w_in, lru_conv_w, lru_conv_b, lru_gate_w, lru_gate_b, lru_lambda, lru_w_out, attn_w_qkv, attn_sinks, attn_w_o, router_w, router_b, moe_w_gu, moe_b_gu, moe_w_down, moe_b_down):
    B, S, D = x.shape
    C = ctx.shape[1]
    depth = ada_w.shape[0]
    T = C + S
    assert C % TM == 0 and S % TM == 0 and C % Q_BLOCK == 0
    geom = (T // TM, C // TM, B)
    alpha = (2.0 * depth) ** 0.25
    kv_dim = (attn_w_qkv.shape[2] - D) // 2

    act = jax.nn.silu(jnp.concatenate([c, c_ctx[None, :]], axis=0))
    mods = _ada_params(act, ada_w.reshape(depth * 2, D, 3 * D), ada_b.reshape(depth * 2, 3 * D))
    mods = mods.reshape(depth * 2, SUBLANES, 3, D)
    tab = _rope_tables(S, C)

    z = jnp.concatenate([ctx, x], axis=1).reshape(B * T, D)
    for i in range(depth):
        j = i // 2
        if i % 2 == 0:
            u = _mod_matmul(z, mods, 2 * i, lru_w_in[j].astype(BF16), tn=D, out_dtype=F32, geom=geom)
            y = _lru_mix(u, lru_conv_w[j], lru_conv_b[j], lru_gate_w[j], lru_gate_b[j], lru_lambda[j],
                         nbatch=B, ctx_len=C)
            w_out = lru_w_out[j]
        else:
            qkv = _mod_matmul(z, mods, 2 * i, attn_w_qkv[j].astype(BF16), tn=D + 2 * kv_dim, out_dtype=BF16,
                              geom=geom, rope=(tab, (D, kv_dim)))
            y = _attention(qkv, attn_sinks[j], nbatch=B, ctx_len=C, q_dim=D, kv_dim=kv_dim)
            w_out = attn_w_o[j]
        z = _proj_ln(y, w_out.astype(BF16), z, mods, 2 * i, ln_g[i, 0], ln_b[i, 0], alpha=alpha, geom=geom)
        z = _moe_layer(z, mods, 2 * i + 1, i, router_w[i], router_b[i], moe_w_gu, moe_b_gu,
                       moe_w_down, moe_b_down, ln_g[i, 1], ln_b[i, 1], alpha=alpha, geom=geom,
                       latent_only=(i == depth - 1))
    return z.reshape(B, S, D)
```

```python
import functools
import math

import jax
import jax.numpy as jnp
from jax import lax
from jax.experimental import pallas as pl
from jax.experimental.pallas import tpu as pltpu

GRID_W = 64
LN_EPS = 1e-5
LRU_BLOCK_W = 256
CONV_W = 4
LRU_C = 8.0
HEAD_DIM = 64
GROUP = 8
WINDOW = 128
Q_BLOCK = 128
ROPE_BASE = 10000.0
NEG_INF = -1e30
TOP_K = 4
SWIGLU_LIMIT = 7.0
SWIGLU_ALPHA = 1.702
LOG2_E = math.log2(math.e)

LANES = 128
SUBLANES = 8
TM = 256
EXPERT_BM = 256
WEIGHT_DMA_CHUNKS = 4
VMEM_LIMIT = 56 * 1024 * 1024

BF16 = jnp.bfloat16
F32 = jnp.float32


def _cparams(sem, vmem=None):
    return pltpu.CompilerParams(dimension_semantics=sem, vmem_limit_bytes=vmem)


def _ada_kernel(a_ref, w_ref, b_ref, o_ref):
    acc = jnp.dot(a_ref[...], w_ref[0].astype(BF16), preferred_element_type=F32)
    o_ref[0] = acc[0:SUBLANES] + acc[SUBLANES:2 * SUBLANES] + b_ref[0]


def _ada_params(act, w, bias):
    R, D = act.shape
    L, _, N = w.shape
    tn = 1024
    a8 = jnp.zeros((SUBLANES, D), F32).at[:R].set(act)
    hi = a8.astype(BF16)
    lo = (a8 - hi.astype(F32)).astype(BF16)
    a16 = jnp.concatenate([hi, lo], axis=0)
    return pl.pallas_call(
        _ada_kernel,
        out_shape=jax.ShapeDtypeStruct((L, SUBLANES, N), F32),
        grid=(L, N // tn),
        in_specs=[
            pl.BlockSpec((2 * SUBLANES, D), lambda l, n: (0, 0)),
            pl.BlockSpec((1, D, tn), lambda l, n: (l, 0, n)),
            pl.BlockSpec((1, 1, tn), lambda l, n: (l, 0, n)),
        ],
        out_specs=pl.BlockSpec((1, SUBLANES, tn), lambda l, n: (l, 0, n)),
        compiler_params=_cparams(("parallel", "parallel"), VMEM_LIMIT),
        name="ada_params",
    )(a16, w, bias.reshape(L, 1, N))


def _mod_row(i, tiles_per_batch, ctx_tiles, nbatch):
    return jnp.where(i % tiles_per_batch < ctx_tiles, nbatch, i // tiles_per_batch)


def _token_tiles(geom, latent_only):
    tpb, ctx_tiles, nbatch = geom
    if latent_only:
        lpb = tpb - ctx_tiles
        return lpb * nbatch, (lambda i: (i // lpb) * tpb + ctx_tiles + i % lpb), (lambda i: i // lpb)
    return tpb * nbatch, (lambda i: i), (lambda i: _mod_row(i, tpb, ctx_tiles, nbatch))


def _rope_chunk(y, cos, sin_signed, first_half):
    rot = jnp.where(first_half, pltpu.roll(y, LANES - 16, axis=1), pltpu.roll(y, 16, axis=1))
    return y * cos + rot * sin_signed


def _modmm_kernel(x_ref, mod_ref, w_ref, *rest, rope_cols):
    if rope_cols is None:
        (o_ref,) = rest
    else:
        tab_ref, o_ref = rest
    shift = mod_ref[0:1, :]
    scale = mod_ref[1:2, :]
    a = (x_ref[...] * (1.0 + scale) + shift).astype(BF16)
    tn = o_ref.shape[1]
    ch = 512 if tn % 512 == 0 else 256
    for c in range(tn // ch):
        acc = jnp.dot(a, w_ref[:, c * ch:(c + 1) * ch].astype(BF16), preferred_element_type=F32)
        if rope_cols is None:
            o_ref[:, c * ch:(c + 1) * ch] = acc.astype(o_ref.dtype)
        else:
            q_cols, k_cols = rope_cols
            lane = lax.broadcasted_iota(jnp.int32, (acc.shape[0], LANES), 1)
            first_half = (lane % 32) < 16
            for s in range(ch // LANES):
                col = c * ch + s * LANES
                y = acc[:, s * LANES:(s + 1) * LANES]
                if col < q_cols:
                    y = _rope_chunk(y, tab_ref[0], tab_ref[1], first_half)
                elif col < q_cols + k_cols:
                    y = _rope_chunk(y, tab_ref[2], tab_ref[3], first_half)
                o_ref[:, col:col + LANES] = y.astype(o_ref.dtype)


def _mod_matmul(x, mods, ls, w, *, tn, out_dtype, geom, rope=None):
    M, K = x.shape
    N = w.shape[1]
    tpb, ctx_tiles, nbatch = geom
    in_specs = [
        pl.BlockSpec((TM, K), lambda j, i: (i, 0)),
        pl.BlockSpec((None, None, 3, K), lambda j, i: (ls, _mod_row(i, tpb, ctx_tiles, nbatch), 0, 0)),
        pl.BlockSpec((K, tn), lambda j, i: (0, j)),
    ]
    args = [x, mods, w]
    rope_cols = None
    if rope is not None:
        tab, rope_cols = rope
        in_specs.append(pl.BlockSpec((4, TM, LANES), lambda j, i: (0, i % tpb, 0)))
        args.append(tab)
    return pl.pallas_call(
        functools.partial(_modmm_kernel, rope_cols=rope_cols),
        out_shape=jax.ShapeDtypeStruct((M, N), out_dtype),
        grid=(N // tn, M // TM),
        in_specs=in_specs,
        out_specs=pl.BlockSpec((TM, tn), lambda j, i: (i, j)),
        compiler_params=_cparams(("parallel", "parallel"), VMEM_LIMIT),
        name="mod_matmul",
    )(*args)


def _layer_norm_rows(z, g, b):
    mu = jnp.mean(z, axis=-1, keepdims=True)
    zc = z - mu
    var = jnp.mean(zc * zc, axis=-1, keepdims=True)
    return zc * lax.rsqrt(var + LN_EPS) * g + b


def _proj_ln_kernel(y_ref, w_ref, x_ref, mod_ref, g_ref, b_ref, o_ref, z_ref, *, alpha):
    gate = mod_ref[2:3, :]
    y = y_ref[...]
    n = o_ref.shape[1]
    ch = 512
    for c in range(n // ch):
        sl = slice(c * ch, (c + 1) * ch)
        acc = jnp.dot(y, w_ref[:, sl].astype(BF16), preferred_element_type=F32)
        z_ref[:, sl] = alpha * x_ref[:, sl] + gate[:, sl] * acc
    o_ref[...] = _layer_norm_rows(z_ref[...], g_ref[...], b_ref[...])


def _proj_ln(y, w, x, mods, ls, ln_g, ln_b, *, alpha, geom):
    M, K = y.shape
    D = w.shape[1]
    tpb, ctx_tiles, nbatch = geom
    return pl.pallas_call(
        functools.partial(_proj_ln_kernel, alpha=alpha),
        out_shape=jax.ShapeDtypeStruct((M, D), F32),
        grid=(M // TM,),
        in_specs=[
            pl.BlockSpec((TM, K), lambda i: (i, 0)),
            pl.BlockSpec((K, D), lambda i: (0, 0)),
            pl.BlockSpec((TM, D), lambda i: (i, 0)),
            pl.BlockSpec((None, None, 3, D), lambda i: (ls, _mod_row(i, tpb, ctx_tiles, nbatch), 0, 0)),
            pl.BlockSpec((1, D), lambda i: (0, 0)),
            pl.BlockSpec((1, D), lambda i: (0, 0)),
        ],
        out_specs=pl.BlockSpec((TM, D), lambda i: (i, 0)),
        scratch_shapes=[pltpu.VMEM((TM, D), F32)],
        compiler_params=_cparams(("parallel",), VMEM_LIMIT),
        name="proj_ln",
    )(y, w, x, mods, ln_g.reshape(1, D), ln_b.reshape(1, D))


def _pack_bf16_pair(lo, hi):
    lo_w = lax.bitcast_convert_type(lo.astype(BF16).astype(F32), jnp.uint32) >> 16
    hi_w = lax.bitcast_convert_type(hi.astype(BF16).astype(F32), jnp.uint32) & jnp.uint32(0xFFFF0000)
    return lo_w | hi_w


def _unpack_bf16_pair(w):
    lo = lax.bitcast_convert_type(w << 16, F32)
    hi = lax.bitcast_convert_type(w & jnp.uint32(0xFFFF0000), F32)
    return lo, hi


def _index_copy(idx_hbm, idx_smem, sems, tile):
    return pltpu.make_async_copy(idx_hbm.at[tile], idx_smem.at[tile % 2], sems.at[tile % 2])


def _combine_ln_kernel(dest_hbm, ys_hbm, gate4_ref, x_ref, mod_ref, g_ref, b_ref, o_ref,
                       idx_smem, buf_ref, rows2d_ref, idx_sems, row_sems, *, alpha):
    i = pl.program_id(0)
    n = pl.num_programs(0)

    def gather_rows(tile):
        slot = tile % 2

        def issue(r, carry):
            for k in range(TOP_K):
                row = idx_smem[slot, r * TOP_K + k]
                pltpu.make_async_copy(ys_hbm.at[row], buf_ref.at[slot, k, r],
                                      row_sems.at[slot]).start(priority=k % 2)
            return carry

        lax.fori_loop(0, TM, issue, 0, unroll=8)

    @pl.when(i == 0)
    def _():
        cp = _index_copy(dest_hbm, idx_smem, idx_sems, 0)
        cp.start()
        cp.wait()
        gather_rows(0)

        @pl.when(n > 1)
        def _():
            _index_copy(dest_hbm, idx_smem, idx_sems, 1).start()

    @pl.when(i + 1 < n)
    def _():
        _index_copy(dest_hbm, idx_smem, idx_sems, i + 1).wait()
        gather_rows(i + 1)

        @pl.when(i + 2 < n)
        def _():
            _index_copy(dest_hbm, idx_smem, idx_sems, i + 2).start()

    slot = i % 2
    for k in range(TOP_K):
        pltpu.make_async_copy(ys_hbm.at[pl.ds(0, TM)], buf_ref.at[slot, k], row_sems.at[slot]).wait()
    y_lo = y_hi = None
    for k in range(TOP_K):
        rows2d_ref[...] = buf_ref[slot, k].reshape(rows2d_ref.shape)
        lo, hi = _unpack_bf16_pair(rows2d_ref[...])
        gk = gate4_ref[:, k:k + 1]
        y_lo = lo * gk if y_lo is None else y_lo + lo * gk
        y_hi = hi * gk if y_hi is None else y_hi + hi * gk
    y = jnp.concatenate([y_lo, y_hi], axis=1)
    z = alpha * x_ref[...] + mod_ref[2:3, :] * y
    o_ref[...] = _layer_norm_rows(z, g_ref[...], b_ref[...])


def _combine_ln(dest_tiles, ys, gate4, x, mods, ls, ln_g, ln_b, *, alpha, geom, latent_only):
    D = x.shape[1]
    n_tiles, row_tile, mod_row = _token_tiles(geom, latent_only)
    return pl.pallas_call(
        functools.partial(_combine_ln_kernel, alpha=alpha),
        out_shape=jax.ShapeDtypeStruct((n_tiles * TM, D), F32),
        grid=(n_tiles,),
        in_specs=[
            pl.BlockSpec(memory_space=pl.ANY),
            pl.BlockSpec(memory_space=pl.ANY),
            pl.BlockSpec((TM, LANES), lambda i: (i, 0)),
            pl.BlockSpec((TM, D), lambda i: (row_tile(i), 0)),
            pl.BlockSpec((None, None, 3, D), lambda i: (ls, mod_row(i), 0, 0)),
            pl.BlockSpec((1, D), lambda i: (0, 0)),
            pl.BlockSpec((1, D), lambda i: (0, 0)),
        ],
        out_specs=pl.BlockSpec((TM, D), lambda i: (i, 0)),
        scratch_shapes=[
            pltpu.SMEM((2, TM * TOP_K), jnp.int32),
            pltpu.VMEM((2, TOP_K, TM, 1, D // 2), jnp.uint32),
            pltpu.VMEM((TM, D // 2), jnp.uint32),
            pltpu.SemaphoreType.DMA((2,)),
            pltpu.SemaphoreType.DMA((2,)),
        ],
        compiler_params=_cparams(("arbitrary",), VMEM_LIMIT),
        name="combine_ln",
    )(dest_tiles, ys, gate4, x, mods, ln_g.reshape(1, D), ln_b.reshape(1, D))


def _dispatch_kernel(pad_lo_ref, nused_ref, dest_hbm, x_ref, mod_ref, xs_hbm, idx_smem, a_ref, zero_ref,
                     idx_sems, row_sems, *, bm):
    i = pl.program_id(0)
    n = pl.num_programs(0)
    slot = i % 2

    def wait_rows(s):
        for _ in range(TOP_K):
            pltpu.make_async_copy(a_ref.at[s], xs_hbm.at[pl.ds(0, TM)], row_sems.at[s]).wait()

    @pl.when(i == 0)
    def _():
        _index_copy(dest_hbm, idx_smem, idx_sems, 0).start()
        zero_ref[...] = jnp.zeros_like(zero_ref)
        n_exp = pad_lo_ref.shape[0]
        n_blk = xs_hbm.shape[0] // bm

        def zero_block(row0):
            return pltpu.make_async_copy(zero_ref, xs_hbm.at[pl.ds(pl.multiple_of(row0, bm), bm)],
                                         row_sems.at[0])

        for e in range(n_exp):
            @pl.when(pad_lo_ref[e] >= 0)
            def _():
                zero_block(pad_lo_ref[e]).start()

        def fill_tail(b, carry):
            zero_block(b * bm).start()
            return carry

        lax.fori_loop(nused_ref[0], n_blk, fill_tail, 0)
        for e in range(n_exp):
            @pl.when(pad_lo_ref[e] >= 0)
            def _():
                zero_block(0).wait()

        def wait_tail(b, carry):
            zero_block(0).wait()
            return carry

        lax.fori_loop(nused_ref[0], n_blk, wait_tail, 0)

    @pl.when(i + 1 < n)
    def _():
        _index_copy(dest_hbm, idx_smem, idx_sems, i + 1).start()

    @pl.when(i >= 2)
    def _():
        wait_rows(slot)

    shift = mod_ref[0:1, :]
    scale = mod_ref[1:2, :]
    a = x_ref[...] * (1.0 + scale) + shift
    half = a.shape[1] // 2
    a_ref[slot] = _pack_bf16_pair(a[:, :half], a[:, half:]).reshape(a_ref.shape[1:])
    _index_copy(dest_hbm, idx_smem, idx_sems, i).wait()

    def issue(r, carry):
        for k in range(TOP_K):
            row = idx_smem[slot, r * TOP_K + k]
            pltpu.make_async_copy(a_ref.at[slot, r], xs_hbm.at[row], row_sems.at[slot]).start(priority=k % 2)
        return carry

    lax.fori_loop(0, TM, issue, 0, unroll=8)

    @pl.when(i == n - 1)
    def _():
        wait_rows(slot)

        @pl.when(n > 1)
        def _():
            wait_rows(1 - slot)


def _dispatch(pad_lo, n_used, dest_tiles, x, mods, ls, n_rows, *, geom, latent_only):
    D = x.shape[1]
    n_tiles, row_tile, mod_row = _token_tiles(geom, latent_only)
    bm = EXPERT_BM
    return pl.pallas_call(
        functools.partial(_dispatch_kernel, bm=bm),
        out_shape=jax.ShapeDtypeStruct((n_rows, 1, D // 2), jnp.uint32),
        grid_spec=pltpu.PrefetchScalarGridSpec(
            num_scalar_prefetch=2,
            grid=(n_tiles,),
            in_specs=[
                pl.BlockSpec(memory_space=pl.ANY),
                pl.BlockSpec((TM, D), lambda i, p, n: (row_tile(i), 0)),
                pl.BlockSpec((None, None, 3, D), lambda i, p, n: (ls, mod_row(i), 0, 0)),
            ],
            out_specs=pl.BlockSpec(memory_space=pl.ANY),
            scratch_shapes=[
                pltpu.SMEM((2, TM * TOP_K), jnp.int32),
                pltpu.VMEM((2, TM, 1, D // 2), jnp.uint32),
                pltpu.VMEM((bm, 1, D // 2), jnp.uint32),
                pltpu.SemaphoreType.DMA((2,)),
                pltpu.SemaphoreType.DMA((2,)),
            ],
        ),
        compiler_params=_cparams(("arbitrary",), VMEM_LIMIT),
        name="moe_dispatch",
    )(pad_lo, n_used, dest_tiles, x, mods)


def _sigmoid(x):
    return 0.5 * jnp.tanh(0.5 * x) + 0.5


def _softplus(x):
    return jnp.maximum(x, 0.0) + jnp.log(1.0 + jnp.exp(-jnp.abs(x)))


def _gelu_tanh(x):
    c = math.sqrt(2.0 / math.pi)
    return 0.5 * x * (1.0 + jnp.tanh(c * (x + 0.044715 * (x * x * x))))


def _group_scan(a, b, row, reverse):
    for s in (1, 2, 4):
        if reverse:
            keep = row < SUBLANES - s
            shift = SUBLANES - s
        else:
            keep = row >= s
            shift = s
        a_sh = jnp.where(keep, pltpu.roll(a, shift, axis=0), 1.0)
        b_sh = jnp.where(keep, pltpu.roll(b, shift, axis=0), 0.0)
        b = b + a * b_sh
        a = a * a_sh
    return a, b


def _lru_kernel(ug_ref, ur_ref, cw_ref, cb_ref, wg_ref, bg_ref, lam_ref, o_ref,
                a0_ref, b0_ref, a1_ref, b1_ref, *, ctx_len):
    T, W = ur_ref.shape
    CH = 256
    n_chunks = T // CH
    ctx_chunks = ctx_len // CH
    row_c = lax.broadcasted_iota(jnp.int32, (CH, W), 0)
    kexp = (-0.5 * LRU_C * math.log2(math.e)) * _softplus(-lam_ref[...])

    def coeff_chunk(c, carry):
        c0 = pl.multiple_of(c * CH, CH)
        cur = ur_ref[pl.ds(c0, CH), :]
        prev8 = ur_ref[pl.ds(pl.multiple_of(jnp.maximum(c0 - SUBLANES, 0), SUBLANES), SUBLANES), :]
        next8 = ur_ref[pl.ds(pl.multiple_of(jnp.minimum(c0 + CH, T - SUBLANES), SUBLANES), SUBLANES), :]
        seg_start = jnp.logical_or(c == 0, c == ctx_chunks)
        seg_end = jnp.logical_or(c == ctx_chunks - 1, c == n_chunks - 1)
        p_last = jnp.where(seg_start, 0.0, prev8[SUBLANES - 1:SUBLANES, :])
        n_first = jnp.where(seg_end, 0.0, next8[0:1, :])
        n_second = jnp.where(seg_end, 0.0, next8[1:2, :])
        u_m1 = jnp.where(row_c == 0, p_last, pltpu.roll(cur, 1, axis=0))
        u_p1 = jnp.where(row_c == CH - 1, n_first, pltpu.roll(cur, CH - 1, axis=0))
        u_p2 = jnp.where(row_c == CH - 2, n_first,
                         jnp.where(row_c == CH - 1, n_second, pltpu.roll(cur, CH - 2, axis=0)))
        r = cb_ref[...] + u_m1 * cw_ref[0:1, :]
        r = r + cur * cw_ref[1:2, :]
        r = r + u_p1 * cw_ref[2:3, :]
        r = r + u_p2 * cw_ref[3:4, :]
        gts = jnp.dot(r.astype(BF16), wg_ref[0].astype(BF16), preferred_element_type=F32) + bg_ref[0]
        half_r = 0.5 * r
        for d, (a_ref, b_ref) in enumerate(((a0_ref, b0_ref), (a1_ref, b1_ref))):
            t_rec = jnp.tanh(gts[:, (2 * d) * W:(2 * d + 1) * W])
            t_in = jnp.tanh(gts[:, (2 * d + 1) * W:(2 * d + 2) * W])
            kd = kexp[d:d + 1, :]
            a = jnp.exp2(kd * t_rec + kd)
            a_ref[pl.ds(c0, CH), :] = a
            b_ref[pl.ds(c0, CH), :] = jnp.sqrt(jnp.maximum(1.0 - a * a, 0.0)) * (half_r * t_in + half_r)
        return carry

    lax.fori_loop(0, n_chunks, coeff_chunk, 0)

    row8 = lax.broadcasted_iota(jnp.int32, (SUBLANES, W), 0)
    n_groups = T // SUBLANES
    ctx_groups = ctx_len // SUBLANES

    def scan_step(k, carry):
        hf, hb = carry
        gf = pl.multiple_of(k * SUBLANES, SUBLANES)
        a, b = _group_scan(a0_ref[pl.ds(gf, SUBLANES), :], b0_ref[pl.ds(gf, SUBLANES), :], row8, False)
        h = a * hf + b
        a0_ref[pl.ds(gf, SUBLANES), :] = h
        hf = jnp.broadcast_to(h[SUBLANES - 1:SUBLANES, :], h.shape)
        g = jnp.where(k < ctx_groups, ctx_groups - 1 - k, n_groups - 1 - (k - ctx_groups))
        gb = pl.multiple_of(g * SUBLANES, SUBLANES)
        a, b = _group_scan(a1_ref[pl.ds(gb, SUBLANES), :], b1_ref[pl.ds(gb, SUBLANES), :], row8, True)
        h = a * hb + b
        a1_ref[pl.ds(gb, SUBLANES), :] = h
        hb = jnp.broadcast_to(h[0:1, :], h.shape)
        return hf, hb

    zero = jnp.zeros((SUBLANES, W), F32)
    lax.fori_loop(0, n_groups, scan_step, (zero, zero), unroll=4)

    def out_chunk(c, carry):
        c0 = pl.multiple_of(c * CH, CH)
        y = a0_ref[pl.ds(c0, CH), :] + a1_ref[pl.ds(c0, CH), :]
        o_ref[pl.ds(c0, CH), :] = (y * _gelu_tanh(ug_ref[pl.ds(c0, CH), :])).astype(o_ref.dtype)
        return carry

    lax.fori_loop(0, n_chunks, out_chunk, 0)


def _lru_mix(u, conv_w, conv_b, gate_w, gate_b, lam, *, nbatch, ctx_len):
    M, W2 = u.shape
    W = W2 // 2
    T = M // nbatch
    nblk = W // LRU_BLOCK_W
    bw = LRU_BLOCK_W
    wg = 0.5 * jnp.transpose(gate_w, (2, 3, 0, 1, 4)).reshape(nblk, bw, 4 * bw)
    bg = 0.5 * jnp.transpose(gate_b.reshape(2, 2, nblk, bw), (2, 0, 1, 3)).reshape(nblk, 1, 4 * bw)
    return pl.pallas_call(
        functools.partial(_lru_kernel, ctx_len=ctx_len),
        out_shape=jax.ShapeDtypeStruct((M, W), BF16),
        grid=(nbatch, nblk),
        in_specs=[
            pl.BlockSpec((T, bw), lambda b, n: (b, n)),
            pl.BlockSpec((T, bw), lambda b, n: (b, nblk + n)),
            pl.BlockSpec((CONV_W, bw), lambda b, n: (0, n)),
            pl.BlockSpec((1, bw), lambda b, n: (0, n)),
            pl.BlockSpec((1, bw, 4 * bw), lambda b, n: (n, 0, 0)),
            pl.BlockSpec((1, 1, 4 * bw), lambda b, n: (n, 0, 0)),
            pl.BlockSpec((2, bw), lambda b, n: (0, n)),
        ],
        out_specs=pl.BlockSpec((T, bw), lambda b, n: (b, n)),
        scratch_shapes=[pltpu.VMEM((T, bw), F32) for _ in range(4)],
        compiler_params=_cparams(("parallel", "parallel"), VMEM_LIMIT),
        name="lru_mix",
    )(u, u, conv_w, conv_b.reshape(1, W), wg, bg, lam)


def _attn_kernel(sink_ref, q_ref, k_ref, v_ref, o_ref, k2_ref, v2_ref, p_ref, *, ctx_len, n_kv):
    QB = q_ref.shape[0]
    T = k_ref.shape[0]
    n_lat = (T - ctx_len) // QB
    ctx_blocks = ctx_len // QB
    band = 3 * QB
    pairs = GROUP // 2
    qb = pl.program_id(1)
    j = qb - ctx_blocks
    start = jnp.clip(j - 1, 0, n_lat - 3)
    off = pl.multiple_of(ctx_len + start * QB, QB)
    lo_half = lax.broadcasted_iota(jnp.int32, (1, LANES), 1) < HEAD_DIM
    nt = (((1,), (1,)), ((), ()))

    def attend(with_band):
        nk = ctx_len + (band if with_band else 0)
        if with_band:
            qpos = j * QB + lax.broadcasted_iota(jnp.int32, (QB, band), 0)
            kpos = start * QB + lax.broadcasted_iota(jnp.int32, (QB, band), 1)
            valid = jnp.abs(qpos - kpos) <= WINDOW
        for h in range(n_kv):
            slab = slice((h // 2) * LANES, (h // 2 + 1) * LANES)

            def two_blocks(x):
                swapped = jnp.concatenate([x[:, HEAD_DIM:], x[:, :HEAD_DIM]], axis=1)
                own, other = (x, swapped) if h % 2 == 0 else (swapped, x)
                zero = jnp.zeros_like(x)
                return jnp.where(lo_half, own, zero), jnp.where(lo_half, zero, other)

            for dst, src in ((k2_ref, k_ref), (v2_ref, v_ref)):
                top_c, bot_c = two_blocks(src[0:ctx_len, slab])
                dst[0:ctx_len, :] = top_c
                dst[nk:nk + ctx_len, :] = bot_c
                if with_band:
                    top_b, bot_b = two_blocks(src[pl.ds(off, band), slab])
                    dst[ctx_len:nk, :] = top_b
                    dst[nk + ctx_len:2 * nk, :] = bot_b
            c_q = h * GROUP * HEAD_DIM
            qs = jnp.concatenate([q_ref[:, c_q + p * LANES:c_q + (p + 1) * LANES] for p in range(pairs)], axis=0)
            s = lax.dot_general(qs, k2_ref[0:2 * nk, :], nt, preferred_element_type=F32)
            invs = []
            for p in range(pairs):
                rows = slice(p * QB, (p + 1) * QB)
                inv_pair = []
                for half in range(2):
                    c0 = half * nk
                    sink = sink_ref[h * GROUP + 2 * p + half] * LOG2_E
                    s_c = s[rows, c0:c0 + ctx_len]
                    m = jnp.maximum(jnp.max(s_c, axis=-1, keepdims=True), sink)
                    if with_band:
                        s_b = jnp.where(valid, s[rows, c0 + ctx_len:c0 + nk], NEG_INF)
                        m = jnp.maximum(m, jnp.max(s_b, axis=-1, keepdims=True))
                    p_c = jnp.exp2(s_c - m)
                    denom = jnp.sum(p_c, axis=-1, keepdims=True) + jnp.exp2(sink - m)
                    p_ref[rows, c0:c0 + ctx_len] = p_c.astype(BF16)
                    if with_band:
                        p_b = jnp.exp2(s_b - m)
                        denom = denom + jnp.sum(p_b, axis=-1, keepdims=True)
                        p_ref[rows, c0 + ctx_len:c0 + nk] = p_b.astype(BF16)
                    inv_pair.append(1.0 / denom)
                invs.append(inv_pair)
            o = jnp.dot(p_ref[:, 0:2 * nk], v2_ref[0:2 * nk, :], preferred_element_type=F32)
            for p in range(pairs):
                inv = jnp.where(lo_half, invs[p][0], invs[p][1])
                o_ref[:, c_q + p * LANES:c_q + (p + 1) * LANES] = (o[p * QB:(p + 1) * QB] * inv).astype(o_ref.dtype)

    @pl.when(qb >= ctx_blocks)
    def _():
        attend(True)

    @pl.when(qb < ctx_blocks)
    def _():
        attend(False)


def _attention(qkv, sinks, *, nbatch, ctx_len, q_dim, kv_dim):
    M = qkv.shape[0]
    T = M // nbatch
    nqb = T // Q_BLOCK
    kcol = q_dim // kv_dim
    return pl.pallas_call(
        functools.partial(_attn_kernel, ctx_len=ctx_len, n_kv=kv_dim // HEAD_DIM),
        out_shape=jax.ShapeDtypeStruct((M, q_dim), BF16),
        grid_spec=pltpu.PrefetchScalarGridSpec(
            num_scalar_prefetch=1,
            grid=(nbatch, nqb),
            in_specs=[
                pl.BlockSpec((Q_BLOCK, q_dim), lambda b, q, s: (b * nqb + q, 0)),
                pl.BlockSpec((T, kv_dim), lambda b, q, s: (b, kcol)),
                pl.BlockSpec((T, kv_dim), lambda b, q, s: (b, kcol + 1)),
            ],
            out_specs=pl.BlockSpec((Q_BLOCK, q_dim), lambda b, q, s: (b * nqb + q, 0)),
            scratch_shapes=[
                pltpu.VMEM((2 * (ctx_len + 3 * Q_BLOCK), LANES), BF16),
                pltpu.VMEM((2 * (ctx_len + 3 * Q_BLOCK), LANES), BF16),
                pltpu.VMEM((GROUP // 2 * Q_BLOCK, 2 * (ctx_len + 3 * Q_BLOCK)), BF16),
            ],
        ),
        compiler_params=_cparams(("parallel", "parallel"), VMEM_LIMIT),
        name="swa_attention",
    )(sinks, qkv, qkv, qkv)


def _split_bf16(x):
    hi = x.astype(BF16)
    lo = (x - hi.astype(F32)).astype(BF16)
    return hi, lo


def _router_kernel(x_ref, mod_ref, wr_ref, br_ref, e_ref, g_ref, r_ref, cnt_ref, carry_ref):
    i = pl.program_id(0)

    @pl.when(i == 0)
    def _():
        carry_ref[...] = jnp.zeros_like(carry_ref)

    shift = mod_ref[0:1, :]
    scale = mod_ref[1:2, :]
    a = x_ref[...] * (1.0 + scale) + shift
    a_hi, a_lo = _split_bf16(a)
    ne = br_ref.shape[1]
    hi_terms = jnp.dot(a_hi, wr_ref[...], preferred_element_type=F32)
    lo_term = jnp.dot(a_lo, wr_ref[:, 0:ne], preferred_element_type=F32)
    logits = (hi_terms[:, 0:ne] + (lo_term + hi_terms[:, ne:2 * ne])) + br_ref[...]
    tm = logits.shape[0]
    lane_e = lax.broadcasted_iota(jnp.int32, (tm, ne), 1)
    lane_o = lax.broadcasted_iota(jnp.int32, (tm, LANES), 1)
    vals, idxs, hots = [], [], []
    cur = logits
    for k in range(TOP_K):
        m = jnp.max(cur, axis=-1, keepdims=True)
        idx = jnp.min(jnp.where(cur == m, lane_e, ne), axis=-1, keepdims=True)
        hot = lane_e == idx
        vals.append(m)
        idxs.append(idx)
        hots.append(hot)
        cur = jnp.where(hot, -jnp.inf, cur)
    exps = [jnp.exp(v - vals[0]) for v in vals]
    inv = 1.0 / (exps[0] + exps[1] + exps[2] + exps[3])
    msum = jnp.zeros((tm, ne), F32)
    for hot in hots:
        msum = msum + hot.astype(F32)
    rr = lax.broadcasted_iota(jnp.int32, (tm, tm), 0)
    cc = lax.broadcasted_iota(jnp.int32, (tm, tm), 1)
    tri = (cc < rr).astype(BF16)
    before = jnp.dot(tri, msum.astype(BF16), preferred_element_type=F32) + carry_ref[...]
    e_out = jnp.zeros((tm, LANES), jnp.int32)
    g_out = jnp.zeros((tm, LANES), F32)
    r_out = jnp.zeros((tm, LANES), jnp.int32)
    for k in range(TOP_K):
        rank = jnp.sum(jnp.where(hots[k], before, 0.0), axis=-1, keepdims=True).astype(jnp.int32)
        e_out = jnp.where(lane_o == k, idxs[k], e_out)
        g_out = jnp.where(lane_o == k, exps[k] * inv, g_out)
        r_out = jnp.where(lane_o == k, rank, r_out)
    e_ref[...] = e_out
    g_ref[...] = g_out
    r_ref[...] = r_out
    carry_ref[...] = carry_ref[...] + jnp.sum(msum, axis=0, keepdims=True)
    cnt_ref[...] = carry_ref[...].astype(jnp.int32)


def _router(x, mods, ls, w_r, b_r, *, geom, latent_only):
    D = x.shape[1]
    ne = w_r.shape[1]
    n_tiles, row_tile, mod_row = _token_tiles(geom, latent_only)
    M = n_tiles * TM
    return pl.pallas_call(
        _router_kernel,
        out_shape=(
            jax.ShapeDtypeStruct((M, LANES), jnp.int32),
            jax.ShapeDtypeStruct((M, LANES), F32),
            jax.ShapeDtypeStruct((M, LANES), jnp.int32),
            jax.ShapeDtypeStruct((1, ne), jnp.int32),
        ),
        grid=(n_tiles,),
        in_specs=[
            pl.BlockSpec((TM, D), lambda i: (row_tile(i), 0)),
            pl.BlockSpec((None, None, 3, D), lambda i: (ls, mod_row(i), 0, 0)),
            pl.BlockSpec((D, 2 * ne), lambda i: (0, 0)),
            pl.BlockSpec((1, ne), lambda i: (0, 0)),
        ],
        out_specs=(
            pl.BlockSpec((TM, LANES), lambda i: (i, 0)),
            pl.BlockSpec((TM, LANES), lambda i: (i, 0)),
            pl.BlockSpec((TM, LANES), lambda i: (i, 0)),
            pl.BlockSpec((1, ne), lambda i: (0, 0)),
        ),
        scratch_shapes=[pltpu.VMEM((1, ne), F32)],
        compiler_params=_cparams(("arbitrary",), VMEM_LIMIT),
        name="moe_router",
    )(x, mods, jnp.concatenate(_split_bf16(w_r), axis=1), b_r.reshape(1, ne))


def _expert_kernel(blk_e_ref, slot_ref, first_ref, next_e_ref, nused_ref, xs_ref, bgu_ref, bdn_ref,
                   wgu_hbm, wdn_hbm, o_ref, wgu_buf, wdn_buf, x2d_ref, sem_gu, sem_dn, *, layer):
    b = pl.program_id(0)
    de, d = wdn_buf.shape[1:]
    half = d // 2

    def fetch(expert, slot):
        cps = []
        for hbm, buf, sem, chunks in ((wgu_hbm, wgu_buf, sem_gu, WEIGHT_DMA_CHUNKS),
                                      (wdn_hbm, wdn_buf, sem_dn, WEIGHT_DMA_CHUNKS // 2)):
            rows = buf.shape[1] // chunks
            for c in range(chunks):
                rs = pl.ds(c * rows, rows)
                cps.append(pltpu.make_async_copy(hbm.at[layer, expert, rs], buf.at[slot, rs], sem.at[slot]))
        return cps

    @pl.when(b < nused_ref[0])
    def _():
        slot = slot_ref[b]

        @pl.when(b == 0)
        def _():
            for c, cp in enumerate(fetch(blk_e_ref[0], 0)):
                cp.start(priority=c % 2)

        @pl.when(first_ref[b] == 1)
        def _():
            @pl.when(next_e_ref[b] >= 0)
            def _():
                for c, cp in enumerate(fetch(next_e_ref[b], 1 - slot)):
                    cp.start(priority=c % 2)
            for cp in fetch(blk_e_ref[b], slot):
                cp.wait()

        x2d_ref[...] = xs_ref[...].reshape(x2d_ref.shape)
        x_lo, x_hi = (v.astype(BF16) for v in _unpack_bf16_pair(x2d_ref[...]))
        hs = []
        for c in range(2 * de // 256):
            sl = slice(c * 256, (c + 1) * 256)
            acc = jnp.dot(x_lo, wgu_buf[slot, 0:half, sl].astype(BF16), preferred_element_type=F32)
            acc = acc + jnp.dot(x_hi, wgu_buf[slot, half:d, sl].astype(BF16), preferred_element_type=F32)
            hs.append(acc + bgu_ref[:, sl])
        h = jnp.concatenate(hs, axis=1)
        g = jnp.minimum(h[:, :de], SWIGLU_LIMIT)
        u = jnp.clip(h[:, de:], -SWIGLU_LIMIT, SWIGLU_LIMIT)
        act = ((u + 1.0) * (g * _sigmoid(SWIGLU_ALPHA * g))).astype(BF16)
        ys = []
        for c in range(d // 512):
            sl = slice(c * 512, (c + 1) * 512)
            ys.append(jnp.dot(act, wdn_buf[slot, :, sl].astype(BF16), preferred_element_type=F32) + bdn_ref[:, sl])
        nh = len(ys) // 2
        for c in range(nh):
            packed = _pack_bf16_pair(ys[c], ys[c + nh])
            o_ref[:, :, c * 512:(c + 1) * 512] = packed.reshape(packed.shape[0], 1, 512)

    @pl.when(b >= nused_ref[0])
    def _():
        o_ref[...] = jnp.zeros_like(o_ref)


def _expert_ffn(xs, blk_tabs, n_used, layer, w_gu, b_gu, w_dn, b_dn):
    P = xs.shape[0]
    _, ne, D, de2 = w_gu.shape
    de = de2 // 2
    n_blk = P // EXPERT_BM
    blk_e, slot, first, next_e = blk_tabs
    return pl.pallas_call(
        functools.partial(_expert_kernel, layer=layer),
        out_shape=jax.ShapeDtypeStruct((P, 1, D // 2), jnp.uint32),
        grid_spec=pltpu.PrefetchScalarGridSpec(
            num_scalar_prefetch=5,
            grid=(n_blk,),
            in_specs=[
                pl.BlockSpec((EXPERT_BM, 1, D // 2), lambda b, e, s, f, x, n: (jnp.minimum(b, n[0] - 1), 0, 0)),
                pl.BlockSpec((None, None, 1, de2), lambda b, e, s, f, x, n: (layer, e[b], 0, 0)),
                pl.BlockSpec((None, None, 1, D), lambda b, e, s, f, x, n: (layer, e[b], 0, 0)),
                pl.BlockSpec(memory_space=pl.ANY),
                pl.BlockSpec(memory_space=pl.ANY),
            ],
            out_specs=pl.BlockSpec((EXPERT_BM, 1, D // 2), lambda b, e, s, f, x, n: (b, 0, 0)),
            scratch_shapes=[
                pltpu.VMEM((2, D, de2), F32),
                pltpu.VMEM((2, de, D), F32),
                pltpu.VMEM((EXPERT_BM, D // 2), jnp.uint32),
                pltpu.SemaphoreType.DMA((2,)),
                pltpu.SemaphoreType.DMA((2,)),
            ],
        ),
        compiler_params=_cparams(("arbitrary",), VMEM_LIMIT),
        name="expert_ffn",
    )(blk_e, slot, first, next_e, n_used, xs, b_gu.reshape(b_gu.shape[0], ne, 1, de2),
      b_dn.reshape(b_dn.shape[0], ne, 1, D), w_gu, w_dn)


def _moe_layer(x, mods, ls, layer, w_r, b_r, w_gu, b_gu, w_dn, b_dn, ln_g, ln_b, *, alpha, geom, latent_only):
    ne = w_r.shape[1]
    bm = EXPERT_BM
    e128, g128, r128, counts = _router(x, mods, ls, w_r, b_r, geom=geom, latent_only=latent_only)
    M = e128.shape[0]
    top_e = e128[:, :TOP_K]
    rank = r128[:, :TOP_K]
    counts = counts[0]
    padded = ((counts + bm - 1) // bm) * bm
    ends = jnp.cumsum(padded)
    pstart = ends - padded
    experts = jnp.arange(ne, dtype=jnp.int32)
    dest = jnp.sum(jnp.where(top_e[..., None] == experts, pstart, 0), axis=-1) + rank
    dest_tiles = dest.astype(jnp.int32).reshape(M // TM, TM * TOP_K)
    A = M * TOP_K
    P = (-(-A // bm)) * bm + ne * bm
    n_blk = P // bm
    n_used = (ends[-1] // bm).astype(jnp.int32).reshape(1)
    blk = jnp.arange(n_blk, dtype=jnp.int32)
    blk_row = jnp.minimum(blk, n_used[0] - 1) * bm
    blk_e = jnp.minimum(jnp.sum(ends[None, :] <= blk_row[:, None], axis=1), ne - 1).astype(jnp.int32)
    first = jnp.logical_and(blk < n_used[0],
                            jnp.concatenate([jnp.ones((1,), bool), blk_e[1:] != blk_e[:-1]])).astype(jnp.int32)
    slot = ((jnp.cumsum(first) - 1) % 2).astype(jnp.int32)
    later = jnp.logical_and(experts[None, :] > experts[:, None], (padded > 0)[None, :])
    next_nonempty = jnp.min(jnp.where(later, experts[None, :], ne), axis=1)
    next_nonempty = jnp.where(next_nonempty == ne, -1, next_nonempty)
    next_e = jnp.sum(jnp.where(blk_e[:, None] == experts[None, :], next_nonempty[None, :], 0), axis=1).astype(jnp.int32)
    pad_lo = jnp.where(padded > 0, ends - bm, -1).astype(jnp.int32)
    xs = _dispatch(pad_lo, n_used, dest_tiles, x, mods, ls, P, geom=geom, latent_only=latent_only)
    ys = _expert_ffn(xs, (blk_e, slot, first, next_e), n_used, layer, w_gu, b_gu, w_dn, b_dn)
    return _combine_ln(dest_tiles, ys, g128, x, mods, ls, ln_g, ln_b, alpha=alpha, geom=geom,
                       latent_only=latent_only)


def _rope_tables(seq, ctx_len):
    rows = seq // GRID_W
    row_pos = jnp.repeat(jnp.arange(rows), GRID_W).astype(F32)
    col_pos = jnp.tile(jnp.arange(GRID_W), rows).astype(F32)
    n = HEAD_DIM // 2
    freqs = ROPE_BASE ** (-jnp.arange(0, n, 2, dtype=F32) / n)
    ang_r = row_pos[:, None] * freqs[None, :]
    ang_c = col_pos[:, None] * freqs[None, :]
    cos = jnp.concatenate([jnp.cos(ang_r), jnp.cos(ang_r), jnp.cos(ang_c), jnp.cos(ang_c)], axis=1)
    sin = jnp.concatenate([-jnp.sin(ang_r), jnp.sin(ang_r), -jnp.sin(ang_c), jnp.sin(ang_c)], axis=1)
    cos = jnp.concatenate([jnp.ones((ctx_len, HEAD_DIM), F32), cos], axis=0)
    sin = jnp.concatenate([jnp.zeros((ctx_len, HEAD_DIM), F32), sin], axis=0)
    cos = jnp.tile(cos, (1, LANES // HEAD_DIM))
    sin = jnp.tile(sin, (1, LANES // HEAD_DIM))
    scale = HEAD_DIM ** -0.5 * LOG2_E
    return jnp.stack([cos * scale, sin * scale, cos, sin], axis=0)


def kernel(x, c, ctx, c_ctx, ada_w, ada_b, ln_g, ln_b, lru_w_in, lru_conv_w, lru_conv_b, lru_gate_w, lru_gate_b, lru_lambda, lru_w_out, attn_w_qkv, attn_sinks, attn_w_o, router_w, router_b, moe_w_gu, moe_b_gu, moe_w_down, moe_b_down):
    B, S, D = x.shape
    C = ctx.shape[1]
    depth = ada_w.shape[0]
    T = C + S
    assert C % TM == 0 and S % TM == 0 and C % Q_BLOCK == 0
    geom = (T // TM, C // TM, B)
    alpha = (2.0 * depth) ** 0.25
    kv_dim = (attn_w_qkv.shape[2] - D) // 2

    act = jax.nn.silu(jnp.concatenate([c, c_ctx[None, :]], axis=0))
    mods = _ada_params(act, ada_w.reshape(depth * 2, D, 3 * D), ada_b.reshape(depth * 2, 3 * D))
    mods = mods.reshape(depth * 2, SUBLANES, 3, D)
    tab = _rope_tables(S, C)

    z = jnp.concatenate([ctx, x], axis=1).reshape(B * T, D)
    for i in range(depth):
        j = i // 2
        if i % 2 == 0:
            u = _mod_matmul(z, mods, 2 * i, lru_w_in[j].astype(BF16), tn=D, out_dtype=F32, geom=geom)
            y = _lru_mix(u, lru_conv_w[j], lru_conv_b[j], lru_gate_w[j], lru_gate_b[j], lru_lambda[j],
                         nbatch=B, ctx_len=C)
            w_out = lru_w_out[j]
        else:
            qkv = _mod_matmul(z, mods, 2 * i, attn_w_qkv[j].astype(BF16), tn=D + 2 * kv_dim, out_dtype=BF16,
                              geom=geom, rope=(tab, (D, kv_dim)))
            y = _attention(qkv, attn_sinks[j], nbatch=B, ctx_len=C, q_dim=D, kv_dim=kv_dim)
            w_out = attn_w_o[j]
        z = _proj_ln(y, w_out.astype(BF16), z, mods, 2 * i, ln_g[i, 0], ln_b[i, 0], alpha=alpha, geom=geom)
        z = _moe_layer(z, mods, 2 * i + 1, i, router_w[i], router_b[i], moe_w_gu, moe_b_gu,
                       moe_w_down, moe_b_down, ln_g[i, 1], ln_b[i, 1], alpha=alpha, geom=geom,
                       latent_only=(i == depth - 1))
    return z.reshape(B, S, D)
```

```python
import functools
import math

import jax
import jax.numpy as jnp
from jax import lax
from jax.experimental import pallas as pl
from jax.experimental.pallas import tpu as pltpu

GRID_W = 64
LN_EPS = 1e-5
LRU_BLOCK_W = 256
CONV_W = 4
LRU_C = 8.0
HEAD_DIM = 64
GROUP = 8
WINDOW = 128
Q_BLOCK = 128
ROPE_BASE = 10000.0
NEG_INF = -1e30
TOP_K = 4
SWIGLU_LIMIT = 7.0
SWIGLU_ALPHA = 1.702
LOG2_E = math.log2(math.e)

LANES = 128
SUBLANES = 8
TM = 256
EXPERT_BM = 256
WEIGHT_DMA_CHUNKS = 4
VMEM_LIMIT = 56 * 1024 * 1024

BF16 = jnp.bfloat16
F32 = jnp.float32


def _cparams(sem, vmem=None):
    return pltpu.CompilerParams(dimension_semantics=sem, vmem_limit_bytes=vmem)


def _ada_kernel(a_ref, w_ref, b_ref, o_ref):
    acc = jnp.dot(a_ref[...], w_ref[0].astype(BF16), preferred_element_type=F32)
    o_ref[0] = acc[0:SUBLANES] + acc[SUBLANES:2 * SUBLANES] + b_ref[0]


def _ada_params(act, w, bias):
    R, D = act.shape
    L, _, N = w.shape
    tn = 1024
    a8 = jnp.zeros((SUBLANES, D), F32).at[:R].set(act)
    hi = a8.astype(BF16)
    lo = (a8 - hi.astype(F32)).astype(BF16)
    a16 = jnp.concatenate([hi, lo], axis=0)
    return pl.pallas_call(
        _ada_kernel,
        out_shape=jax.ShapeDtypeStruct((L, SUBLANES, N), F32),
        grid=(L, N // tn),
        in_specs=[
            pl.BlockSpec((2 * SUBLANES, D), lambda l, n: (0, 0)),
            pl.BlockSpec((1, D, tn), lambda l, n: (l, 0, n)),
            pl.BlockSpec((1, 1, tn), lambda l, n: (l, 0, n)),
        ],
        out_specs=pl.BlockSpec((1, SUBLANES, tn), lambda l, n: (l, 0, n)),
        compiler_params=_cparams(("parallel", "parallel"), VMEM_LIMIT),
        name="ada_params",
    )(a16, w, bias.reshape(L, 1, N))


def _mod_row(i, tiles_per_batch, ctx_tiles, nbatch):
    return jnp.where(i % tiles_per_batch < ctx_tiles, nbatch, i // tiles_per_batch)


def _token_tiles(geom, latent_only):
    tpb, ctx_tiles, nbatch = geom
    if latent_only:
        lpb = tpb - ctx_tiles
        return lpb * nbatch, (lambda i: (i // lpb) * tpb + ctx_tiles + i % lpb), (lambda i: i // lpb)
    return tpb * nbatch, (lambda i: i), (lambda i: _mod_row(i, tpb, ctx_tiles, nbatch))


def _rope_chunk(y, cos, sin_signed, first_half):
    rot = jnp.where(first_half, pltpu.roll(y, LANES - 16, axis=1), pltpu.roll(y, 16, axis=1))
    return y * cos + rot * sin_signed


def _modmm_kernel(x_ref, mod_ref, w_ref, *rest, rope_cols):
    if rope_cols is None:
        (o_ref,) = rest
    else:
        tab_ref, o_ref = rest
    shift = mod_ref[0:1, :]
    scale = mod_ref[1:2, :]
    a = (x_ref[...] * (1.0 + scale) + shift).astype(BF16)
    tn = o_ref.shape[1]
    ch = 512 if tn % 512 == 0 else 256
    for c in range(tn // ch):
        acc = jnp.dot(a, w_ref[:, c * ch:(c + 1) * ch].astype(BF16), preferred_element_type=F32)
        if rope_cols is None:
            o_ref[:, c * ch:(c + 1) * ch] = acc.astype(o_ref.dtype)
        else:
            q_cols, k_cols = rope_cols
            lane = lax.broadcasted_iota(jnp.int32, (acc.shape[0], LANES), 1)
            first_half = (lane % 32) < 16
            for s in range(ch // LANES):
                col = c * ch + s * LANES
                y = acc[:, s * LANES:(s + 1) * LANES]
                if col < q_cols:
                    y = _rope_chunk(y, tab_ref[0], tab_ref[1], first_half)
                elif col < q_cols + k_cols:
                    y = _rope_chunk(y, tab_ref[2], tab_ref[3], first_half)
                o_ref[:, col:col + LANES] = y.astype(o_ref.dtype)


def _mod_matmul(x, mods, ls, w, *, tn, out_dtype, geom, rope=None):
    M, K = x.shape
    N = w.shape[1]
    tpb, ctx_tiles, nbatch = geom
    in_specs = [
        pl.BlockSpec((TM, K), lambda j, i: (i, 0)),
        pl.BlockSpec((None, None, 3, K), lambda j, i: (ls, _mod_row(i, tpb, ctx_tiles, nbatch), 0, 0)),
        pl.BlockSpec((K, tn), lambda j, i: (0, j)),
    ]
    args = [x, mods, w]
    rope_cols = None
    if rope is not None:
        tab, rope_cols = rope
        in_specs.append(pl.BlockSpec((4, TM, LANES), lambda j, i: (0, i % tpb, 0)))
        args.append(tab)
    return pl.pallas_call(
        functools.partial(_modmm_kernel, rope_cols=rope_cols),
        out_shape=jax.ShapeDtypeStruct((M, N), out_dtype),
        grid=(N // tn, M // TM),
        in_specs=in_specs,
        out_specs=pl.BlockSpec((TM, tn), lambda j, i: (i, j)),
        compiler_params=_cparams(("parallel", "parallel"), VMEM_LIMIT),
        name="mod_matmul",
    )(*args)


def _layer_norm_rows(z, g, b):
    mu = jnp.mean(z, axis=-1, keepdims=True)
    zc = z - mu
    var = jnp.mean(zc * zc, axis=-1, keepdims=True)
    return zc * lax.rsqrt(var + LN_EPS) * g + b


def _proj_ln_kernel(y_ref, w_ref, x_ref, mod_ref, g_ref, b_ref, o_ref, z_ref, *, alpha):
    gate = mod_ref[2:3, :]
    y = y_ref[...]
    n = o_ref.shape[1]
    ch = 512
    for c in range(n // ch):
        sl = slice(c * ch, (c + 1) * ch)
        acc = jnp.dot(y, w_ref[:, sl].astype(BF16), preferred_element_type=F32)
        z_ref[:, sl] = alpha * x_ref[:, sl] + gate[:, sl] * acc
    o_ref[...] = _layer_norm_rows(z_ref[...], g_ref[...], b_ref[...])


def _proj_ln(y, w, x, mods, ls, ln_g, ln_b, *, alpha, geom):
    M, K = y.shape
    D = w.shape[1]
    tpb, ctx_tiles, nbatch = geom
    return pl.pallas_call(
        functools.partial(_proj_ln_kernel, alpha=alpha),
        out_shape=jax.ShapeDtypeStruct((M, D), F32),
        grid=(M // TM,),
        in_specs=[
            pl.BlockSpec((TM, K), lambda i: (i, 0)),
            pl.BlockSpec((K, D), lambda i: (0, 0)),
            pl.BlockSpec((TM, D), lambda i: (i, 0)),
            pl.BlockSpec((None, None, 3, D), lambda i: (ls, _mod_row(i, tpb, ctx_tiles, nbatch), 0, 0)),
            pl.BlockSpec((1, D), lambda i: (0, 0)),
            pl.BlockSpec((1, D), lambda i: (0, 0)),
        ],
        out_specs=pl.BlockSpec((TM, D), lambda i: (i, 0)),
        scratch_shapes=[pltpu.VMEM((TM, D), F32)],
        compiler_params=_cparams(("parallel",), VMEM_LIMIT),
        name="proj_ln",
    )(y, w, x, mods, ln_g.reshape(1, D), ln_b.reshape(1, D))


def _pack_bf16_pair(lo, hi):
    lo_w = lax.bitcast_convert_type(lo.astype(BF16).astype(F32), jnp.uint32) >> 16
    hi_w = lax.bitcast_convert_type(hi.astype(BF16).astype(F32), jnp.uint32) & jnp.uint32(0xFFFF0000)
    return lo_w | hi_w


def _unpack_bf16_pair(w):
    lo = lax.bitcast_convert_type(w << 16, F32)
    hi = lax.bitcast_convert_type(w & jnp.uint32(0xFFFF0000), F32)
    return lo, hi


def _index_copy(idx_hbm, idx_smem, sems, tile):
    return pltpu.make_async_copy(idx_hbm.at[tile], idx_smem.at[tile % 2], sems.at[tile % 2])


def _combine_ln_kernel(dest_hbm, ys_hbm, gate4_ref, x_ref, mod_ref, g_ref, b_ref, o_ref,
                       idx_smem, buf_ref, rows2d_ref, idx_sems, row_sems, *, alpha):
    i = pl.program_id(0)
    n = pl.num_programs(0)

    def gather_rows(tile):
        slot = tile % 2

        def issue(r, carry):
            for k in range(TOP_K):
                row = idx_smem[slot, r * TOP_K + k]
                pltpu.make_async_copy(ys_hbm.at[row], buf_ref.at[slot, k, r],
                                      row_sems.at[slot]).start(priority=k % 2)
            return carry

        lax.fori_loop(0, TM, issue, 0, unroll=8)

    @pl.when(i == 0)
    def _():
        cp = _index_copy(dest_hbm, idx_smem, idx_sems, 0)
        cp.start()
        cp.wait()
        gather_rows(0)

        @pl.when(n > 1)
        def _():
            _index_copy(dest_hbm, idx_smem, idx_sems, 1).start()

    @pl.when(i + 1 < n)
    def _():
        _index_copy(dest_hbm, idx_smem, idx_sems, i + 1).wait()
        gather_rows(i + 1)

        @pl.when(i + 2 < n)
        def _():
            _index_copy(dest_hbm, idx_smem, idx_sems, i + 2).start()

    slot = i % 2
    for k in range(TOP_K):
        pltpu.make_async_copy(ys_hbm.at[pl.ds(0, TM)], buf_ref.at[slot, k], row_sems.at[slot]).wait()
    y_lo = y_hi = None
    for k in range(TOP_K):
        rows2d_ref[...] = buf_ref[slot, k].reshape(rows2d_ref.shape)
        lo, hi = _unpack_bf16_pair(rows2d_ref[...])
        gk = gate4_ref[:, k:k + 1]
        y_lo = lo * gk if y_lo is None else y_lo + lo * gk
        y_hi = hi * gk if y_hi is None else y_hi + hi * gk
    y = jnp.concatenate([y_lo, y_hi], axis=1)
    z = alpha * x_ref[...] + mod_ref[2:3, :] * y
    o_ref[...] = _layer_norm_rows(z, g_ref[...], b_ref[...])


def _combine_ln(dest_tiles, ys, gate4, x, mods, ls, ln_g, ln_b, *, alpha, geom, latent_only):
    D = x.shape[1]
    n_tiles, row_tile, mod_row = _token_tiles(geom, latent_only)
    return pl.pallas_call(
        functools.partial(_combine_ln_kernel, alpha=alpha),
        out_shape=jax.ShapeDtypeStruct((n_tiles * TM, D), F32),
        grid=(n_tiles,),
        in_specs=[
            pl.BlockSpec(memory_space=pl.ANY),
            pl.BlockSpec(memory_space=pl.ANY),
            pl.BlockSpec((TM, LANES), lambda i: (i, 0)),
            pl.BlockSpec((TM, D), lambda i: (row_tile(i), 0)),
            pl.BlockSpec((None, None, 3, D), lambda i: (ls, mod_row(i), 0, 0)),
            pl.BlockSpec((1, D), lambda i: (0, 0)),
            pl.BlockSpec((1, D), lambda i: (0, 0)),
        ],
        out_specs=pl.BlockSpec((TM, D), lambda i: (i, 0)),
        scratch_shapes=[
            pltpu.SMEM((2, TM * TOP_K), jnp.int32),
            pltpu.VMEM((2, TOP_K, TM, 1, D // 2), jnp.uint32),
            pltpu.VMEM((TM, D // 2), jnp.uint32),
            pltpu.SemaphoreType.DMA((2,)),
            pltpu.SemaphoreType.DMA((2,)),
        ],
        compiler_params=_cparams(("arbitrary",), VMEM_LIMIT),
        name="combine_ln",
    )(dest_tiles, ys, gate4, x, mods, ln_g.reshape(1, D), ln_b.reshape(1, D))


def _dispatch_kernel(pad_lo_ref, nused_ref, dest_hbm, x_ref, mod_ref, xs_hbm, idx_smem, a_ref, zero_ref,
                     idx_sems, row_sems, *, bm):
    i = pl.program_id(0)
    n = pl.num_programs(0)
    slot = i % 2

    def wait_rows(s):
        for _ in range(TOP_K):
            pltpu.make_async_copy(a_ref.at[s], xs_hbm.at[pl.ds(0, TM)], row_sems.at[s]).wait()

    @pl.when(i == 0)
    def _():
        _index_copy(dest_hbm, idx_smem, idx_sems, 0).start()
        zero_ref[...] = jnp.zeros_like(zero_ref)
        n_exp = pad_lo_ref.shape[0]
        n_blk = xs_hbm.shape[0] // bm

        def zero_block(row0):
            return pltpu.make_async_copy(zero_ref, xs_hbm.at[pl.ds(pl.multiple_of(row0, bm), bm)],
                                         row_sems.at[0])

        for e in range(n_exp):
            @pl.when(pad_lo_ref[e] >= 0)
            def _():
                zero_block(pad_lo_ref[e]).start()

        def fill_tail(b, carry):
            zero_block(b * bm).start()
            return carry

        lax.fori_loop(nused_ref[0], n_blk, fill_tail, 0)
        for e in range(n_exp):
            @pl.when(pad_lo_ref[e] >= 0)
            def _():
                zero_block(0).wait()

        def wait_tail(b, carry):
            zero_block(0).wait()
            return carry

        lax.fori_loop(nused_ref[0], n_blk, wait_tail, 0)

    @pl.when(i + 1 < n)
    def _():
        _index_copy(dest_hbm, idx_smem, idx_sems, i + 1).start()

    @pl.when(i >= 2)
    def _():
        wait_rows(slot)

    shift = mod_ref[0:1, :]
    scale = mod_ref[1:2, :]
    a = x_ref[...] * (1.0 + scale) + shift
    half = a.shape[1] // 2
    a_ref[slot] = _pack_bf16_pair(a[:, :half], a[:, half:]).reshape(a_ref.shape[1:])
    _index_copy(dest_hbm, idx_smem, idx_sems, i).wait()

    def issue(r, carry):
        for k in range(TOP_K):
            row = idx_smem[slot, r * TOP_K + k]
            pltpu.make_async_copy(a_ref.at[slot, r], xs_hbm.at[row], row_sems.at[slot]).start(priority=k % 2)
        return carry

    lax.fori_loop(0, TM, issue, 0, unroll=8)

    @pl.when(i == n - 1)
    def _():
        wait_rows(slot)

        @pl.when(n > 1)
        def _():
            wait_rows(1 - slot)


def _dispatch(pad_lo, n_used, dest_tiles, x, mods, ls, n_rows, *, geom, latent_only):
    D = x.shape[1]
    n_tiles, row_tile, mod_row = _token_tiles(geom, latent_only)
    bm = EXPERT_BM
    return pl.pallas_call(
        functools.partial(_dispatch_kernel, bm=bm),
        out_shape=jax.ShapeDtypeStruct((n_rows, 1, D // 2), jnp.uint32),
        grid_spec=pltpu.PrefetchScalarGridSpec(
            num_scalar_prefetch=2,
            grid=(n_tiles,),
            in_specs=[
                pl.BlockSpec(memory_space=pl.ANY),
                pl.BlockSpec((TM, D), lambda i, p, n: (row_tile(i), 0)),
                pl.BlockSpec((None, None, 3, D), lambda i, p, n: (ls, mod_row(i), 0, 0)),
            ],
            out_specs=pl.BlockSpec(memory_space=pl.ANY),
            scratch_shapes=[
                pltpu.SMEM((2, TM * TOP_K), jnp.int32),
                pltpu.VMEM((2, TM, 1, D // 2), jnp.uint32),
                pltpu.VMEM((bm, 1, D // 2), jnp.uint32),
                pltpu.SemaphoreType.DMA((2,)),
                pltpu.SemaphoreType.DMA((2,)),
            ],
        ),
        compiler_params=_cparams(("arbitrary",), VMEM_LIMIT),
        name="moe_dispatch",
    )(pad_lo, n_used, dest_tiles, x, mods)


def _sigmoid(x):
    return 0.5 * jnp.tanh(0.5 * x) + 0.5


def _softplus(x):
    return jnp.maximum(x, 0.0) + jnp.log(1.0 + jnp.exp(-jnp.abs(x)))


def _gelu_tanh(x):
    c = math.sqrt(2.0 / math.pi)
    return 0.5 * x * (1.0 + jnp.tanh(c * (x + 0.044715 * (x * x * x))))


def _group_scan(a, b, row, reverse):
    for s in (1, 2, 4):
        if reverse:
            keep = row < SUBLANES - s
            shift = SUBLANES - s
        else:
            keep = row >= s
            shift = s
        a_sh = jnp.where(keep, pltpu.roll(a, shift, axis=0), 1.0)
        b_sh = jnp.where(keep, pltpu.roll(b, shift, axis=0), 0.0)
        b = b + a * b_sh
        a = a * a_sh
    return a, b


def _lru_kernel(ug_ref, ur_ref, cw_ref, cb_ref, wg_ref, bg_ref, lam_ref, o_ref,
                a0_ref, b0_ref, a1_ref, b1_ref, *, ctx_len):
    T, W = ur_ref.shape
    CH = 256
    n_chunks = T // CH
    ctx_chunks = ctx_len // CH
    row_c = lax.broadcasted_iota(jnp.int32, (CH, W), 0)
    kexp = (-0.5 * LRU_C * math.log2(math.e)) * _softplus(-lam_ref[...])

    def coeff_chunk(c, carry):
        c0 = pl.multiple_of(c * CH, CH)
        cur = ur_ref[pl.ds(c0, CH), :]
        prev8 = ur_ref[pl.ds(pl.multiple_of(jnp.maximum(c0 - SUBLANES, 0), SUBLANES), SUBLANES), :]
        next8 = ur_ref[pl.ds(pl.multiple_of(jnp.minimum(c0 + CH, T - SUBLANES), SUBLANES), SUBLANES), :]
        seg_start = jnp.logical_or(c == 0, c == ctx_chunks)
        seg_end = jnp.logical_or(c == ctx_chunks - 1, c == n_chunks - 1)
        p_last = jnp.where(seg_start, 0.0, prev8[SUBLANES - 1:SUBLANES, :])
        n_first = jnp.where(seg_end, 0.0, next8[0:1, :])
        n_second = jnp.where(seg_end, 0.0, next8[1:2, :])
        u_m1 = jnp.where(row_c == 0, p_last, pltpu.roll(cur, 1, axis=0))
        u_p1 = jnp.where(row_c == CH - 1, n_first, pltpu.roll(cur, CH - 1, axis=0))
        u_p2 = jnp.where(row_c == CH - 2, n_first,
                         jnp.where(row_c == CH - 1, n_second, pltpu.roll(cur, CH - 2, axis=0)))
        r = cb_ref[...] + u_m1 * cw_ref[0:1, :]
        r = r + cur * cw_ref[1:2, :]
        r = r + u_p1 * cw_ref[2:3, :]
        r = r + u_p2 * cw_ref[3:4, :]
        gts = jnp.dot(r.astype(BF16), wg_ref[0].astype(BF16), preferred_element_type=F32) + bg_ref[0]
        half_r = 0.5 * r
        for d, (a_ref, b_ref) in enumerate(((a0_ref, b0_ref), (a1_ref, b1_ref))):
            t_rec = jnp.tanh(gts[:, (2 * d) * W:(2 * d + 1) * W])
            t_in = jnp.tanh(gts[:, (2 * d + 1) * W:(2 * d + 2) * W])
            kd = kexp[d:d + 1, :]
            a = jnp.exp2(kd * t_rec + kd)
            a_ref[pl.ds(c0, CH), :] = a
            b_ref[pl.ds(c0, CH), :] = jnp.sqrt(jnp.maximum(1.0 - a * a, 0.0)) * (half_r * t_in + half_r)
        return carry

    lax.fori_loop(0, n_chunks, coeff_chunk, 0)

    row8 = lax.broadcasted_iota(jnp.int32, (SUBLANES, W), 0)
    n_groups = T // SUBLANES
    ctx_groups = ctx_len // SUBLANES

    def scan_step(k, carry):
        hf, hb = carry
        gf = pl.multiple_of(k * SUBLANES, SUBLANES)
        a, b = _group_scan(a0_ref[pl.ds(gf, SUBLANES), :], b0_ref[pl.ds(gf, SUBLANES), :], row8, False)
        h = a * hf + b
        a0_ref[pl.ds(gf, SUBLANES), :] = h
        hf = jnp.broadcast_to(h[SUBLANES - 1:SUBLANES, :], h.shape)
        g = jnp.where(k < ctx_groups, ctx_groups - 1 - k, n_groups - 1 - (k - ctx_groups))
        gb = pl.multiple_of(g * SUBLANES, SUBLANES)
        a, b = _group_scan(a1_ref[pl.ds(gb, SUBLANES), :], b1_ref[pl.ds(gb, SUBLANES), :], row8, True)
        h = a * hb + b
        a1_ref[pl.ds(gb, SUBLANES), :] = h
        hb = jnp.broadcast_to(h[0:1, :], h.shape)
        return hf, hb

    zero = jnp.zeros((SUBLANES, W), F32)
    lax.fori_loop(0, n_groups, scan_step, (zero, zero), unroll=4)

    def out_chunk(c, carry):
        c0 = pl.multiple_of(c * CH, CH)
        y = a0_ref[pl.ds(c0, CH), :] + a1_ref[pl.ds(c0, CH), :]
        o_ref[pl.ds(c0, CH), :] = (y * _gelu_tanh(ug_ref[pl.ds(c0, CH), :])).astype(o_ref.dtype)
        return carry

    lax.fori_loop(0, n_chunks, out_chunk, 0)


def _lru_mix(u, conv_w, conv_b, gate_w, gate_b, lam, *, nbatch, ctx_len):
    M, W2 = u.shape
    W = W2 // 2
    T = M // nbatch
    nblk = W // LRU_BLOCK_W
    bw = LRU_BLOCK_W
    wg = 0.5 * jnp.transpose(gate_w, (2, 3, 0, 1, 4)).reshape(nblk, bw, 4 * bw)
    bg = 0.5 * jnp.transpose(gate_b.reshape(2, 2, nblk, bw), (2, 0, 1, 3)).reshape(nblk, 1, 4 * bw)
    return pl.pallas_call(
        functools.partial(_lru_kernel, ctx_len=ctx_len),
        out_shape=jax.ShapeDtypeStruct((M, W), BF16),
        grid=(nbatch, nblk),
        in_specs=[
            pl.BlockSpec((T, bw), lambda b, n: (b, n)),
            pl.BlockSpec((T, bw), lambda b, n: (b, nblk + n)),
            pl.BlockSpec((CONV_W, bw), lambda b, n: (0, n)),
            pl.BlockSpec((1, bw), lambda b, n: (0, n)),
            pl.BlockSpec((1, bw, 4 * bw), lambda b, n: (n, 0, 0)),
            pl.BlockSpec((1, 1, 4 * bw), lambda b, n: (n, 0, 0)),
            pl.BlockSpec((2, bw), lambda b, n: (0, n)),
        ],
        out_specs=pl.BlockSpec((T, bw), lambda b, n: (b, n)),
        scratch_shapes=[pltpu.VMEM((T, bw), F32) for _ in range(4)],
        compiler_params=_cparams(("parallel", "parallel"), VMEM_LIMIT),
        name="lru_mix",
    )(u, u, conv_w, conv_b.reshape(1, W), wg, bg, lam)


def _attn_kernel(sink_ref, q_ref, k_ref, v_ref, o_ref, k2_ref, v2_ref, p_ref, *, ctx_len, n_kv):
    QB = q_ref.shape[0]
    T = k_ref.shape[0]
    n_lat = (T - ctx_len) // QB
    ctx_blocks = ctx_len // QB
    band = 3 * QB
    pairs = GROUP // 2
    qb = pl.program_id(1)
    j = qb - ctx_blocks
    start = jnp.clip(j - 1, 0, n_lat - 3)
    off = pl.multiple_of(ctx_len + start * QB, QB)
    lo_half = lax.broadcasted_iota(jnp.int32, (1, LANES), 1) < HEAD_DIM
    nt = (((1,), (1,)), ((), ()))

    def attend(with_band):
        nk = ctx_len + (band if with_band else 0)
        if with_band:
            col = lax.broadcasted_iota(jnp.int32, (QB, nk), 1)
            qpos = j * QB + lax.broadcasted_iota(jnp.int32, (QB, nk), 0)
            kpos = start * QB + (col - ctx_len)
            valid = jnp.logical_or(col < ctx_len, jnp.abs(qpos - kpos) <= WINDOW)
        for h in range(n_kv):
            slab = slice((h // 2) * LANES, (h // 2 + 1) * LANES)

            def two_blocks(x):
                swapped = jnp.concatenate([x[:, HEAD_DIM:], x[:, :HEAD_DIM]], axis=1)
                own, other = (x, swapped) if h % 2 == 0 else (swapped, x)
                zero = jnp.zeros_like(x)
                return jnp.where(lo_half, own, zero), jnp.where(lo_half, zero, other)

            for dst, src in ((k2_ref, k_ref), (v2_ref, v_ref)):
                top_c, bot_c = two_blocks(src[0:ctx_len, slab])
                dst[0:ctx_len, :] = top_c
                dst[nk:nk + ctx_len, :] = bot_c
                if with_band:
                    top_b, bot_b = two_blocks(src[pl.ds(off, band), slab])
                    dst[ctx_len:nk, :] = top_b
                    dst[nk + ctx_len:2 * nk, :] = bot_b
            c_q = h * GROUP * HEAD_DIM
            qs = jnp.concatenate([q_ref[:, c_q + p * LANES:c_q + (p + 1) * LANES] for p in range(pairs)], axis=0)
            s = lax.dot_general(qs, k2_ref[0:2 * nk, :], nt, preferred_element_type=F32)
            invs = []
            for p in range(pairs):
                rows = slice(p * QB, (p + 1) * QB)
                inv_pair = []
                for half in range(2):
                    c0 = half * nk
                    sink = sink_ref[h * GROUP + 2 * p + half] * LOG2_E
                    s_h = s[rows, c0:c0 + nk]
                    if with_band:
                        s_h = jnp.where(valid, s_h, NEG_INF)
                    m = jnp.maximum(jnp.max(s_h, axis=-1, keepdims=True), sink)
                    p_h = jnp.exp2(s_h - m)
                    denom = jnp.sum(p_h, axis=-1, keepdims=True) + jnp.exp2(sink - m)
                    p_ref[rows, c0:c0 + nk] = p_h.astype(BF16)
                    inv_pair.append(1.0 / denom)
                invs.append(inv_pair)
            o = jnp.dot(p_ref[:, 0:2 * nk], v2_ref[0:2 * nk, :], preferred_element_type=F32)
            for p in range(pairs):
                inv = jnp.where(lo_half, invs[p][0], invs[p][1])
                o_ref[:, c_q + p * LANES:c_q + (p + 1) * LANES] = (o[p * QB:(p + 1) * QB] * inv).astype(o_ref.dtype)

    @pl.when(qb >= ctx_blocks)
    def _():
        attend(True)

    @pl.when(qb < ctx_blocks)
    def _():
        attend(False)


def _attention(qkv, sinks, *, nbatch, ctx_len, q_dim, kv_dim):
    M = qkv.shape[0]
    T = M // nbatch
    nqb = T // Q_BLOCK
    kcol = q_dim // kv_dim
    return pl.pallas_call(
        functools.partial(_attn_kernel, ctx_len=ctx_len, n_kv=kv_dim // HEAD_DIM),
        out_shape=jax.ShapeDtypeStruct((M, q_dim), BF16),
        grid_spec=pltpu.PrefetchScalarGridSpec(
            num_scalar_prefetch=1,
            grid=(nbatch, nqb),
            in_specs=[
                pl.BlockSpec((Q_BLOCK, q_dim), lambda b, q, s: (b * nqb + q, 0)),
                pl.BlockSpec((T, kv_dim), lambda b, q, s: (b, kcol)),
                pl.BlockSpec((T, kv_dim), lambda b, q, s: (b, kcol + 1)),
            ],
            out_specs=pl.BlockSpec((Q_BLOCK, q_dim), lambda b, q, s: (b * nqb + q, 0)),
            scratch_shapes=[
                pltpu.VMEM((2 * (ctx_len + 3 * Q_BLOCK), LANES), BF16),
                pltpu.VMEM((2 * (ctx_len + 3 * Q_BLOCK), LANES), BF16),
                pltpu.VMEM((GROUP // 2 * Q_BLOCK, 2 * (ctx_len + 3 * Q_BLOCK)), BF16),
            ],
        ),
        compiler_params=_cparams(("parallel", "parallel"), VMEM_LIMIT),
        name="swa_attention",
    )(sinks, qkv, qkv, qkv)


def _split_bf16(x):
    hi = x.astype(BF16)
    lo = (x - hi.astype(F32)).astype(BF16)
    return hi, lo


def _router_kernel(x_ref, mod_ref, wr_ref, br_ref, e_ref, g_ref, r_ref, cnt_ref, carry_ref):
    i = pl.program_id(0)

    @pl.when(i == 0)
    def _():
        carry_ref[...] = jnp.zeros_like(carry_ref)

    shift = mod_ref[0:1, :]
    scale = mod_ref[1:2, :]
    a = x_ref[...] * (1.0 + scale) + shift
    a_hi, a_lo = _split_bf16(a)
    ne = br_ref.shape[1]
    hi_terms = jnp.dot(a_hi, wr_ref[...], preferred_element_type=F32)
    lo_term = jnp.dot(a_lo, wr_ref[:, 0:ne], preferred_element_type=F32)
    logits = (hi_terms[:, 0:ne] + (lo_term + hi_terms[:, ne:2 * ne])) + br_ref[...]
    tm = logits.shape[0]
    lane_e = lax.broadcasted_iota(jnp.int32, (tm, ne), 1)
    lane_o = lax.broadcasted_iota(jnp.int32, (tm, LANES), 1)
    vals, idxs, hots = [], [], []
    cur = logits
    for k in range(TOP_K):
        m = jnp.max(cur, axis=-1, keepdims=True)
        idx = jnp.min(jnp.where(cur == m, lane_e, ne), axis=-1, keepdims=True)
        hot = lane_e == idx
        vals.append(m)
        idxs.append(idx)
        hots.append(hot)
        cur = jnp.where(hot, -jnp.inf, cur)
    exps = [jnp.exp(v - vals[0]) for v in vals]
    inv = 1.0 / (exps[0] + exps[1] + exps[2] + exps[3])
    msum = jnp.zeros((tm, ne), F32)
    for hot in hots:
        msum = msum + hot.astype(F32)
    rr = lax.broadcasted_iota(jnp.int32, (tm, tm), 0)
    cc = lax.broadcasted_iota(jnp.int32, (tm, tm), 1)
    tri = (cc < rr).astype(BF16)
    before = jnp.dot(tri, msum.astype(BF16), preferred_element_type=F32) + carry_ref[...]
    e_out = jnp.zeros((tm, LANES), jnp.int32)
    g_out = jnp.zeros((tm, LANES), F32)
    r_out = jnp.zeros((tm, LANES), jnp.int32)
    for k in range(TOP_K):
        rank = jnp.sum(jnp.where(hots[k], before, 0.0), axis=-1, keepdims=True).astype(jnp.int32)
        e_out = jnp.where(lane_o == k, idxs[k], e_out)
        g_out = jnp.where(lane_o == k, exps[k] * inv, g_out)
        r_out = jnp.where(lane_o == k, rank, r_out)
    e_ref[...] = e_out
    g_ref[...] = g_out
    r_ref[...] = r_out
    carry_ref[...] = carry_ref[...] + jnp.sum(msum, axis=0, keepdims=True)
    cnt_ref[...] = carry_ref[...].astype(jnp.int32)


def _router(x, mods, ls, w_r, b_r, *, geom, latent_only):
    D = x.shape[1]
    ne = w_r.shape[1]
    n_tiles, row_tile, mod_row = _token_tiles(geom, latent_only)
    M = n_tiles * TM
    return pl.pallas_call(
        _router_kernel,
        out_shape=(
            jax.ShapeDtypeStruct((M, LANES), jnp.int32),
            jax.ShapeDtypeStruct((M, LANES), F32),
            jax.ShapeDtypeStruct((M, LANES), jnp.int32),
            jax.ShapeDtypeStruct((1, ne), jnp.int32),
        ),
        grid=(n_tiles,),
        in_specs=[
            pl.BlockSpec((TM, D), lambda i: (row_tile(i), 0)),
            pl.BlockSpec((None, None, 3, D), lambda i: (ls, mod_row(i), 0, 0)),
            pl.BlockSpec((D, 2 * ne), lambda i: (0, 0)),
            pl.BlockSpec((1, ne), lambda i: (0, 0)),
        ],
        out_specs=(
            pl.BlockSpec((TM, LANES), lambda i: (i, 0)),
            pl.BlockSpec((TM, LANES), lambda i: (i, 0)),
            pl.BlockSpec((TM, LANES), lambda i: (i, 0)),
            pl.BlockSpec((1, ne), lambda i: (0, 0)),
        ),
        scratch_shapes=[pltpu.VMEM((1, ne), F32)],
        compiler_params=_cparams(("arbitrary",), VMEM_LIMIT),
        name="moe_router",
    )(x, mods, jnp.concatenate(_split_bf16(w_r), axis=1), b_r.reshape(1, ne))


def _expert_kernel(blk_e_ref, slot_ref, first_ref, next_e_ref, nused_ref, xs_ref, bgu_ref, bdn_ref,
                   wgu_hbm, wdn_hbm, o_ref, wgu_buf, wdn_buf, x2d_ref, sem_gu, sem_dn, *, layer):
    b = pl.program_id(0)
    de, d = wdn_buf.shape[1:]
    half = d // 2

    def fetch(expert, slot):
        cps = []
        for hbm, buf, sem, chunks in ((wgu_hbm, wgu_buf, sem_gu, WEIGHT_DMA_CHUNKS),
                                      (wdn_hbm, wdn_buf, sem_dn, WEIGHT_DMA_CHUNKS // 2)):
            rows = buf.shape[1] // chunks
            for c in range(chunks):
                rs = pl.ds(c * rows, rows)
                cps.append(pltpu.make_async_copy(hbm.at[layer, expert, rs], buf.at[slot, rs], sem.at[slot]))
        return cps

    @pl.when(b < nused_ref[0])
    def _():
        slot = slot_ref[b]

        @pl.when(b == 0)
        def _():
            for c, cp in enumerate(fetch(blk_e_ref[0], 0)):
                cp.start(priority=c % 2)

        @pl.when(first_ref[b] == 1)
        def _():
            @pl.when(next_e_ref[b] >= 0)
            def _():
                for c, cp in enumerate(fetch(next_e_ref[b], 1 - slot)):
                    cp.start(priority=c % 2)
            for cp in fetch(blk_e_ref[b], slot):
                cp.wait()

        x2d_ref[...] = xs_ref[...].reshape(x2d_ref.shape)
        x_lo, x_hi = (v.astype(BF16) for v in _unpack_bf16_pair(x2d_ref[...]))
        hs = []
        for c in range(2 * de // 256):
            sl = slice(c * 256, (c + 1) * 256)
            acc = jnp.dot(x_lo, wgu_buf[slot, 0:half, sl].astype(BF16), preferred_element_type=F32)
            acc = acc + jnp.dot(x_hi, wgu_buf[slot, half:d, sl].astype(BF16), preferred_element_type=F32)
            hs.append(acc + bgu_ref[:, sl])
        h = jnp.concatenate(hs, axis=1)
        g = jnp.minimum(h[:, :de], SWIGLU_LIMIT)
        u = jnp.clip(h[:, de:], -SWIGLU_LIMIT, SWIGLU_LIMIT)
        act = ((u + 1.0) * (g * _sigmoid(SWIGLU_ALPHA * g))).astype(BF16)
        ys = []
        for c in range(d // 512):
            sl = slice(c * 512, (c + 1) * 512)
            ys.append(jnp.dot(act, wdn_buf[slot, :, sl].astype(BF16), preferred_element_type=F32) + bdn_ref[:, sl])
        nh = len(ys) // 2
        for c in range(nh):
            packed = _pack_bf16_pair(ys[c], ys[c + nh])
            o_ref[:, :, c * 512:(c + 1) * 512] = packed.reshape(packed.shape[0], 1, 512)

    @pl.when(b >= nused_ref[0])
    def _():
        o_ref[...] = jnp.zeros_like(o_ref)


def _expert_ffn(xs, blk_tabs, n_used, layer, w_gu, b_gu, w_dn, b_dn):
    P = xs.shape[0]
    _, ne, D, de2 = w_gu.shape
    de = de2 // 2
    n_blk = P // EXPERT_BM
    blk_e, slot, first, next_e = blk_tabs
    return pl.pallas_call(
        functools.partial(_expert_kernel, layer=layer),
        out_shape=jax.ShapeDtypeStruct((P, 1, D // 2), jnp.uint32),
        grid_spec=pltpu.PrefetchScalarGridSpec(
            num_scalar_prefetch=5,
            grid=(n_blk,),
            in_specs=[
                pl.BlockSpec((EXPERT_BM, 1, D // 2), lambda b, e, s, f, x, n: (jnp.minimum(b, n[0] - 1), 0, 0)),
                pl.BlockSpec((None, None, 1, de2), lambda b, e, s, f, x, n: (layer, e[b], 0, 0)),
                pl.BlockSpec((None, None, 1, D), lambda b, e, s, f, x, n: (layer, e[b], 0, 0)),
                pl.BlockSpec(memory_space=pl.ANY),
                pl.BlockSpec(memory_space=pl.ANY),
            ],
            out_specs=pl.BlockSpec((EXPERT_BM, 1, D // 2), lambda b, e, s, f, x, n: (b, 0, 0)),
            scratch_shapes=[
                pltpu.VMEM((2, D, de2), F32),
                pltpu.VMEM((2, de, D), F32),
                pltpu.VMEM((EXPERT_BM, D // 2), jnp.uint32),
                pltpu.SemaphoreType.DMA((2,)),
                pltpu.SemaphoreType.DMA((2,)),
            ],
        ),
        compiler_params=_cparams(("arbitrary",), VMEM_LIMIT),
        name="expert_ffn",
    )(blk_e, slot, first, next_e, n_used, xs, b_gu.reshape(b_gu.shape[0], ne, 1, de2),
      b_dn.reshape(b_dn.shape[0], ne, 1, D), w_gu, w_dn)


def _moe_layer(x, mods, ls, layer, w_r, b_r, w_gu, b_gu, w_dn, b_dn, ln_g, ln_b, *, alpha, geom, latent_only):
    ne = w_r.shape[1]
    bm = EXPERT_BM
    e128, g128, r128, counts = _router(x, mods, ls, w_r, b_r, geom=geom, latent_only=latent_only)
    M = e128.shape[0]
    top_e = e128[:, :TOP_K]
    rank = r128[:, :TOP_K]
    counts = counts[0]
    padded = ((counts + bm - 1) // bm) * bm
    ends = jnp.cumsum(padded)
    pstart = ends - padded
    experts = jnp.arange(ne, dtype=jnp.int32)
    dest = jnp.sum(jnp.where(top_e[..., None] == experts, pstart, 0), axis=-1) + rank
    dest_tiles = dest.astype(jnp.int32).reshape(M // TM, TM * TOP_K)
    A = M * TOP_K
    P = (-(-A // bm)) * bm + ne * bm
    n_blk = P // bm
    n_used = (ends[-1] // bm).astype(jnp.int32).reshape(1)
    blk = jnp.arange(n_blk, dtype=jnp.int32)
    blk_row = jnp.minimum(blk, n_used[0] - 1) * bm
    blk_e = jnp.minimum(jnp.sum(ends[None, :] <= blk_row[:, None], axis=1), ne - 1).astype(jnp.int32)
    first = jnp.logical_and(blk < n_used[0],
                            jnp.concatenate([jnp.ones((1,), bool), blk_e[1:] != blk_e[:-1]])).astype(jnp.int32)
    slot = ((jnp.cumsum(first) - 1) % 2).astype(jnp.int32)
    later = jnp.logical_and(experts[None, :] > experts[:, None], (padded > 0)[None, :])
    next_nonempty = jnp.min(jnp.where(later, experts[None, :], ne), axis=1)
    next_nonempty = jnp.where(next_nonempty == ne, -1, next_nonempty)
    next_e = jnp.sum(jnp.where(blk_e[:, None] == experts[None, :], next_nonempty[None, :], 0), axis=1).astype(jnp.int32)
    pad_lo = jnp.where(padded > 0, ends - bm, -1).astype(jnp.int32)
    xs = _dispatch(pad_lo, n_used, dest_tiles, x, mods, ls, P, geom=geom, latent_only=latent_only)
    ys = _expert_ffn(xs, (blk_e, slot, first, next_e), n_used, layer, w_gu, b_gu, w_dn, b_dn)
    return _combine_ln(dest_tiles, ys, g128, x, mods, ls, ln_g, ln_b, alpha=alpha, geom=geom,
                       latent_only=latent_only)


def _rope_tables(seq, ctx_len):
    rows = seq // GRID_W
    row_pos = jnp.repeat(jnp.arange(rows), GRID_W).astype(F32)
    col_pos = jnp.tile(jnp.arange(GRID_W), rows).astype(F32)
    n = HEAD_DIM // 2
    freqs = ROPE_BASE ** (-jnp.arange(0, n, 2, dtype=F32) / n)
    ang_r = row_pos[:, None] * freqs[None, :]
    ang_c = col_pos[:, None] * freqs[None, :]
    cos = jnp.concatenate([jnp.cos(ang_r), jnp.cos(ang_r), jnp.cos(ang_c), jnp.cos(ang_c)], axis=1)
    sin = jnp.concatenate([-jnp.sin(ang_r), jnp.sin(ang_r), -jnp.sin(ang_c), jnp.sin(ang_c)], axis=1)
    cos = jnp.concatenate([jnp.ones((ctx_len, HEAD_DIM), F32), cos], axis=0)
    sin = jnp.concatenate([jnp.zeros((ctx_len, HEAD_DIM), F32), sin], axis=0)
    cos = jnp.tile(cos, (1, LANES // HEAD_DIM))
    sin = jnp.tile(sin, (1, LANES // HEAD_DIM))
    scale = HEAD_DIM ** -0.5 * LOG2_E
    return jnp.stack([cos * scale, sin * scale, cos, sin], axis=0)


def kernel(x, c, ctx, c_ctx, ada_w, ada_b, ln_g, ln_b, lru_w_in, lru_conv_w, lru_conv_b, lru_gate_w, lru_gate_b, lru_lambda, lru_w_out, attn_w_qkv, attn_sinks, attn_w_o, router_w, router_b, moe_w_gu, moe_b_gu, moe_w_down, moe_b_down):
    B, S, D = x.shape
    C = ctx.shape[1]
    depth = ada_w.shape[0]
    T = C + S
    assert C % TM == 0 and S % TM == 0 and C % Q_BLOCK == 0
    geom = (T // TM, C // TM, B)
    alpha = (2.0 * depth) ** 0.25
    kv_dim = (attn_w_qkv.shape[2] - D) // 2

    act = jax.nn.silu(jnp.concatenate([c, c_ctx[None, :]], axis=0))
    mods = _ada_params(act, ada_w.reshape(depth * 2, D, 3 * D), ada_b.reshape(depth * 2, 3 * D))
    mods = mods.reshape(depth * 2, SUBLANES, 3, D)
    tab = _rope_tables(S, C)

    z = jnp.concatenate([ctx, x], axis=1).reshape(B * T, D)
    for i in range(depth):
        j = i // 2
        if i % 2 == 0:
            u = _mod_matmul(z, mods, 2 * i, lru_w_in[j].astype(BF16), tn=D, out_dtype=F32, geom=geom)
            y = _lru_mix(u, lru_conv_w[j], lru_conv_b[j], lru_gate_w[j], lru_gate_b[j], lru_lambda[j],
                         nbatch=B, ctx_len=C)
            w_out = lru_w_out[j]
        else:
            qkv = _mod_matmul(z, mods, 2 * i, attn_w_qkv[j].astype(BF16), tn=D + 2 * kv_dim, out_dtype=BF16,
                              geom=geom, rope=(tab, (D, kv_dim)))
            y = _attention(qkv, attn_sinks[j], nbatch=B, ctx_len=C, q_dim=D, kv_dim=kv_dim)
            w_out = attn_w_o[j]
        z = _proj_ln(y, w_out.astype(BF16), z, mods, 2 * i, ln_g[i, 0], ln_b[i, 0], alpha=alpha, geom=geom)
        z = _moe_layer(z, mods, 2 * i + 1, i, router_w[i], router_b[i], moe_w_gu, moe_b_gu,
                       moe_w_down, moe_b_down, ln_g[i, 1], ln_b[i, 1], alpha=alpha, geom=geom,
                       latent_only=(i == depth - 1))
    return z.reshape(B, S, D)
```

```python
import functools
import math

import jax
import jax.numpy as jnp
from jax import lax
from jax.experimental import pallas as pl
from jax.experimental.pallas import tpu as pltpu

GRID_W = 64
LN_EPS = 1e-5
LRU_BLOCK_W = 256
CONV_W = 4
LRU_C = 8.0
HEAD_DIM = 64
GROUP = 8
WINDOW = 128
Q_BLOCK = 128
ROPE_BASE = 10000.0
NEG_INF = -1e30
TOP_K = 4
SWIGLU_LIMIT = 7.0
SWIGLU_ALPHA = 1.702
LOG2_E = math.log2(math.e)

LANES = 128
SUBLANES = 8
TM = 256
EXPERT_BM = 256
WEIGHT_DMA_CHUNKS = 4
VMEM_LIMIT = 56 * 1024 * 1024

BF16 = jnp.bfloat16
F32 = jnp.float32


def _cparams(sem, vmem=None):
    return pltpu.CompilerParams(dimension_semantics=sem, vmem_limit_bytes=vmem)


def _ada_kernel(a_ref, w_ref, b_ref, o_ref):
    acc = jnp.dot(a_ref[...], w_ref[0].astype(BF16), preferred_element_type=F32)
    o_ref[0] = acc[0:SUBLANES] + acc[SUBLANES:2 * SUBLANES] + b_ref[0]


def _ada_params(act, w, bias):
    R, D = act.shape
    L, _, N = w.shape
    tn = 1024
    a8 = jnp.zeros((SUBLANES, D), F32).at[:R].set(act)
    hi = a8.astype(BF16)
    lo = (a8 - hi.astype(F32)).astype(BF16)
    a16 = jnp.concatenate([hi, lo], axis=0)
    return pl.pallas_call(
        _ada_kernel,
        out_shape=jax.ShapeDtypeStruct((L, SUBLANES, N), F32),
        grid=(L, N // tn),
        in_specs=[
            pl.BlockSpec((2 * SUBLANES, D), lambda l, n: (0, 0)),
            pl.BlockSpec((1, D, tn), lambda l, n: (l, 0, n)),
            pl.BlockSpec((1, 1, tn), lambda l, n: (l, 0, n)),
        ],
        out_specs=pl.BlockSpec((1, SUBLANES, tn), lambda l, n: (l, 0, n)),
        compiler_params=_cparams(("parallel", "parallel"), VMEM_LIMIT),
        name="ada_params",
    )(a16, w, bias.reshape(L, 1, N))


def _mod_row(i, tiles_per_batch, ctx_tiles, nbatch):
    return jnp.where(i % tiles_per_batch < ctx_tiles, nbatch, i // tiles_per_batch)


def _token_tiles(geom, latent_only):
    tpb, ctx_tiles, nbatch = geom
    if latent_only:
        lpb = tpb - ctx_tiles
        return lpb * nbatch, (lambda i: (i // lpb) * tpb + ctx_tiles + i % lpb), (lambda i: i // lpb)
    return tpb * nbatch, (lambda i: i), (lambda i: _mod_row(i, tpb, ctx_tiles, nbatch))


def _rope_chunk(y, cos, sin_signed, first_half):
    rot = jnp.where(first_half, pltpu.roll(y, LANES - 16, axis=1), pltpu.roll(y, 16, axis=1))
    return y * cos + rot * sin_signed


def _modmm_kernel(x_ref, mod_ref, w_ref, *rest, rope_cols):
    if rope_cols is None:
        (o_ref,) = rest
    else:
        tab_ref, o_ref = rest
    shift = mod_ref[0:1, :]
    scale = mod_ref[1:2, :]
    a = (x_ref[...] * (1.0 + scale) + shift).astype(BF16)
    tn = o_ref.shape[1]
    ch = 512 if tn % 512 == 0 else 256
    for c in range(tn // ch):
        acc = jnp.dot(a, w_ref[:, c * ch:(c + 1) * ch].astype(BF16), preferred_element_type=F32)
        if rope_cols is None:
            o_ref[:, c * ch:(c + 1) * ch] = acc.astype(o_ref.dtype)
        else:
            q_cols, k_cols = rope_cols
            lane = lax.broadcasted_iota(jnp.int32, (acc.shape[0], LANES), 1)
            first_half = (lane % 32) < 16
            for s in range(ch // LANES):
                col = c * ch + s * LANES
                y = acc[:, s * LANES:(s + 1) * LANES]
                if col < q_cols:
                    y = _rope_chunk(y, tab_ref[0], tab_ref[1], first_half)
                elif col < q_cols + k_cols:
                    y = _rope_chunk(y, tab_ref[2], tab_ref[3], first_half)
                o_ref[:, col:col + LANES] = y.astype(o_ref.dtype)


def _mod_matmul(x, mods, ls, w_stack, layer, *, tn, out_dtype, geom, rope=None):
    M, K = x.shape
    N = w_stack.shape[2]
    tpb, ctx_tiles, nbatch = geom
    in_specs = [
        pl.BlockSpec((TM, K), lambda j, i: (i, 0)),
        pl.BlockSpec((None, None, 3, K), lambda j, i: (ls, _mod_row(i, tpb, ctx_tiles, nbatch), 0, 0)),
        pl.BlockSpec((None, K, tn), lambda j, i: (layer, 0, j)),
    ]
    args = [x, mods, w_stack]
    rope_cols = None
    if rope is not None:
        tab, rope_cols = rope
        in_specs.append(pl.BlockSpec((4, TM, LANES), lambda j, i: (0, i % tpb, 0)))
        args.append(tab)
    return pl.pallas_call(
        functools.partial(_modmm_kernel, rope_cols=rope_cols),
        out_shape=jax.ShapeDtypeStruct((M, N), out_dtype),
        grid=(N // tn, M // TM),
        in_specs=in_specs,
        out_specs=pl.BlockSpec((TM, tn), lambda j, i: (i, j)),
        compiler_params=_cparams(("parallel", "parallel"), VMEM_LIMIT),
        name="mod_matmul",
    )(*args)


def _layer_norm_rows(z, g, b):
    mu = jnp.mean(z, axis=-1, keepdims=True)
    zc = z - mu
    var = jnp.mean(zc * zc, axis=-1, keepdims=True)
    return zc * lax.rsqrt(var + LN_EPS) * g + b


def _proj_ln_kernel(y_ref, w_ref, x_ref, mod_ref, g_ref, b_ref, o_ref, z_ref, *, alpha):
    gate = mod_ref[2:3, :]
    y = y_ref[...]
    n = o_ref.shape[1]
    ch = 512
    for c in range(n // ch):
        sl = slice(c * ch, (c + 1) * ch)
        acc = jnp.dot(y, w_ref[:, sl].astype(BF16), preferred_element_type=F32)
        z_ref[:, sl] = alpha * x_ref[:, sl] + gate[:, sl] * acc
    o_ref[...] = _layer_norm_rows(z_ref[...], g_ref[...], b_ref[...])


def _proj_ln(y, w_stack, j, x, mods, ls, ln_g, ln_b, *, alpha, geom):
    M, K = y.shape
    D = w_stack.shape[2]
    tpb, ctx_tiles, nbatch = geom
    return pl.pallas_call(
        functools.partial(_proj_ln_kernel, alpha=alpha),
        out_shape=jax.ShapeDtypeStruct((M, D), F32),
        grid=(M // TM,),
        in_specs=[
            pl.BlockSpec((TM, K), lambda i: (i, 0)),
            pl.BlockSpec((None, K, D), lambda i: (j, 0, 0)),
            pl.BlockSpec((TM, D), lambda i: (i, 0)),
            pl.BlockSpec((None, None, 3, D), lambda i: (ls, _mod_row(i, tpb, ctx_tiles, nbatch), 0, 0)),
            pl.BlockSpec((1, D), lambda i: (0, 0)),
            pl.BlockSpec((1, D), lambda i: (0, 0)),
        ],
        out_specs=pl.BlockSpec((TM, D), lambda i: (i, 0)),
        scratch_shapes=[pltpu.VMEM((TM, D), F32)],
        compiler_params=_cparams(("parallel",), VMEM_LIMIT),
        name="proj_ln",
    )(y, w_stack, x, mods, ln_g.reshape(1, D), ln_b.reshape(1, D))


def _pack_bf16_pair(lo, hi):
    lo_w = lax.bitcast_convert_type(lo.astype(BF16).astype(F32), jnp.uint32) >> 16
    hi_w = lax.bitcast_convert_type(hi.astype(BF16).astype(F32), jnp.uint32) & jnp.uint32(0xFFFF0000)
    return lo_w | hi_w


def _unpack_bf16_pair(w):
    lo = lax.bitcast_convert_type(w << 16, F32)
    hi = lax.bitcast_convert_type(w & jnp.uint32(0xFFFF0000), F32)
    return lo, hi


def _index_copy(idx_hbm, idx_smem, sems, tile):
    return pltpu.make_async_copy(idx_hbm.at[tile], idx_smem.at[tile % 2], sems.at[tile % 2])


def _combine_ln_kernel(dest_hbm, ys_hbm, gate4_ref, x_ref, mod_ref, g_ref, b_ref, o_ref,
                       idx_smem, buf_ref, rows2d_ref, idx_sems, row_sems, *, alpha):
    i = pl.program_id(0)
    n = pl.num_programs(0)

    def gather_rows(tile):
        slot = tile % 2

        def issue(r, carry):
            for k in range(TOP_K):
                row = idx_smem[slot, r * TOP_K + k]
                pltpu.make_async_copy(ys_hbm.at[row], buf_ref.at[slot, k, r],
                                      row_sems.at[slot]).start(priority=k % 2)
            return carry

        lax.fori_loop(0, TM, issue, 0, unroll=8)

    @pl.when(i == 0)
    def _():
        cp = _index_copy(dest_hbm, idx_smem, idx_sems, 0)
        cp.start()
        cp.wait()
        gather_rows(0)

        @pl.when(n > 1)
        def _():
            _index_copy(dest_hbm, idx_smem, idx_sems, 1).start()

    @pl.when(i + 1 < n)
    def _():
        _index_copy(dest_hbm, idx_smem, idx_sems, i + 1).wait()
        gather_rows(i + 1)

        @pl.when(i + 2 < n)
        def _():
            _index_copy(dest_hbm, idx_smem, idx_sems, i + 2).start()

    slot = i % 2
    for k in range(TOP_K):
        pltpu.make_async_copy(ys_hbm.at[pl.ds(0, TM)], buf_ref.at[slot, k], row_sems.at[slot]).wait()
    y_lo = y_hi = None
    for k in range(TOP_K):
        rows2d_ref[...] = buf_ref[slot, k].reshape(rows2d_ref.shape)
        lo, hi = _unpack_bf16_pair(rows2d_ref[...])
        gk = gate4_ref[:, k:k + 1]
        y_lo = lo * gk if y_lo is None else y_lo + lo * gk
        y_hi = hi * gk if y_hi is None else y_hi + hi * gk
    y = jnp.concatenate([y_lo, y_hi], axis=1)
    z = alpha * x_ref[...] + mod_ref[2:3, :] * y
    o_ref[...] = _layer_norm_rows(z, g_ref[...], b_ref[...])


def _combine_ln(dest_tiles, ys, gate4, x, mods, ls, ln_g, ln_b, *, alpha, geom, latent_only):
    D = x.shape[1]
    n_tiles, row_tile, mod_row = _token_tiles(geom, latent_only)
    return pl.pallas_call(
        functools.partial(_combine_ln_kernel, alpha=alpha),
        out_shape=jax.ShapeDtypeStruct((n_tiles * TM, D), F32),
        grid=(n_tiles,),
        in_specs=[
            pl.BlockSpec(memory_space=pl.ANY),
            pl.BlockSpec(memory_space=pl.ANY),
            pl.BlockSpec((TM, LANES), lambda i: (i, 0)),
            pl.BlockSpec((TM, D), lambda i: (row_tile(i), 0)),
            pl.BlockSpec((None, None, 3, D), lambda i: (ls, mod_row(i), 0, 0)),
            pl.BlockSpec((1, D), lambda i: (0, 0)),
            pl.BlockSpec((1, D), lambda i: (0, 0)),
        ],
        out_specs=pl.BlockSpec((TM, D), lambda i: (i, 0)),
        scratch_shapes=[
            pltpu.SMEM((2, TM * TOP_K), jnp.int32),
            pltpu.VMEM((2, TOP_K, TM, 1, D // 2), jnp.uint32),
            pltpu.VMEM((TM, D // 2), jnp.uint32),
            pltpu.SemaphoreType.DMA((2,)),
            pltpu.SemaphoreType.DMA((2,)),
        ],
        compiler_params=_cparams(("arbitrary",), VMEM_LIMIT),
        name="combine_ln",
    )(dest_tiles, ys, gate4, x, mods, ln_g.reshape(1, D), ln_b.reshape(1, D))


def _dispatch_kernel(pad_lo_ref, nused_ref, dest_hbm, x_ref, mod_ref, xs_hbm, idx_smem, a_ref, zero_ref,
                     idx_sems, row_sems, *, bm):
    i = pl.program_id(0)
    n = pl.num_programs(0)
    slot = i % 2

    def wait_rows(s):
        for _ in range(TOP_K):
            pltpu.make_async_copy(a_ref.at[s], xs_hbm.at[pl.ds(0, TM)], row_sems.at[s]).wait()

    @pl.when(i == 0)
    def _():
        _index_copy(dest_hbm, idx_smem, idx_sems, 0).start()
        zero_ref[...] = jnp.zeros_like(zero_ref)
        n_exp = pad_lo_ref.shape[0]
        n_blk = xs_hbm.shape[0] // bm

        def zero_block(row0):
            return pltpu.make_async_copy(zero_ref, xs_hbm.at[pl.ds(pl.multiple_of(row0, bm), bm)],
                                         row_sems.at[0])

        for e in range(n_exp):
            @pl.when(pad_lo_ref[e] >= 0)
            def _():
                zero_block(pad_lo_ref[e]).start()

        def fill_tail(b, carry):
            zero_block(b * bm).start()
            return carry

        lax.fori_loop(nused_ref[0], n_blk, fill_tail, 0)
        for e in range(n_exp):
            @pl.when(pad_lo_ref[e] >= 0)
            def _():
                zero_block(0).wait()

        def wait_tail(b, carry):
            zero_block(0).wait()
            return carry

        lax.fori_loop(nused_ref[0], n_blk, wait_tail, 0)

    @pl.when(i + 1 < n)
    def _():
        _index_copy(dest_hbm, idx_smem, idx_sems, i + 1).start()

    @pl.when(i >= 2)
    def _():
        wait_rows(slot)

    shift = mod_ref[0:1, :]
    scale = mod_ref[1:2, :]
    a = x_ref[...] * (1.0 + scale) + shift
    half = a.shape[1] // 2
    a_ref[slot] = _pack_bf16_pair(a[:, :half], a[:, half:]).reshape(a_ref.shape[1:])
    _index_copy(dest_hbm, idx_smem, idx_sems, i).wait()

    def issue(r, carry):
        for k in range(TOP_K):
            row = idx_smem[slot, r * TOP_K + k]
            pltpu.make_async_copy(a_ref.at[slot, r], xs_hbm.at[row], row_sems.at[slot]).start(priority=k % 2)
        return carry

    lax.fori_loop(0, TM, issue, 0, unroll=8)

    @pl.when(i == n - 1)
    def _():
        wait_rows(slot)

        @pl.when(n > 1)
        def _():
            wait_rows(1 - slot)


def _dispatch(pad_lo, n_used, dest_tiles, x, mods, ls, n_rows, *, geom, latent_only):
    D = x.shape[1]
    n_tiles, row_tile, mod_row = _token_tiles(geom, latent_only)
    bm = EXPERT_BM
    return pl.pallas_call(
        functools.partial(_dispatch_kernel, bm=bm),
        out_shape=jax.ShapeDtypeStruct((n_rows, 1, D // 2), jnp.uint32),
        grid_spec=pltpu.PrefetchScalarGridSpec(
            num_scalar_prefetch=2,
            grid=(n_tiles,),
            in_specs=[
                pl.BlockSpec(memory_space=pl.ANY),
                pl.BlockSpec((TM, D), lambda i, p, n: (row_tile(i), 0)),
                pl.BlockSpec((None, None, 3, D), lambda i, p, n: (ls, mod_row(i), 0, 0)),
            ],
            out_specs=pl.BlockSpec(memory_space=pl.ANY),
            scratch_shapes=[
                pltpu.SMEM((2, TM * TOP_K), jnp.int32),
                pltpu.VMEM((2, TM, 1, D // 2), jnp.uint32),
                pltpu.VMEM((bm, 1, D // 2), jnp.uint32),
                pltpu.SemaphoreType.DMA((2,)),
                pltpu.SemaphoreType.DMA((2,)),
            ],
        ),
        compiler_params=_cparams(("arbitrary",), VMEM_LIMIT),
        name="moe_dispatch",
    )(pad_lo, n_used, dest_tiles, x, mods)


def _sigmoid(x):
    return 0.5 * jnp.tanh(0.5 * x) + 0.5


def _softplus(x):
    return jnp.maximum(x, 0.0) + jnp.log(1.0 + jnp.exp(-jnp.abs(x)))


def _gelu_tanh(x):
    c = math.sqrt(2.0 / math.pi)
    return 0.5 * x * (1.0 + jnp.tanh(c * (x + 0.044715 * (x * x * x))))


def _group_scan(a, b, row, reverse):
    for s in (1, 2, 4):
        if reverse:
            keep = row < SUBLANES - s
            shift = SUBLANES - s
        else:
            keep = row >= s
            shift = s
        a_sh = jnp.where(keep, pltpu.roll(a, shift, axis=0), 1.0)
        b_sh = jnp.where(keep, pltpu.roll(b, shift, axis=0), 0.0)
        b = b + a * b_sh
        a = a * a_sh
    return a, b


def _lru_kernel(ug_ref, ur_ref, cw_ref, cb_ref, wg_ref, bg_ref, lam_ref, o_ref,
                a0_ref, b0_ref, a1_ref, b1_ref, *, ctx_len):
    T, W = ur_ref.shape
    CH = 256
    n_chunks = T // CH
    ctx_chunks = ctx_len // CH
    row_c = lax.broadcasted_iota(jnp.int32, (CH, W), 0)
    kexp = (-0.5 * LRU_C * math.log2(math.e)) * _softplus(-lam_ref[...])

    def coeff_chunk(c, carry):
        c0 = pl.multiple_of(c * CH, CH)
        cur = ur_ref[pl.ds(c0, CH), :]
        prev8 = ur_ref[pl.ds(pl.multiple_of(jnp.maximum(c0 - SUBLANES, 0), SUBLANES), SUBLANES), :]
        next8 = ur_ref[pl.ds(pl.multiple_of(jnp.minimum(c0 + CH, T - SUBLANES), SUBLANES), SUBLANES), :]
        seg_start = jnp.logical_or(c == 0, c == ctx_chunks)
        seg_end = jnp.logical_or(c == ctx_chunks - 1, c == n_chunks - 1)
        p_last = jnp.where(seg_start, 0.0, prev8[SUBLANES - 1:SUBLANES, :])
        n_first = jnp.where(seg_end, 0.0, next8[0:1, :])
        n_second = jnp.where(seg_end, 0.0, next8[1:2, :])
        u_m1 = jnp.where(row_c == 0, p_last, pltpu.roll(cur, 1, axis=0))
        u_p1 = jnp.where(row_c == CH - 1, n_first, pltpu.roll(cur, CH - 1, axis=0))
        u_p2 = jnp.where(row_c == CH - 2, n_first,
                         jnp.where(row_c == CH - 1, n_second, pltpu.roll(cur, CH - 2, axis=0)))
        r = cb_ref[...] + u_m1 * cw_ref[0:1, :]
        r = r + cur * cw_ref[1:2, :]
        r = r + u_p1 * cw_ref[2:3, :]
        r = r + u_p2 * cw_ref[3:4, :]
        gts = jnp.dot(r.astype(BF16), wg_ref[0].astype(BF16), preferred_element_type=F32) + bg_ref[0]
        half_r = 0.5 * r
        for d, (a_ref, b_ref) in enumerate(((a0_ref, b0_ref), (a1_ref, b1_ref))):
            t_rec = jnp.tanh(gts[:, (2 * d) * W:(2 * d + 1) * W])
            t_in = jnp.tanh(gts[:, (2 * d + 1) * W:(2 * d + 2) * W])
            kd = kexp[d:d + 1, :]
            a = jnp.exp2(kd * t_rec + kd)
            a_ref[pl.ds(c0, CH), :] = a
            b_ref[pl.ds(c0, CH), :] = jnp.sqrt(jnp.maximum(1.0 - a * a, 0.0)) * (half_r * t_in + half_r)
        return carry

    lax.fori_loop(0, n_chunks, coeff_chunk, 0)

    row8 = lax.broadcasted_iota(jnp.int32, (SUBLANES, W), 0)
    n_groups = T // SUBLANES
    ctx_groups = ctx_len // SUBLANES

    def scan_step(k, carry):
        hf, hb = carry
        gf = pl.multiple_of(k * SUBLANES, SUBLANES)
        a, b = _group_scan(a0_ref[pl.ds(gf, SUBLANES), :], b0_ref[pl.ds(gf, SUBLANES), :], row8, False)
        h = a * hf + b
        a0_ref[pl.ds(gf, SUBLANES), :] = h
        hf = jnp.broadcast_to(h[SUBLANES - 1:SUBLANES, :], h.shape)
        g = jnp.where(k < ctx_groups, ctx_groups - 1 - k, n_groups - 1 - (k - ctx_groups))
        gb = pl.multiple_of(g * SUBLANES, SUBLANES)
        a, b = _group_scan(a1_ref[pl.ds(gb, SUBLANES), :], b1_ref[pl.ds(gb, SUBLANES), :], row8, True)
        h = a * hb + b
        a1_ref[pl.ds(gb, SUBLANES), :] = h
        hb = jnp.broadcast_to(h[0:1, :], h.shape)
        return hf, hb

    zero = jnp.zeros((SUBLANES, W), F32)
    lax.fori_loop(0, n_groups, scan_step, (zero, zero), unroll=4)

    def out_chunk(c, carry):
        c0 = pl.multiple_of(c * CH, CH)
        y = a0_ref[pl.ds(c0, CH), :] + a1_ref[pl.ds(c0, CH), :]
        o_ref[pl.ds(c0, CH), :] = (y * _gelu_tanh(ug_ref[pl.ds(c0, CH), :])).astype(o_ref.dtype)
        return carry

    lax.fori_loop(0, n_chunks, out_chunk, 0)


def _lru_mix(u, conv_w, conv_b, gate_w, gate_b, lam, *, nbatch, ctx_len):
    M, W2 = u.shape
    W = W2 // 2
    T = M // nbatch
    nblk = W // LRU_BLOCK_W
    bw = LRU_BLOCK_W
    wg = 0.5 * jnp.transpose(gate_w, (2, 3, 0, 1, 4)).reshape(nblk, bw, 4 * bw)
    bg = 0.5 * jnp.transpose(gate_b.reshape(2, 2, nblk, bw), (2, 0, 1, 3)).reshape(nblk, 1, 4 * bw)
    return pl.pallas_call(
        functools.partial(_lru_kernel, ctx_len=ctx_len),
        out_shape=jax.ShapeDtypeStruct((M, W), BF16),
        grid=(nbatch, nblk),
        in_specs=[
            pl.BlockSpec((T, bw), lambda b, n: (b, n)),
            pl.BlockSpec((T, bw), lambda b, n: (b, nblk + n)),
            pl.BlockSpec((CONV_W, bw), lambda b, n: (0, n)),
            pl.BlockSpec((1, bw), lambda b, n: (0, n)),
            pl.BlockSpec((1, bw, 4 * bw), lambda b, n: (n, 0, 0)),
            pl.BlockSpec((1, 1, 4 * bw), lambda b, n: (n, 0, 0)),
            pl.BlockSpec((2, bw), lambda b, n: (0, n)),
        ],
        out_specs=pl.BlockSpec((T, bw), lambda b, n: (b, n)),
        scratch_shapes=[pltpu.VMEM((T, bw), F32) for _ in range(4)],
        compiler_params=_cparams(("parallel", "parallel"), VMEM_LIMIT),
        name="lru_mix",
    )(u, u, conv_w, conv_b.reshape(1, W), wg, bg, lam)


def _attn_kernel(sink_ref, q_ref, k_ref, v_ref, o_ref, k2_ref, v2_ref, p_ref, *, ctx_len, n_kv):
    QB = q_ref.shape[0]
    T = k_ref.shape[0]
    n_lat = (T - ctx_len) // QB
    ctx_blocks = ctx_len // QB
    band = 3 * QB
    pairs = GROUP // 2
    qb = pl.program_id(1)
    j = qb - ctx_blocks
    start = jnp.clip(j - 1, 0, n_lat - 3)
    off = pl.multiple_of(ctx_len + start * QB, QB)
    lo_half = lax.broadcasted_iota(jnp.int32, (1, LANES), 1) < HEAD_DIM
    nt = (((1,), (1,)), ((), ()))

    def attend(with_band):
        nk = ctx_len + (band if with_band else 0)
        if with_band:
            col = lax.broadcasted_iota(jnp.int32, (QB, nk), 1)
            qpos = j * QB + lax.broadcasted_iota(jnp.int32, (QB, nk), 0)
            kpos = start * QB + (col - ctx_len)
            valid = jnp.logical_or(col < ctx_len, jnp.abs(qpos - kpos) <= WINDOW)
        for h in range(n_kv):
            slab = slice((h // 2) * LANES, (h // 2 + 1) * LANES)

            def two_blocks(x):
                swapped = jnp.concatenate([x[:, HEAD_DIM:], x[:, :HEAD_DIM]], axis=1)
                own, other = (x, swapped) if h % 2 == 0 else (swapped, x)
                zero = jnp.zeros_like(x)
                return jnp.where(lo_half, own, zero), jnp.where(lo_half, zero, other)

            for dst, src in ((k2_ref, k_ref), (v2_ref, v_ref)):
                top_c, bot_c = two_blocks(src[0:ctx_len, slab])
                dst[0:ctx_len, :] = top_c
                dst[nk:nk + ctx_len, :] = bot_c
                if with_band:
                    top_b, bot_b = two_blocks(src[pl.ds(off, band), slab])
                    dst[ctx_len:nk, :] = top_b
                    dst[nk + ctx_len:2 * nk, :] = bot_b
            c_q = h * GROUP * HEAD_DIM
            qs = jnp.concatenate([q_ref[:, c_q + p * LANES:c_q + (p + 1) * LANES] for p in range(pairs)], axis=0)
            s = lax.dot_general(qs, k2_ref[0:2 * nk, :], nt, preferred_element_type=F32)
            invs = []
            for p in range(pairs):
                rows = slice(p * QB, (p + 1) * QB)
                inv_pair = []
                for half in range(2):
                    c0 = half * nk
                    sink = sink_ref[h * GROUP + 2 * p + half] * LOG2_E
                    s_h = s[rows, c0:c0 + nk]
                    if with_band:
                        s_h = jnp.where(valid, s_h, NEG_INF)
                    m = jnp.maximum(jnp.max(s_h, axis=-1, keepdims=True), sink)
                    p_h = jnp.exp2(s_h - m)
                    denom = jnp.sum(p_h, axis=-1, keepdims=True) + jnp.exp2(sink - m)
                    p_ref[rows, c0:c0 + nk] = p_h.astype(BF16)
                    inv_pair.append(1.0 / denom)
                invs.append(inv_pair)
            o = jnp.dot(p_ref[:, 0:2 * nk], v2_ref[0:2 * nk, :], preferred_element_type=F32)
            for p in range(pairs):
                inv = jnp.where(lo_half, invs[p][0], invs[p][1])
                o_ref[:, c_q + p * LANES:c_q + (p + 1) * LANES] = (o[p * QB:(p + 1) * QB] * inv).astype(o_ref.dtype)

    @pl.when(qb >= ctx_blocks)
    def _():
        attend(True)

    @pl.when(qb < ctx_blocks)
    def _():
        attend(False)


def _attention(qkv, sinks, *, nbatch, ctx_len, q_dim, kv_dim):
    M = qkv.shape[0]
    T = M // nbatch
    nqb = T // Q_BLOCK
    kcol = q_dim // kv_dim
    return pl.pallas_call(
        functools.partial(_attn_kernel, ctx_len=ctx_len, n_kv=kv_dim // HEAD_DIM),
        out_shape=jax.ShapeDtypeStruct((M, q_dim), BF16),
        grid_spec=pltpu.PrefetchScalarGridSpec(
            num_scalar_prefetch=1,
            grid=(nbatch, nqb),
            in_specs=[
                pl.BlockSpec((Q_BLOCK, q_dim), lambda b, q, s: (b * nqb + q, 0)),
                pl.BlockSpec((T, kv_dim), lambda b, q, s: (b, kcol)),
                pl.BlockSpec((T, kv_dim), lambda b, q, s: (b, kcol + 1)),
            ],
            out_specs=pl.BlockSpec((Q_BLOCK, q_dim), lambda b, q, s: (b * nqb + q, 0)),
            scratch_shapes=[
                pltpu.VMEM((2 * (ctx_len + 3 * Q_BLOCK), LANES), BF16),
                pltpu.VMEM((2 * (ctx_len + 3 * Q_BLOCK), LANES), BF16),
                pltpu.VMEM((GROUP // 2 * Q_BLOCK, 2 * (ctx_len + 3 * Q_BLOCK)), BF16),
            ],
        ),
        compiler_params=_cparams(("parallel", "parallel"), VMEM_LIMIT),
        name="swa_attention",
    )(sinks, qkv, qkv, qkv)


def _split_bf16(x):
    hi = x.astype(BF16)
    lo = (x - hi.astype(F32)).astype(BF16)
    return hi, lo


def _router_kernel(x_ref, mod_ref, wr_ref, br_ref, e_ref, g_ref, r_ref, cnt_ref, carry_ref):
    i = pl.program_id(0)

    @pl.when(i == 0)
    def _():
        carry_ref[...] = jnp.zeros_like(carry_ref)

    shift = mod_ref[0:1, :]
    scale = mod_ref[1:2, :]
    a = x_ref[...] * (1.0 + scale) + shift
    a_hi, a_lo = _split_bf16(a)
    ne = br_ref.shape[1]
    hi_terms = jnp.dot(a_hi, wr_ref[...], preferred_element_type=F32)
    lo_term = jnp.dot(a_lo, wr_ref[:, 0:ne], preferred_element_type=F32)
    logits = (hi_terms[:, 0:ne] + (lo_term + hi_terms[:, ne:2 * ne])) + br_ref[...]
    tm = logits.shape[0]
    lane_e = lax.broadcasted_iota(jnp.int32, (tm, ne), 1)
    lane_o = lax.broadcasted_iota(jnp.int32, (tm, LANES), 1)
    vals, idxs, hots = [], [], []
    cur = logits
    for k in range(TOP_K):
        m = jnp.max(cur, axis=-1, keepdims=True)
        idx = jnp.min(jnp.where(cur == m, lane_e, ne), axis=-1, keepdims=True)
        hot = lane_e == idx
        vals.append(m)
        idxs.append(idx)
        hots.append(hot)
        cur = jnp.where(hot, -jnp.inf, cur)
    exps = [jnp.exp(v - vals[0]) for v in vals]
    inv = 1.0 / (exps[0] + exps[1] + exps[2] + exps[3])
    msum = jnp.zeros((tm, ne), F32)
    for hot in hots:
        msum = msum + hot.astype(F32)
    rr = lax.broadcasted_iota(jnp.int32, (tm, tm), 0)
    cc = lax.broadcasted_iota(jnp.int32, (tm, tm), 1)
    tri = (cc < rr).astype(BF16)
    before = jnp.dot(tri, msum.astype(BF16), preferred_element_type=F32) + carry_ref[...]
    e_out = jnp.zeros((tm, LANES), jnp.int32)
    g_out = jnp.zeros((tm, LANES), F32)
    r_out = jnp.zeros((tm, LANES), jnp.int32)
    for k in range(TOP_K):
        rank = jnp.sum(jnp.where(hots[k], before, 0.0), axis=-1, keepdims=True).astype(jnp.int32)
        e_out = jnp.where(lane_o == k, idxs[k], e_out)
        g_out = jnp.where(lane_o == k, exps[k] * inv, g_out)
        r_out = jnp.where(lane_o == k, rank, r_out)
    e_ref[...] = e_out
    g_ref[...] = g_out
    r_ref[...] = r_out
    carry_ref[...] = carry_ref[...] + jnp.sum(msum, axis=0, keepdims=True)
    cnt_ref[...] = carry_ref[...].astype(jnp.int32)


def _router(x, mods, ls, w_r, b_r, *, geom, latent_only):
    D = x.shape[1]
    ne = w_r.shape[1]
    n_tiles, row_tile, mod_row = _token_tiles(geom, latent_only)
    M = n_tiles * TM
    return pl.pallas_call(
        _router_kernel,
        out_shape=(
            jax.ShapeDtypeStruct((M, LANES), jnp.int32),
            jax.ShapeDtypeStruct((M, LANES), F32),
            jax.ShapeDtypeStruct((M, LANES), jnp.int32),
            jax.ShapeDtypeStruct((1, ne), jnp.int32),
        ),
        grid=(n_tiles,),
        in_specs=[
            pl.BlockSpec((TM, D), lambda i: (row_tile(i), 0)),
            pl.BlockSpec((None, None, 3, D), lambda i: (ls, mod_row(i), 0, 0)),
            pl.BlockSpec((D, 2 * ne), lambda i: (0, 0)),
            pl.BlockSpec((1, ne), lambda i: (0, 0)),
        ],
        out_specs=(
            pl.BlockSpec((TM, LANES), lambda i: (i, 0)),
            pl.BlockSpec((TM, LANES), lambda i: (i, 0)),
            pl.BlockSpec((TM, LANES), lambda i: (i, 0)),
            pl.BlockSpec((1, ne), lambda i: (0, 0)),
        ),
        scratch_shapes=[pltpu.VMEM((1, ne), F32)],
        compiler_params=_cparams(("arbitrary",), VMEM_LIMIT),
        name="moe_router",
    )(x, mods, jnp.concatenate(_split_bf16(w_r), axis=1), b_r.reshape(1, ne))


def _expert_kernel(blk_e_ref, slot_ref, first_ref, next_e_ref, nused_ref, xs_ref, bgu_ref, bdn_ref,
                   wgu_hbm, wdn_hbm, o_ref, wgu_buf, wdn_buf, x2d_ref, sem_gu, sem_dn, *, layer):
    b = pl.program_id(0)
    de, d = wdn_buf.shape[1:]
    half = d // 2

    def fetch(expert, slot):
        cps = []
        for hbm, buf, sem, chunks in ((wgu_hbm, wgu_buf, sem_gu, WEIGHT_DMA_CHUNKS),
                                      (wdn_hbm, wdn_buf, sem_dn, WEIGHT_DMA_CHUNKS // 2)):
            rows = buf.shape[1] // chunks
            for c in range(chunks):
                rs = pl.ds(c * rows, rows)
                cps.append(pltpu.make_async_copy(hbm.at[layer, expert, rs], buf.at[slot, rs], sem.at[slot]))
        return cps

    @pl.when(b < nused_ref[0])
    def _():
        slot = slot_ref[b]

        @pl.when(b == 0)
        def _():
            for c, cp in enumerate(fetch(blk_e_ref[0], 0)):
                cp.start(priority=c % 2)

        @pl.when(first_ref[b] == 1)
        def _():
            @pl.when(next_e_ref[b] >= 0)
            def _():
                for c, cp in enumerate(fetch(next_e_ref[b], 1 - slot)):
                    cp.start(priority=c % 2)
            for cp in fetch(blk_e_ref[b], slot):
                cp.wait()

        x2d_ref[...] = xs_ref[...].reshape(x2d_ref.shape)
        x_lo, x_hi = (v.astype(BF16) for v in _unpack_bf16_pair(x2d_ref[...]))
        hs = []
        for c in range(2 * de // 256):
            sl = slice(c * 256, (c + 1) * 256)
            acc = jnp.dot(x_lo, wgu_buf[slot, 0:half, sl].astype(BF16), preferred_element_type=F32)
            acc = acc + jnp.dot(x_hi, wgu_buf[slot, half:d, sl].astype(BF16), preferred_element_type=F32)
            hs.append(acc + bgu_ref[:, sl])
        h = jnp.concatenate(hs, axis=1)
        g = jnp.minimum(h[:, :de], SWIGLU_LIMIT)
        u = jnp.clip(h[:, de:], -SWIGLU_LIMIT, SWIGLU_LIMIT)
        act = ((u + 1.0) * (g * _sigmoid(SWIGLU_ALPHA * g))).astype(BF16)
        ys = []
        for c in range(d // 512):
            sl = slice(c * 512, (c + 1) * 512)
            ys.append(jnp.dot(act, wdn_buf[slot, :, sl].astype(BF16), preferred_element_type=F32) + bdn_ref[:, sl])
        nh = len(ys) // 2
        for c in range(nh):
            packed = _pack_bf16_pair(ys[c], ys[c + nh])
            o_ref[:, :, c * 512:(c + 1) * 512] = packed.reshape(packed.shape[0], 1, 512)

    @pl.when(b >= nused_ref[0])
    def _():
        o_ref[...] = jnp.zeros_like(o_ref)


def _expert_ffn(xs, blk_tabs, n_used, layer, w_gu, b_gu, w_dn, b_dn):
    P = xs.shape[0]
    _, ne, D, de2 = w_gu.shape
    de = de2 // 2
    n_blk = P // EXPERT_BM
    blk_e, slot, first, next_e = blk_tabs
    return pl.pallas_call(
        functools.partial(_expert_kernel, layer=layer),
        out_shape=jax.ShapeDtypeStruct((P, 1, D // 2), jnp.uint32),
        grid_spec=pltpu.PrefetchScalarGridSpec(
            num_scalar_prefetch=5,
            grid=(n_blk,),
            in_specs=[
                pl.BlockSpec((EXPERT_BM, 1, D // 2), lambda b, e, s, f, x, n: (jnp.minimum(b, n[0] - 1), 0, 0)),
                pl.BlockSpec((None, None, 1, de2), lambda b, e, s, f, x, n: (layer, e[b], 0, 0)),
                pl.BlockSpec((None, None, 1, D), lambda b, e, s, f, x, n: (layer, e[b], 0, 0)),
                pl.BlockSpec(memory_space=pl.ANY),
                pl.BlockSpec(memory_space=pl.ANY),
            ],
            out_specs=pl.BlockSpec((EXPERT_BM, 1, D // 2), lambda b, e, s, f, x, n: (b, 0, 0)),
            scratch_shapes=[
                pltpu.VMEM((2, D, de2), F32),
                pltpu.VMEM((2, de, D), F32),
                pltpu.VMEM((EXPERT_BM, D // 2), jnp.uint32),
                pltpu.SemaphoreType.DMA((2,)),
                pltpu.SemaphoreType.DMA((2,)),
            ],
        ),
        compiler_params=_cparams(("arbitrary",), VMEM_LIMIT),
        name="expert_ffn",
    )(blk_e, slot, first, next_e, n_used, xs, b_gu.reshape(b_gu.shape[0], ne, 1, de2),
      b_dn.reshape(b_dn.shape[0], ne, 1, D), w_gu, w_dn)


def _moe_layer(x, mods, ls, layer, w_r, b_r, w_gu, b_gu, w_dn, b_dn, ln_g, ln_b, *, alpha, geom, latent_only):
    ne = w_r.shape[1]
    bm = EXPERT_BM
    e128, g128, r128, counts = _router(x, mods, ls, w_r, b_r, geom=geom, latent_only=latent_only)
    M = e128.shape[0]
    top_e = e128[:, :TOP_K]
    rank = r128[:, :TOP_K]
    counts = counts[0]
    padded = ((counts + bm - 1) // bm) * bm
    ends = jnp.cumsum(padded)
    pstart = ends - padded
    experts = jnp.arange(ne, dtype=jnp.int32)
    dest = jnp.sum(jnp.where(top_e[..., None] == experts, pstart, 0), axis=-1) + rank
    dest_tiles = dest.astype(jnp.int32).reshape(M // TM, TM * TOP_K)
    A = M * TOP_K
    P = (-(-A // bm)) * bm + ne * bm
    n_blk = P // bm
    n_used = (ends[-1] // bm).astype(jnp.int32).reshape(1)
    blk = jnp.arange(n_blk, dtype=jnp.int32)
    blk_row = jnp.minimum(blk, n_used[0] - 1) * bm
    blk_e = jnp.minimum(jnp.sum(ends[None, :] <= blk_row[:, None], axis=1), ne - 1).astype(jnp.int32)
    first = jnp.logical_and(blk < n_used[0],
                            jnp.concatenate([jnp.ones((1,), bool), blk_e[1:] != blk_e[:-1]])).astype(jnp.int32)
    slot = ((jnp.cumsum(first) - 1) % 2).astype(jnp.int32)
    later = jnp.logical_and(experts[None, :] > experts[:, None], (padded > 0)[None, :])
    next_nonempty = jnp.min(jnp.where(later, experts[None, :], ne), axis=1)
    next_nonempty = jnp.where(next_nonempty == ne, -1, next_nonempty)
    next_e = jnp.sum(jnp.where(blk_e[:, None] == experts[None, :], next_nonempty[None, :], 0), axis=1).astype(jnp.int32)
    pad_lo = jnp.where(padded > 0, ends - bm, -1).astype(jnp.int32)
    xs = _dispatch(pad_lo, n_used, dest_tiles, x, mods, ls, P, geom=geom, latent_only=latent_only)
    ys = _expert_ffn(xs, (blk_e, slot, first, next_e), n_used, layer, w_gu, b_gu, w_dn, b_dn)
    return _combine_ln(dest_tiles, ys, g128, x, mods, ls, ln_g, ln_b, alpha=alpha, geom=geom,
                       latent_only=latent_only)


def _rope_tables(seq, ctx_len):
    rows = seq // GRID_W
    row_pos = jnp.repeat(jnp.arange(rows), GRID_W).astype(F32)
    col_pos = jnp.tile(jnp.arange(GRID_W), rows).astype(F32)
    n = HEAD_DIM // 2
    freqs = ROPE_BASE ** (-jnp.arange(0, n, 2, dtype=F32) / n)
    ang_r = row_pos[:, None] * freqs[None, :]
    ang_c = col_pos[:, None] * freqs[None, :]
    cos = jnp.concatenate([jnp.cos(ang_r), jnp.cos(ang_r), jnp.cos(ang_c), jnp.cos(ang_c)], axis=1)
    sin = jnp.concatenate([-jnp.sin(ang_r), jnp.sin(ang_r), -jnp.sin(ang_c), jnp.sin(ang_c)], axis=1)
    cos = jnp.concatenate([jnp.ones((ctx_len, HEAD_DIM), F32), cos], axis=0)
    sin = jnp.concatenate([jnp.zeros((ctx_len, HEAD_DIM), F32), sin], axis=0)
    cos = jnp.tile(cos, (1, LANES // HEAD_DIM))
    sin = jnp.tile(sin, (1, LANES // HEAD_DIM))
    scale = HEAD_DIM ** -0.5 * LOG2_E
    return jnp.stack([cos * scale, sin * scale, cos, sin], axis=0)


def kernel(x, c, ctx, c_ctx, ada_w, ada_b, ln_g, ln_b, lru_w_in, lru_conv_w, lru_conv_b, lru_gate_w, lru_gate_b, lru_lambda, lru_w_out, attn_w_qkv, attn_sinks, attn_w_o, router_w, router_b, moe_w_gu, moe_b_gu, moe_w_down, moe_b_down):
    B, S, D = x.shape
    C = ctx.shape[1]
    depth = ada_w.shape[0]
    T = C + S
    assert C % TM == 0 and S % TM == 0 and C % Q_BLOCK == 0
    geom = (T // TM, C // TM, B)
    alpha = (2.0 * depth) ** 0.25
    kv_dim = (attn_w_qkv.shape[2] - D) // 2

    act = jax.nn.silu(jnp.concatenate([c, c_ctx[None, :]], axis=0))
    mods = _ada_params(act, ada_w.reshape(depth * 2, D, 3 * D), ada_b.reshape(depth * 2, 3 * D))
    mods = mods.reshape(depth * 2, SUBLANES, 3, D)
    tab = _rope_tables(S, C)

    z = jnp.concatenate([ctx, x], axis=1).reshape(B * T, D)
    for i in range(depth):
        j = i // 2
        if i % 2 == 0:
            u = _mod_matmul(z, mods, 2 * i, lru_w_in, j, tn=D, out_dtype=F32, geom=geom)
            y = _lru_mix(u, lru_conv_w[j], lru_conv_b[j], lru_gate_w[j], lru_gate_b[j], lru_lambda[j],
                         nbatch=B, ctx_len=C)
            w_out = lru_w_out
        else:
            qkv = _mod_matmul(z, mods, 2 * i, attn_w_qkv, j, tn=D + 2 * kv_dim, out_dtype=BF16,
                              geom=geom, rope=(tab, (D, kv_dim)))
            y = _attention(qkv, attn_sinks[j], nbatch=B, ctx_len=C, q_dim=D, kv_dim=kv_dim)
            w_out = attn_w_o
        z = _proj_ln(y, w_out, j, z, mods, 2 * i, ln_g[i, 0], ln_b[i, 0], alpha=alpha, geom=geom)
        z = _moe_layer(z, mods, 2 * i + 1, i, router_w[i], router_b[i], moe_w_gu, moe_b_gu,
                       moe_w_down, moe_b_down, ln_g[i, 1], ln_b[i, 1], alpha=alpha, geom=geom,
                       latent_only=(i == depth - 1))
    return z.reshape(B, S, D)
```

```python
import functools
import math

import jax
import jax.numpy as jnp
from jax import lax
from jax.experimental import pallas as pl
from jax.experimental.pallas import tpu as pltpu

GRID_W = 64
LN_EPS = 1e-5
LRU_BLOCK_W = 256
CONV_W = 4
LRU_C = 8.0
HEAD_DIM = 64
GROUP = 8
WINDOW = 128
Q_BLOCK = 128
ROPE_BASE = 10000.0
NEG_INF = -1e30
TOP_K = 4
SWIGLU_LIMIT = 7.0
SWIGLU_ALPHA = 1.702
LOG2_E = math.log2(math.e)

LANES = 128
SUBLANES = 8
TM = 256
EXPERT_BM = 256
WEIGHT_DMA_CHUNKS = 4
VMEM_LIMIT = 56 * 1024 * 1024

BF16 = jnp.bfloat16
F32 = jnp.float32


def _cparams(sem, vmem=None):
    return pltpu.CompilerParams(dimension_semantics=sem, vmem_limit_bytes=vmem)


def _ada_kernel(a_ref, w_ref, b_ref, o_ref):
    acc = jnp.dot(a_ref[...], w_ref[0].astype(BF16), preferred_element_type=F32)
    o_ref[0] = acc[0:SUBLANES] + acc[SUBLANES:2 * SUBLANES] + b_ref[0]


def _ada_params(act, w, bias):
    R, D = act.shape
    L, _, N = w.shape
    tn = 1024
    a8 = jnp.zeros((SUBLANES, D), F32).at[:R].set(act)
    hi = a8.astype(BF16)
    lo = (a8 - hi.astype(F32)).astype(BF16)
    a16 = jnp.concatenate([hi, lo], axis=0)
    return pl.pallas_call(
        _ada_kernel,
        out_shape=jax.ShapeDtypeStruct((L, SUBLANES, N), F32),
        grid=(L, N // tn),
        in_specs=[
            pl.BlockSpec((2 * SUBLANES, D), lambda l, n: (0, 0)),
            pl.BlockSpec((1, D, tn), lambda l, n: (l, 0, n)),
            pl.BlockSpec((1, 1, tn), lambda l, n: (l, 0, n)),
        ],
        out_specs=pl.BlockSpec((1, SUBLANES, tn), lambda l, n: (l, 0, n)),
        compiler_params=_cparams(("parallel", "parallel"), VMEM_LIMIT),
        name="ada_params",
    )(a16, w, bias.reshape(L, 1, N))


def _mod_row(i, tiles_per_batch, ctx_tiles, nbatch):
    return jnp.where(i % tiles_per_batch < ctx_tiles, nbatch, i // tiles_per_batch)


def _token_tiles(geom, latent_only):
    tpb, ctx_tiles, nbatch = geom
    if latent_only:
        lpb = tpb - ctx_tiles
        return lpb * nbatch, (lambda i: (i // lpb) * tpb + ctx_tiles + i % lpb), (lambda i: i // lpb)
    return tpb * nbatch, (lambda i: i), (lambda i: _mod_row(i, tpb, ctx_tiles, nbatch))


def _rope_chunk(y, cos, sin_signed, first_half):
    rot = jnp.where(first_half, pltpu.roll(y, LANES - 16, axis=1), pltpu.roll(y, 16, axis=1))
    return y * cos + rot * sin_signed


def _modmm_kernel(x_ref, mod_ref, w_ref, *rest, rope_cols):
    if rope_cols is None:
        (o_ref,) = rest
    else:
        tab_ref, o_ref = rest
    shift = mod_ref[0:1, :]
    scale = mod_ref[1:2, :]
    a = (x_ref[...] * (1.0 + scale) + shift).astype(BF16)
    tn = o_ref.shape[1]
    ch = 512 if tn % 512 == 0 else 256
    for c in range(tn // ch):
        acc = jnp.dot(a, w_ref[:, c * ch:(c + 1) * ch].astype(BF16), preferred_element_type=F32)
        if rope_cols is None:
            o_ref[:, c * ch:(c + 1) * ch] = acc.astype(o_ref.dtype)
        else:
            q_cols, k_cols = rope_cols
            lane = lax.broadcasted_iota(jnp.int32, (acc.shape[0], LANES), 1)
            first_half = (lane % 32) < 16
            for s in range(ch // LANES):
                col = c * ch + s * LANES
                y = acc[:, s * LANES:(s + 1) * LANES]
                if col < q_cols:
                    y = _rope_chunk(y, tab_ref[0], tab_ref[1], first_half)
                elif col < q_cols + k_cols:
                    y = _rope_chunk(y, tab_ref[2], tab_ref[3], first_half)
                o_ref[:, col:col + LANES] = y.astype(o_ref.dtype)


def _mod_matmul(x, mods, ls, w_stack, layer, *, tn, out_dtype, geom, rope=None):
    M, K = x.shape
    N = w_stack.shape[2]
    tpb, ctx_tiles, nbatch = geom
    in_specs = [
        pl.BlockSpec((TM, K), lambda j, i: (i, 0)),
        pl.BlockSpec((None, None, 3, K), lambda j, i: (ls, _mod_row(i, tpb, ctx_tiles, nbatch), 0, 0)),
        pl.BlockSpec((None, K, tn), lambda j, i: (layer, 0, j)),
    ]
    args = [x, mods, w_stack]
    rope_cols = None
    if rope is not None:
        tab, rope_cols = rope
        in_specs.append(pl.BlockSpec((4, TM, LANES), lambda j, i: (0, i % tpb, 0)))
        args.append(tab)
    return pl.pallas_call(
        functools.partial(_modmm_kernel, rope_cols=rope_cols),
        out_shape=jax.ShapeDtypeStruct((M, N), out_dtype),
        grid=(N // tn, M // TM),
        in_specs=in_specs,
        out_specs=pl.BlockSpec((TM, tn), lambda j, i: (i, j)),
        compiler_params=_cparams(("parallel", "parallel"), VMEM_LIMIT),
        name="mod_matmul",
    )(*args)


def _layer_norm_rows(z, g, b):
    mu = jnp.mean(z, axis=-1, keepdims=True)
    zc = z - mu
    var = jnp.mean(zc * zc, axis=-1, keepdims=True)
    return zc * lax.rsqrt(var + LN_EPS) * g + b


def _proj_ln_kernel(y_ref, w_ref, x_ref, mod_ref, g_ref, b_ref, o_ref, z_ref, *, alpha):
    gate = mod_ref[2:3, :]
    y = y_ref[...]
    n = o_ref.shape[1]
    ch = 512
    for c in range(n // ch):
        sl = slice(c * ch, (c + 1) * ch)
        acc = jnp.dot(y, w_ref[:, sl].astype(BF16), preferred_element_type=F32)
        z_ref[:, sl] = alpha * x_ref[:, sl] + gate[:, sl] * acc
    o_ref[...] = _layer_norm_rows(z_ref[...], g_ref[...], b_ref[...])


def _proj_ln_route_kernel(y_ref, w_ref, x_ref, mod_ref, mod2_ref, g_ref, b_ref, wr_ref, br_ref,
                          o_ref, e_ref, gate_ref, r_ref, cnt_ref, z_ref, carry_ref, *, alpha):
    _proj_ln_kernel(y_ref, w_ref, x_ref, mod_ref, g_ref, b_ref, o_ref, z_ref, alpha=alpha)
    _route_tile(o_ref[...], mod2_ref, wr_ref, br_ref, e_ref, gate_ref, r_ref, cnt_ref, carry_ref)


def _proj_ln_route(y, w_stack, j, x, mods, ls, ln_g, ln_b, w_r, b_r, *, alpha, geom):
    M, K = y.shape
    D = w_stack.shape[2]
    ne = w_r.shape[1]
    tpb, ctx_tiles, nbatch = geom
    mod_spec = lambda sub: pl.BlockSpec((None, None, 3, D),
                                        lambda i: (sub, _mod_row(i, tpb, ctx_tiles, nbatch), 0, 0))
    lane_spec = pl.BlockSpec((TM, LANES), lambda i: (i, 0))
    return pl.pallas_call(
        functools.partial(_proj_ln_route_kernel, alpha=alpha),
        out_shape=(
            jax.ShapeDtypeStruct((M, D), F32),
            jax.ShapeDtypeStruct((M, LANES), jnp.int32),
            jax.ShapeDtypeStruct((M, LANES), F32),
            jax.ShapeDtypeStruct((M, LANES), jnp.int32),
            jax.ShapeDtypeStruct((1, ne), jnp.int32),
        ),
        grid=(M // TM,),
        in_specs=[
            pl.BlockSpec((TM, K), lambda i: (i, 0)),
            pl.BlockSpec((None, K, D), lambda i: (j, 0, 0)),
            pl.BlockSpec((TM, D), lambda i: (i, 0)),
            mod_spec(ls),
            mod_spec(ls + 1),
            pl.BlockSpec((1, D), lambda i: (0, 0)),
            pl.BlockSpec((1, D), lambda i: (0, 0)),
            pl.BlockSpec((D, 2 * ne), lambda i: (0, 0)),
            pl.BlockSpec((1, ne), lambda i: (0, 0)),
        ],
        out_specs=(
            pl.BlockSpec((TM, D), lambda i: (i, 0)),
            lane_spec, lane_spec, lane_spec,
            pl.BlockSpec((1, ne), lambda i: (0, 0)),
        ),
        scratch_shapes=[pltpu.VMEM((TM, D), F32), pltpu.VMEM((1, ne), F32)],
        compiler_params=_cparams(("arbitrary",), VMEM_LIMIT),
        name="proj_ln_route",
    )(y, w_stack, x, mods, mods, ln_g.reshape(1, D), ln_b.reshape(1, D),
      jnp.concatenate(_split_bf16(w_r), axis=1), b_r.reshape(1, ne))


def _proj_ln(y, w_stack, j, x, mods, ls, ln_g, ln_b, *, alpha, geom):
    M, K = y.shape
    D = w_stack.shape[2]
    tpb, ctx_tiles, nbatch = geom
    return pl.pallas_call(
        functools.partial(_proj_ln_kernel, alpha=alpha),
        out_shape=jax.ShapeDtypeStruct((M, D), F32),
        grid=(M // TM,),
        in_specs=[
            pl.BlockSpec((TM, K), lambda i: (i, 0)),
            pl.BlockSpec((None, K, D), lambda i: (j, 0, 0)),
            pl.BlockSpec((TM, D), lambda i: (i, 0)),
            pl.BlockSpec((None, None, 3, D), lambda i: (ls, _mod_row(i, tpb, ctx_tiles, nbatch), 0, 0)),
            pl.BlockSpec((1, D), lambda i: (0, 0)),
            pl.BlockSpec((1, D), lambda i: (0, 0)),
        ],
        out_specs=pl.BlockSpec((TM, D), lambda i: (i, 0)),
        scratch_shapes=[pltpu.VMEM((TM, D), F32)],
        compiler_params=_cparams(("parallel",), VMEM_LIMIT),
        name="proj_ln",
    )(y, w_stack, x, mods, ln_g.reshape(1, D), ln_b.reshape(1, D))


def _pack_bf16_pair(lo, hi):
    lo_w = lax.bitcast_convert_type(lo.astype(BF16).astype(F32), jnp.uint32) >> 16
    hi_w = lax.bitcast_convert_type(hi.astype(BF16).astype(F32), jnp.uint32) & jnp.uint32(0xFFFF0000)
    return lo_w | hi_w


def _unpack_bf16_pair(w):
    lo = lax.bitcast_convert_type(w << 16, F32)
    hi = lax.bitcast_convert_type(w & jnp.uint32(0xFFFF0000), F32)
    return lo, hi


def _index_copy(idx_hbm, idx_smem, sems, tile):
    return pltpu.make_async_copy(idx_hbm.at[tile], idx_smem.at[tile % 2], sems.at[tile % 2])


def _combine_ln_kernel(dest_hbm, ys_hbm, gate4_ref, x_ref, mod_ref, g_ref, b_ref, o_ref,
                       idx_smem, buf_ref, rows2d_ref, idx_sems, row_sems, *, alpha):
    i = pl.program_id(0)
    n = pl.num_programs(0)

    def gather_rows(tile):
        slot = tile % 2

        def issue(r, carry):
            for k in range(TOP_K):
                row = idx_smem[slot, r * TOP_K + k]
                pltpu.make_async_copy(ys_hbm.at[row], buf_ref.at[slot, k, r],
                                      row_sems.at[slot]).start(priority=k % 2)
            return carry

        lax.fori_loop(0, TM, issue, 0, unroll=8)

    @pl.when(i == 0)
    def _():
        cp = _index_copy(dest_hbm, idx_smem, idx_sems, 0)
        cp.start()
        cp.wait()
        gather_rows(0)

        @pl.when(n > 1)
        def _():
            _index_copy(dest_hbm, idx_smem, idx_sems, 1).start()

    @pl.when(i + 1 < n)
    def _():
        _index_copy(dest_hbm, idx_smem, idx_sems, i + 1).wait()
        gather_rows(i + 1)

        @pl.when(i + 2 < n)
        def _():
            _index_copy(dest_hbm, idx_smem, idx_sems, i + 2).start()

    slot = i % 2
    for k in range(TOP_K):
        pltpu.make_async_copy(ys_hbm.at[pl.ds(0, TM)], buf_ref.at[slot, k], row_sems.at[slot]).wait()
    y_lo = y_hi = None
    for k in range(TOP_K):
        rows2d_ref[...] = buf_ref[slot, k].reshape(rows2d_ref.shape)
        lo, hi = _unpack_bf16_pair(rows2d_ref[...])
        gk = gate4_ref[:, k:k + 1]
        y_lo = lo * gk if y_lo is None else y_lo + lo * gk
        y_hi = hi * gk if y_hi is None else y_hi + hi * gk
    y = jnp.concatenate([y_lo, y_hi], axis=1)
    z = alpha * x_ref[...] + mod_ref[2:3, :] * y
    o_ref[...] = _layer_norm_rows(z, g_ref[...], b_ref[...])


def _combine_ln(dest_tiles, ys, gate4, x, mods, ls, ln_g, ln_b, *, alpha, geom, latent_only):
    D = x.shape[1]
    n_tiles, row_tile, mod_row = _token_tiles(geom, latent_only)
    return pl.pallas_call(
        functools.partial(_combine_ln_kernel, alpha=alpha),
        out_shape=jax.ShapeDtypeStruct((n_tiles * TM, D), F32),
        grid=(n_tiles,),
        in_specs=[
            pl.BlockSpec(memory_space=pl.ANY),
            pl.BlockSpec(memory_space=pl.ANY),
            pl.BlockSpec((TM, LANES), lambda i: (i, 0)),
            pl.BlockSpec((TM, D), lambda i: (row_tile(i), 0)),
            pl.BlockSpec((None, None, 3, D), lambda i: (ls, mod_row(i), 0, 0)),
            pl.BlockSpec((1, D), lambda i: (0, 0)),
            pl.BlockSpec((1, D), lambda i: (0, 0)),
        ],
        out_specs=pl.BlockSpec((TM, D), lambda i: (i, 0)),
        scratch_shapes=[
            pltpu.SMEM((2, TM * TOP_K), jnp.int32),
            pltpu.VMEM((2, TOP_K, TM, 1, D // 2), jnp.uint32),
            pltpu.VMEM((TM, D // 2), jnp.uint32),
            pltpu.SemaphoreType.DMA((2,)),
            pltpu.SemaphoreType.DMA((2,)),
        ],
        compiler_params=_cparams(("arbitrary",), VMEM_LIMIT),
        name="combine_ln",
    )(dest_tiles, ys, gate4, x, mods, ln_g.reshape(1, D), ln_b.reshape(1, D))


def _dispatch_kernel(pad_lo_ref, nused_ref, dest_hbm, x_ref, mod_ref, xs_hbm, idx_smem, a_ref, zero_ref,
                     idx_sems, row_sems, *, bm):
    i = pl.program_id(0)
    n = pl.num_programs(0)
    slot = i % 2

    def wait_rows(s):
        for _ in range(TOP_K):
            pltpu.make_async_copy(a_ref.at[s], xs_hbm.at[pl.ds(0, TM)], row_sems.at[s]).wait()

    @pl.when(i == 0)
    def _():
        _index_copy(dest_hbm, idx_smem, idx_sems, 0).start()
        zero_ref[...] = jnp.zeros_like(zero_ref)
        n_exp = pad_lo_ref.shape[0]
        n_blk = xs_hbm.shape[0] // bm

        def zero_block(row0):
            return pltpu.make_async_copy(zero_ref, xs_hbm.at[pl.ds(pl.multiple_of(row0, bm), bm)],
                                         row_sems.at[0])

        for e in range(n_exp):
            @pl.when(pad_lo_ref[e] >= 0)
            def _():
                zero_block(pad_lo_ref[e]).start()

        def fill_tail(b, carry):
            zero_block(b * bm).start()
            return carry

        lax.fori_loop(nused_ref[0], n_blk, fill_tail, 0)
        for e in range(n_exp):
            @pl.when(pad_lo_ref[e] >= 0)
            def _():
                zero_block(0).wait()

        def wait_tail(b, carry):
            zero_block(0).wait()
            return carry

        lax.fori_loop(nused_ref[0], n_blk, wait_tail, 0)

    @pl.when(i + 1 < n)
    def _():
        _index_copy(dest_hbm, idx_smem, idx_sems, i + 1).start()

    @pl.when(i >= 2)
    def _():
        wait_rows(slot)

    shift = mod_ref[0:1, :]
    scale = mod_ref[1:2, :]
    a = x_ref[...] * (1.0 + scale) + shift
    half = a.shape[1] // 2
    a_ref[slot] = _pack_bf16_pair(a[:, :half], a[:, half:]).reshape(a_ref.shape[1:])
    _index_copy(dest_hbm, idx_smem, idx_sems, i).wait()

    def issue(r, carry):
        for k in range(TOP_K):
            row = idx_smem[slot, r * TOP_K + k]
            pltpu.make_async_copy(a_ref.at[slot, r], xs_hbm.at[row], row_sems.at[slot]).start(priority=k % 2)
        return carry

    lax.fori_loop(0, TM, issue, 0, unroll=8)

    @pl.when(i == n - 1)
    def _():
        wait_rows(slot)

        @pl.when(n > 1)
        def _():
            wait_rows(1 - slot)


def _dispatch(pad_lo, n_used, dest_tiles, x, mods, ls, n_rows, *, geom, latent_only):
    D = x.shape[1]
    n_tiles, row_tile, mod_row = _token_tiles(geom, latent_only)
    bm = EXPERT_BM
    return pl.pallas_call(
        functools.partial(_dispatch_kernel, bm=bm),
        out_shape=jax.ShapeDtypeStruct((n_rows, 1, D // 2), jnp.uint32),
        grid_spec=pltpu.PrefetchScalarGridSpec(
            num_scalar_prefetch=2,
            grid=(n_tiles,),
            in_specs=[
                pl.BlockSpec(memory_space=pl.ANY),
                pl.BlockSpec((TM, D), lambda i, p, n: (row_tile(i), 0)),
                pl.BlockSpec((None, None, 3, D), lambda i, p, n: (ls, mod_row(i), 0, 0)),
            ],
            out_specs=pl.BlockSpec(memory_space=pl.ANY),
            scratch_shapes=[
                pltpu.SMEM((2, TM * TOP_K), jnp.int32),
                pltpu.VMEM((2, TM, 1, D // 2), jnp.uint32),
                pltpu.VMEM((bm, 1, D // 2), jnp.uint32),
                pltpu.SemaphoreType.DMA((2,)),
                pltpu.SemaphoreType.DMA((2,)),
            ],
        ),
        compiler_params=_cparams(("arbitrary",), VMEM_LIMIT),
        name="moe_dispatch",
    )(pad_lo, n_used, dest_tiles, x, mods)


def _sigmoid(x):
    return 0.5 * jnp.tanh(0.5 * x) + 0.5


def _softplus(x):
    return jnp.maximum(x, 0.0) + jnp.log(1.0 + jnp.exp(-jnp.abs(x)))


def _gelu_tanh(x):
    c = math.sqrt(2.0 / math.pi)
    return 0.5 * x * (1.0 + jnp.tanh(c * (x + 0.044715 * (x * x * x))))


def _group_scan(a, b, row, reverse):
    for s in (1, 2, 4):
        if reverse:
            keep = row < SUBLANES - s
            shift = SUBLANES - s
        else:
            keep = row >= s
            shift = s
        a_sh = jnp.where(keep, pltpu.roll(a, shift, axis=0), 1.0)
        b_sh = jnp.where(keep, pltpu.roll(b, shift, axis=0), 0.0)
        b = b + a * b_sh
        a = a * a_sh
    return a, b


def _lru_kernel(ug_ref, ur_ref, cw_ref, cb_ref, wg_ref, bg_ref, lam_ref, o_ref,
                a0_ref, b0_ref, a1_ref, b1_ref, *, ctx_len):
    T, W = ur_ref.shape
    CH = 256
    n_chunks = T // CH
    ctx_chunks = ctx_len // CH
    row_c = lax.broadcasted_iota(jnp.int32, (CH, W), 0)
    kexp = (-0.5 * LRU_C * math.log2(math.e)) * _softplus(-lam_ref[...])

    def coeff_chunk(c, carry):
        c0 = pl.multiple_of(c * CH, CH)
        cur = ur_ref[pl.ds(c0, CH), :]
        prev8 = ur_ref[pl.ds(pl.multiple_of(jnp.maximum(c0 - SUBLANES, 0), SUBLANES), SUBLANES), :]
        next8 = ur_ref[pl.ds(pl.multiple_of(jnp.minimum(c0 + CH, T - SUBLANES), SUBLANES), SUBLANES), :]
        seg_start = jnp.logical_or(c == 0, c == ctx_chunks)
        seg_end = jnp.logical_or(c == ctx_chunks - 1, c == n_chunks - 1)
        p_last = jnp.where(seg_start, 0.0, prev8[SUBLANES - 1:SUBLANES, :])
        n_first = jnp.where(seg_end, 0.0, next8[0:1, :])
        n_second = jnp.where(seg_end, 0.0, next8[1:2, :])
        u_m1 = jnp.where(row_c == 0, p_last, pltpu.roll(cur, 1, axis=0))
        u_p1 = jnp.where(row_c == CH - 1, n_first, pltpu.roll(cur, CH - 1, axis=0))
        u_p2 = jnp.where(row_c == CH - 2, n_first,
                         jnp.where(row_c == CH - 1, n_second, pltpu.roll(cur, CH - 2, axis=0)))
        r = cb_ref[...] + u_m1 * cw_ref[0:1, :]
        r = r + cur * cw_ref[1:2, :]
        r = r + u_p1 * cw_ref[2:3, :]
        r = r + u_p2 * cw_ref[3:4, :]
        gts = jnp.dot(r.astype(BF16), wg_ref[0].astype(BF16), preferred_element_type=F32) + bg_ref[0]
        half_r = 0.5 * r
        for d, (a_ref, b_ref) in enumerate(((a0_ref, b0_ref), (a1_ref, b1_ref))):
            t_rec = jnp.tanh(gts[:, (2 * d) * W:(2 * d + 1) * W])
            t_in = jnp.tanh(gts[:, (2 * d + 1) * W:(2 * d + 2) * W])
            kd = kexp[d:d + 1, :]
            a = jnp.exp2(kd * t_rec + kd)
            a_ref[pl.ds(c0, CH), :] = a
            b_ref[pl.ds(c0, CH), :] = jnp.sqrt(jnp.maximum(1.0 - a * a, 0.0)) * (half_r * t_in + half_r)
        return carry

    lax.fori_loop(0, n_chunks, coeff_chunk, 0)

    row8 = lax.broadcasted_iota(jnp.int32, (SUBLANES, W), 0)
    n_groups = T // SUBLANES
    ctx_groups = ctx_len // SUBLANES

    def scan_step(k, carry):
        hf, hb = carry
        gf = pl.multiple_of(k * SUBLANES, SUBLANES)
        a, b = _group_scan(a0_ref[pl.ds(gf, SUBLANES), :], b0_ref[pl.ds(gf, SUBLANES), :], row8, False)
        h = a * hf + b
        a0_ref[pl.ds(gf, SUBLANES), :] = h
        hf = jnp.broadcast_to(h[SUBLANES - 1:SUBLANES, :], h.shape)
        g = jnp.where(k < ctx_groups, ctx_groups - 1 - k, n_groups - 1 - (k - ctx_groups))
        gb = pl.multiple_of(g * SUBLANES, SUBLANES)
        a, b = _group_scan(a1_ref[pl.ds(gb, SUBLANES), :], b1_ref[pl.ds(gb, SUBLANES), :], row8, True)
        h = a * hb + b
        a1_ref[pl.ds(gb, SUBLANES), :] = h
        hb = jnp.broadcast_to(h[0:1, :], h.shape)
        return hf, hb

    zero = jnp.zeros((SUBLANES, W), F32)
    lax.fori_loop(0, n_groups, scan_step, (zero, zero), unroll=4)

    def out_chunk(c, carry):
        c0 = pl.multiple_of(c * CH, CH)
        y = a0_ref[pl.ds(c0, CH), :] + a1_ref[pl.ds(c0, CH), :]
        o_ref[pl.ds(c0, CH), :] = (y * _gelu_tanh(ug_ref[pl.ds(c0, CH), :])).astype(o_ref.dtype)
        return carry

    lax.fori_loop(0, n_chunks, out_chunk, 0)


def _lru_mix(u, conv_w, conv_b, gate_w, gate_b, lam, *, nbatch, ctx_len):
    M, W2 = u.shape
    W = W2 // 2
    T = M // nbatch
    nblk = W // LRU_BLOCK_W
    bw = LRU_BLOCK_W
    wg = 0.5 * jnp.transpose(gate_w, (2, 3, 0, 1, 4)).reshape(nblk, bw, 4 * bw)
    bg = 0.5 * jnp.transpose(gate_b.reshape(2, 2, nblk, bw), (2, 0, 1, 3)).reshape(nblk, 1, 4 * bw)
    return pl.pallas_call(
        functools.partial(_lru_kernel, ctx_len=ctx_len),
        out_shape=jax.ShapeDtypeStruct((M, W), BF16),
        grid=(nbatch, nblk),
        in_specs=[
            pl.BlockSpec((T, bw), lambda b, n: (b, n)),
            pl.BlockSpec((T, bw), lambda b, n: (b, nblk + n)),
            pl.BlockSpec((CONV_W, bw), lambda b, n: (0, n)),
            pl.BlockSpec((1, bw), lambda b, n: (0, n)),
            pl.BlockSpec((1, bw, 4 * bw), lambda b, n: (n, 0, 0)),
            pl.BlockSpec((1, 1, 4 * bw), lambda b, n: (n, 0, 0)),
            pl.BlockSpec((2, bw), lambda b, n: (0, n)),
        ],
        out_specs=pl.BlockSpec((T, bw), lambda b, n: (b, n)),
        scratch_shapes=[pltpu.VMEM((T, bw), F32) for _ in range(4)],
        compiler_params=_cparams(("parallel", "parallel"), VMEM_LIMIT),
        name="lru_mix",
    )(u, u, conv_w, conv_b.reshape(1, W), wg, bg, lam)


def _attn_kernel(sink_ref, q_ref, k_ref, v_ref, o_ref, k2_ref, v2_ref, p_ref, *, ctx_len, n_kv):
    QB = q_ref.shape[0]
    T = k_ref.shape[0]
    n_lat = (T - ctx_len) // QB
    ctx_blocks = ctx_len // QB
    band = 3 * QB
    pairs = GROUP // 2
    qb = pl.program_id(1)
    j = qb - ctx_blocks
    start = jnp.clip(j - 1, 0, n_lat - 3)
    off = pl.multiple_of(ctx_len + start * QB, QB)
    lo_half = lax.broadcasted_iota(jnp.int32, (1, LANES), 1) < HEAD_DIM
    nt = (((1,), (1,)), ((), ()))

    def attend(with_band):
        nk = ctx_len + (band if with_band else 0)
        if with_band:
            col = lax.broadcasted_iota(jnp.int32, (QB, nk), 1)
            qpos = j * QB + lax.broadcasted_iota(jnp.int32, (QB, nk), 0)
            kpos = start * QB + (col - ctx_len)
            valid = jnp.logical_or(col < ctx_len, jnp.abs(qpos - kpos) <= WINDOW)
        for h in range(n_kv):
            slab = slice((h // 2) * LANES, (h // 2 + 1) * LANES)

            def two_blocks(x):
                swapped = jnp.concatenate([x[:, HEAD_DIM:], x[:, :HEAD_DIM]], axis=1)
                own, other = (x, swapped) if h % 2 == 0 else (swapped, x)
                zero = jnp.zeros_like(x)
                return jnp.where(lo_half, own, zero), jnp.where(lo_half, zero, other)

            for dst, src in ((k2_ref, k_ref), (v2_ref, v_ref)):
                top_c, bot_c = two_blocks(src[0:ctx_len, slab])
                dst[0:ctx_len, :] = top_c
                dst[nk:nk + ctx_len, :] = bot_c
                if with_band:
                    top_b, bot_b = two_blocks(src[pl.ds(off, band), slab])
                    dst[ctx_len:nk, :] = top_b
                    dst[nk + ctx_len:2 * nk, :] = bot_b
            c_q = h * GROUP * HEAD_DIM
            qs = jnp.concatenate([q_ref[:, c_q + p * LANES:c_q + (p + 1) * LANES] for p in range(pairs)], axis=0)
            s = lax.dot_general(qs, k2_ref[0:2 * nk, :], nt, preferred_element_type=F32)
            invs = []
            for p in range(pairs):
                rows = slice(p * QB, (p + 1) * QB)
                inv_pair = []
                for half in range(2):
                    c0 = half * nk
                    sink = sink_ref[h * GROUP + 2 * p + half] * LOG2_E
                    s_h = s[rows, c0:c0 + nk]
                    if with_band:
                        s_h = jnp.where(valid, s_h, NEG_INF)
                    m = jnp.maximum(jnp.max(s_h, axis=-1, keepdims=True), sink)
                    p_h = jnp.exp2(s_h - m)
                    denom = jnp.sum(p_h, axis=-1, keepdims=True) + jnp.exp2(sink - m)
                    p_ref[rows, c0:c0 + nk] = p_h.astype(BF16)
                    inv_pair.append(1.0 / denom)
                invs.append(inv_pair)
            o = jnp.dot(p_ref[:, 0:2 * nk], v2_ref[0:2 * nk, :], preferred_element_type=F32)
            for p in range(pairs):
                inv = jnp.where(lo_half, invs[p][0], invs[p][1])
                o_ref[:, c_q + p * LANES:c_q + (p + 1) * LANES] = (o[p * QB:(p + 1) * QB] * inv).astype(o_ref.dtype)

    @pl.when(qb >= ctx_blocks)
    def _():
        attend(True)

    @pl.when(qb < ctx_blocks)
    def _():
        attend(False)


def _attention(qkv, sinks, *, nbatch, ctx_len, q_dim, kv_dim):
    M = qkv.shape[0]
    T = M // nbatch
    nqb = T // Q_BLOCK
    kcol = q_dim // kv_dim
    return pl.pallas_call(
        functools.partial(_attn_kernel, ctx_len=ctx_len, n_kv=kv_dim // HEAD_DIM),
        out_shape=jax.ShapeDtypeStruct((M, q_dim), BF16),
        grid_spec=pltpu.PrefetchScalarGridSpec(
            num_scalar_prefetch=1,
            grid=(nbatch, nqb),
            in_specs=[
                pl.BlockSpec((Q_BLOCK, q_dim), lambda b, q, s: (b * nqb + q, 0)),
                pl.BlockSpec((T, kv_dim), lambda b, q, s: (b, kcol)),
                pl.BlockSpec((T, kv_dim), lambda b, q, s: (b, kcol + 1)),
            ],
            out_specs=pl.BlockSpec((Q_BLOCK, q_dim), lambda b, q, s: (b * nqb + q, 0)),
            scratch_shapes=[
                pltpu.VMEM((2 * (ctx_len + 3 * Q_BLOCK), LANES), BF16),
                pltpu.VMEM((2 * (ctx_len + 3 * Q_BLOCK), LANES), BF16),
                pltpu.VMEM((GROUP // 2 * Q_BLOCK, 2 * (ctx_len + 3 * Q_BLOCK)), BF16),
            ],
        ),
        compiler_params=_cparams(("parallel", "parallel"), VMEM_LIMIT),
        name="swa_attention",
    )(sinks, qkv, qkv, qkv)


def _split_bf16(x):
    hi = x.astype(BF16)
    lo = (x - hi.astype(F32)).astype(BF16)
    return hi, lo


def _router_kernel(x_ref, mod_ref, wr_ref, br_ref, e_ref, g_ref, r_ref, cnt_ref, carry_ref):
    _route_tile(x_ref[...], mod_ref, wr_ref, br_ref, e_ref, g_ref, r_ref, cnt_ref, carry_ref)


def _route_tile(x, mod_ref, wr_ref, br_ref, e_ref, g_ref, r_ref, cnt_ref, carry_ref):
    i = pl.program_id(0)

    @pl.when(i == 0)
    def _():
        carry_ref[...] = jnp.zeros_like(carry_ref)

    shift = mod_ref[0:1, :]
    scale = mod_ref[1:2, :]
    a = x * (1.0 + scale) + shift
    a_hi, a_lo = _split_bf16(a)
    ne = br_ref.shape[1]
    hi_terms = jnp.dot(a_hi, wr_ref[...], preferred_element_type=F32)
    lo_term = jnp.dot(a_lo, wr_ref[:, 0:ne], preferred_element_type=F32)
    logits = (hi_terms[:, 0:ne] + (lo_term + hi_terms[:, ne:2 * ne])) + br_ref[...]
    tm = logits.shape[0]
    lane_e = lax.broadcasted_iota(jnp.int32, (tm, ne), 1)
    lane_o = lax.broadcasted_iota(jnp.int32, (tm, LANES), 1)
    vals, idxs, hots = [], [], []
    cur = logits
    for k in range(TOP_K):
        m = jnp.max(cur, axis=-1, keepdims=True)
        idx = jnp.min(jnp.where(cur == m, lane_e, ne), axis=-1, keepdims=True)
        hot = lane_e == idx
        vals.append(m)
        idxs.append(idx)
        hots.append(hot)
        cur = jnp.where(hot, -jnp.inf, cur)
    exps = [jnp.exp(v - vals[0]) for v in vals]
    inv = 1.0 / (exps[0] + exps[1] + exps[2] + exps[3])
    msum = jnp.zeros((tm, ne), F32)
    for hot in hots:
        msum = msum + hot.astype(F32)
    rr = lax.broadcasted_iota(jnp.int32, (tm, tm), 0)
    cc = lax.broadcasted_iota(jnp.int32, (tm, tm), 1)
    tri = (cc < rr).astype(BF16)
    before = jnp.dot(tri, msum.astype(BF16), preferred_element_type=F32) + carry_ref[...]
    e_out = jnp.zeros((tm, LANES), jnp.int32)
    g_out = jnp.zeros((tm, LANES), F32)
    r_out = jnp.zeros((tm, LANES), jnp.int32)
    for k in range(TOP_K):
        rank = jnp.sum(jnp.where(hots[k], before, 0.0), axis=-1, keepdims=True).astype(jnp.int32)
        e_out = jnp.where(lane_o == k, idxs[k], e_out)
        g_out = jnp.where(lane_o == k, exps[k] * inv, g_out)
        r_out = jnp.where(lane_o == k, rank, r_out)
    e_ref[...] = e_out
    g_ref[...] = g_out
    r_ref[...] = r_out
    carry_ref[...] = carry_ref[...] + jnp.sum(msum, axis=0, keepdims=True)
    cnt_ref[...] = carry_ref[...].astype(jnp.int32)


def _router(x, mods, ls, w_r, b_r, *, geom, latent_only):
    D = x.shape[1]
    ne = w_r.shape[1]
    n_tiles, row_tile, mod_row = _token_tiles(geom, latent_only)
    M = n_tiles * TM
    return pl.pallas_call(
        _router_kernel,
        out_shape=(
            jax.ShapeDtypeStruct((M, LANES), jnp.int32),
            jax.ShapeDtypeStruct((M, LANES), F32),
            jax.ShapeDtypeStruct((M, LANES), jnp.int32),
            jax.ShapeDtypeStruct((1, ne), jnp.int32),
        ),
        grid=(n_tiles,),
        in_specs=[
            pl.BlockSpec((TM, D), lambda i: (row_tile(i), 0)),
            pl.BlockSpec((None, None, 3, D), lambda i: (ls, mod_row(i), 0, 0)),
            pl.BlockSpec((D, 2 * ne), lambda i: (0, 0)),
            pl.BlockSpec((1, ne), lambda i: (0, 0)),
        ],
        out_specs=(
            pl.BlockSpec((TM, LANES), lambda i: (i, 0)),
            pl.BlockSpec((TM, LANES), lambda i: (i, 0)),
            pl.BlockSpec((TM, LANES), lambda i: (i, 0)),
            pl.BlockSpec((1, ne), lambda i: (0, 0)),
        ),
        scratch_shapes=[pltpu.VMEM((1, ne), F32)],
        compiler_params=_cparams(("arbitrary",), VMEM_LIMIT),
        name="moe_router",
    )(x, mods, jnp.concatenate(_split_bf16(w_r), axis=1), b_r.reshape(1, ne))


def _expert_kernel(blk_e_ref, slot_ref, first_ref, next_e_ref, nused_ref, xs_ref, bgu_ref, bdn_ref,
                   wgu_hbm, wdn_hbm, o_ref, wgu_buf, wdn_buf, x2d_ref, sem_gu, sem_dn, *, layer):
    b = pl.program_id(0)
    de, d = wdn_buf.shape[1:]
    half = d // 2

    def fetch(expert, slot):
        cps = []
        for hbm, buf, sem, chunks in ((wgu_hbm, wgu_buf, sem_gu, WEIGHT_DMA_CHUNKS),
                                      (wdn_hbm, wdn_buf, sem_dn, WEIGHT_DMA_CHUNKS // 2)):
            rows = buf.shape[1] // chunks
            for c in range(chunks):
                rs = pl.ds(c * rows, rows)
                cps.append(pltpu.make_async_copy(hbm.at[layer, expert, rs], buf.at[slot, rs], sem.at[slot]))
        return cps

    @pl.when(b < nused_ref[0])
    def _():
        slot = slot_ref[b]

        @pl.when(b == 0)
        def _():
            for c, cp in enumerate(fetch(blk_e_ref[0], 0)):
                cp.start(priority=c % 2)

        @pl.when(first_ref[b] == 1)
        def _():
            @pl.when(next_e_ref[b] >= 0)
            def _():
                for c, cp in enumerate(fetch(next_e_ref[b], 1 - slot)):
                    cp.start(priority=c % 2)
            for cp in fetch(blk_e_ref[b], slot):
                cp.wait()

        x2d_ref[...] = xs_ref[...].reshape(x2d_ref.shape)
        x_lo, x_hi = (v.astype(BF16) for v in _unpack_bf16_pair(x2d_ref[...]))
        hs = []
        for c in range(2 * de // 256):
            sl = slice(c * 256, (c + 1) * 256)
            acc = jnp.dot(x_lo, wgu_buf[slot, 0:half, sl].astype(BF16), preferred_element_type=F32)
            acc = acc + jnp.dot(x_hi, wgu_buf[slot, half:d, sl].astype(BF16), preferred_element_type=F32)
            hs.append(acc + bgu_ref[:, sl])
        h = jnp.concatenate(hs, axis=1)
        g = jnp.minimum(h[:, :de], SWIGLU_LIMIT)
        u = jnp.clip(h[:, de:], -SWIGLU_LIMIT, SWIGLU_LIMIT)
        act = ((u + 1.0) * (g * _sigmoid(SWIGLU_ALPHA * g))).astype(BF16)
        ys = []
        for c in range(d // 512):
            sl = slice(c * 512, (c + 1) * 512)
            ys.append(jnp.dot(act, wdn_buf[slot, :, sl].astype(BF16), preferred_element_type=F32) + bdn_ref[:, sl])
        nh = len(ys) // 2
        for c in range(nh):
            packed = _pack_bf16_pair(ys[c], ys[c + nh])
            o_ref[:, :, c * 512:(c + 1) * 512] = packed.reshape(packed.shape[0], 1, 512)

    @pl.when(b >= nused_ref[0])
    def _():
        o_ref[...] = jnp.zeros_like(o_ref)


def _expert_ffn(xs, blk_tabs, n_used, layer, w_gu, b_gu, w_dn, b_dn):
    P = xs.shape[0]
    _, ne, D, de2 = w_gu.shape
    de = de2 // 2
    n_blk = P // EXPERT_BM
    blk_e, slot, first, next_e = blk_tabs
    return pl.pallas_call(
        functools.partial(_expert_kernel, layer=layer),
        out_shape=jax.ShapeDtypeStruct((P, 1, D // 2), jnp.uint32),
        grid_spec=pltpu.PrefetchScalarGridSpec(
            num_scalar_prefetch=5,
            grid=(n_blk,),
            in_specs=[
                pl.BlockSpec((EXPERT_BM, 1, D // 2), lambda b, e, s, f, x, n: (jnp.minimum(b, n[0] - 1), 0, 0)),
                pl.BlockSpec((None, None, 1, de2), lambda b, e, s, f, x, n: (layer, e[b], 0, 0)),
                pl.BlockSpec((None, None, 1, D), lambda b, e, s, f, x, n: (layer, e[b], 0, 0)),
                pl.BlockSpec(memory_space=pl.ANY),
                pl.BlockSpec(memory_space=pl.ANY),
            ],
            out_specs=pl.BlockSpec((EXPERT_BM, 1, D // 2), lambda b, e, s, f, x, n: (b, 0, 0)),
            scratch_shapes=[
                pltpu.VMEM((2, D, de2), F32),
                pltpu.VMEM((2, de, D), F32),
                pltpu.VMEM((EXPERT_BM, D // 2), jnp.uint32),
                pltpu.SemaphoreType.DMA((2,)),
                pltpu.SemaphoreType.DMA((2,)),
            ],
        ),
        compiler_params=_cparams(("arbitrary",), VMEM_LIMIT),
        name="expert_ffn",
    )(blk_e, slot, first, next_e, n_used, xs, b_gu.reshape(b_gu.shape[0], ne, 1, de2),
      b_dn.reshape(b_dn.shape[0], ne, 1, D), w_gu, w_dn)


def _moe_layer(x, mods, ls, layer, w_r, b_r, w_gu, b_gu, w_dn, b_dn, ln_g, ln_b, *, alpha, geom, latent_only,
               routed=None):
    ne = w_r.shape[1]
    bm = EXPERT_BM
    if routed is None:
        routed = _router(x, mods, ls, w_r, b_r, geom=geom, latent_only=latent_only)
    e128, g128, r128, counts = routed
    M = e128.shape[0]
    top_e = e128[:, :TOP_K]
    rank = r128[:, :TOP_K]
    counts = counts[0]
    padded = ((counts + bm - 1) // bm) * bm
    ends = jnp.cumsum(padded)
    pstart = ends - padded
    experts = jnp.arange(ne, dtype=jnp.int32)
    dest = jnp.sum(jnp.where(top_e[..., None] == experts, pstart, 0), axis=-1) + rank
    dest_tiles = dest.astype(jnp.int32).reshape(M // TM, TM * TOP_K)
    A = M * TOP_K
    P = (-(-A // bm)) * bm + ne * bm
    n_blk = P // bm
    n_used = (ends[-1] // bm).astype(jnp.int32).reshape(1)
    blk = jnp.arange(n_blk, dtype=jnp.int32)
    blk_row = jnp.minimum(blk, n_used[0] - 1) * bm
    blk_e = jnp.minimum(jnp.sum(ends[None, :] <= blk_row[:, None], axis=1), ne - 1).astype(jnp.int32)
    first = jnp.logical_and(blk < n_used[0],
                            jnp.concatenate([jnp.ones((1,), bool), blk_e[1:] != blk_e[:-1]])).astype(jnp.int32)
    slot = ((jnp.cumsum(first) - 1) % 2).astype(jnp.int32)
    later = jnp.logical_and(experts[None, :] > experts[:, None], (padded > 0)[None, :])
    next_nonempty = jnp.min(jnp.where(later, experts[None, :], ne), axis=1)
    next_nonempty = jnp.where(next_nonempty == ne, -1, next_nonempty)
    next_e = jnp.sum(jnp.where(blk_e[:, None] == experts[None, :], next_nonempty[None, :], 0), axis=1).astype(jnp.int32)
    pad_lo = jnp.where(padded > 0, ends - bm, -1).astype(jnp.int32)
    xs = _dispatch(pad_lo, n_used, dest_tiles, x, mods, ls, P, geom=geom, latent_only=latent_only)
    ys = _expert_ffn(xs, (blk_e, slot, first, next_e), n_used, layer, w_gu, b_gu, w_dn, b_dn)
    return _combine_ln(dest_tiles, ys, g128, x, mods, ls, ln_g, ln_b, alpha=alpha, geom=geom,
                       latent_only=latent_only)


def _rope_tables(seq, ctx_len):
    rows = seq // GRID_W
    row_pos = jnp.repeat(jnp.arange(rows), GRID_W).astype(F32)
    col_pos = jnp.tile(jnp.arange(GRID_W), rows).astype(F32)
    n = HEAD_DIM // 2
    freqs = ROPE_BASE ** (-jnp.arange(0, n, 2, dtype=F32) / n)
    ang_r = row_pos[:, None] * freqs[None, :]
    ang_c = col_pos[:, None] * freqs[None, :]
    cos = jnp.concatenate([jnp.cos(ang_r), jnp.cos(ang_r), jnp.cos(ang_c), jnp.cos(ang_c)], axis=1)
    sin = jnp.concatenate([-jnp.sin(ang_r), jnp.sin(ang_r), -jnp.sin(ang_c), jnp.sin(ang_c)], axis=1)
    cos = jnp.concatenate([jnp.ones((ctx_len, HEAD_DIM), F32), cos], axis=0)
    sin = jnp.concatenate([jnp.zeros((ctx_len, HEAD_DIM), F32), sin], axis=0)
    cos = jnp.tile(cos, (1, LANES // HEAD_DIM))
    sin = jnp.tile(sin, (1, LANES // HEAD_DIM))
    scale = HEAD_DIM ** -0.5 * LOG2_E
    return jnp.stack([cos * scale, sin * scale, cos, sin], axis=0)


def kernel(x, c, ctx, c_ctx, ada_w, ada_b, ln_g, ln_b, lru_w_in, lru_conv_w, lru_conv_b, lru_gate_w, lru_gate_b, lru_lambda, lru_w_out, attn_w_qkv, attn_sinks, attn_w_o, router_w, router_b, moe_w_gu, moe_b_gu, moe_w_down, moe_b_down):
    B, S, D = x.shape
    C = ctx.shape[1]
    depth = ada_w.shape[0]
    T = C + S
    assert C % TM == 0 and S % TM == 0 and C % Q_BLOCK == 0
    geom = (T // TM, C // TM, B)
    alpha = (2.0 * depth) ** 0.25
    kv_dim = (attn_w_qkv.shape[2] - D) // 2

    act = jax.nn.silu(jnp.concatenate([c, c_ctx[None, :]], axis=0))
    mods = _ada_params(act, ada_w.reshape(depth * 2, D, 3 * D), ada_b.reshape(depth * 2, 3 * D))
    mods = mods.reshape(depth * 2, SUBLANES, 3, D)
    tab = _rope_tables(S, C)

    z = jnp.concatenate([ctx, x], axis=1).reshape(B * T, D)
    for i in range(depth):
        j = i // 2
        if i % 2 == 0:
            u = _mod_matmul(z, mods, 2 * i, lru_w_in, j, tn=D, out_dtype=F32, geom=geom)
            y = _lru_mix(u, lru_conv_w[j], lru_conv_b[j], lru_gate_w[j], lru_gate_b[j], lru_lambda[j],
                         nbatch=B, ctx_len=C)
            w_out = lru_w_out
        else:
            qkv = _mod_matmul(z, mods, 2 * i, attn_w_qkv, j, tn=D + 2 * kv_dim, out_dtype=BF16,
                              geom=geom, rope=(tab, (D, kv_dim)))
            y = _attention(qkv, attn_sinks[j], nbatch=B, ctx_len=C, q_dim=D, kv_dim=kv_dim)
            w_out = attn_w_o
        last = i == depth - 1
        if last:
            z = _proj_ln(y, w_out, j, z, mods, 2 * i, ln_g[i, 0], ln_b[i, 0], alpha=alpha, geom=geom)
            routed = None
        else:
            z, *routed = _proj_ln_route(y, w_out, j, z, mods, 2 * i, ln_g[i, 0], ln_b[i, 0],
                                        router_w[i], router_b[i], alpha=alpha, geom=geom)
        z = _moe_layer(z, mods, 2 * i + 1, i, router_w[i], router_b[i], moe_w_gu, moe_b_gu,
                       moe_w_down, moe_b_down, ln_g[i, 1], ln_b[i, 1], alpha=alpha, geom=geom,
                       latent_only=last, routed=routed)
    return z.reshape(B, S, D)
```

```python
import functools
import math

import jax
import jax.numpy as jnp
from jax import lax
from jax.experimental import pallas as pl
from jax.experimental.pallas import tpu as pltpu

GRID_W = 64
LN_EPS = 1e-5
LRU_BLOCK_W = 256
CONV_W = 4
LRU_C = 8.0
HEAD_DIM = 64
GROUP = 8
WINDOW = 128
Q_BLOCK = 128
ROPE_BASE = 10000.0
NEG_INF = -1e30
TOP_K = 4
SWIGLU_LIMIT = 7.0
SWIGLU_ALPHA = 1.702
LOG2_E = math.log2(math.e)

LANES = 128
SUBLANES = 8
TM = 256
EXPERT_BM = 256
WEIGHT_DMA_CHUNKS = 4
V7X_VMEM_BYTES = 64 * 1024 * 1024
VMEM_LIMIT = V7X_VMEM_BYTES * 7 // 8

BF16 = jnp.bfloat16
F32 = jnp.float32


def _cparams(sem, vmem=None):
    return pltpu.CompilerParams(dimension_semantics=sem, vmem_limit_bytes=vmem)


def _ada_kernel(a_ref, w_ref, b_ref, o_ref):
    acc = jnp.dot(a_ref[...], w_ref[0].astype(BF16), preferred_element_type=F32)
    o_ref[0] = acc[0:SUBLANES] + acc[SUBLANES:2 * SUBLANES] + b_ref[0]


def _ada_params(act, w, bias):
    R, D = act.shape
    L, _, N = w.shape
    tn = 1024
    a8 = jnp.zeros((SUBLANES, D), F32).at[:R].set(act)
    hi = a8.astype(BF16)
    lo = (a8 - hi.astype(F32)).astype(BF16)
    a16 = jnp.concatenate([hi, lo], axis=0)
    return pl.pallas_call(
        _ada_kernel,
        out_shape=jax.ShapeDtypeStruct((L, SUBLANES, N), F32),
        grid=(L, N // tn),
        in_specs=[
            pl.BlockSpec((2 * SUBLANES, D), lambda l, n: (0, 0)),
            pl.BlockSpec((1, D, tn), lambda l, n: (l, 0, n)),
            pl.BlockSpec((1, 1, tn), lambda l, n: (l, 0, n)),
        ],
        out_specs=pl.BlockSpec((1, SUBLANES, tn), lambda l, n: (l, 0, n)),
        compiler_params=_cparams(("parallel", "parallel"), VMEM_LIMIT),
        name="ada_params",
    )(a16, w, bias.reshape(L, 1, N))


def _mod_row(i, tiles_per_batch, ctx_tiles, nbatch):
    return jnp.where(i % tiles_per_batch < ctx_tiles, nbatch, i // tiles_per_batch)


def _token_tiles(geom, latent_only):
    tpb, ctx_tiles, nbatch = geom
    if latent_only:
        lpb = tpb - ctx_tiles
        return lpb * nbatch, (lambda i: (i // lpb) * tpb + ctx_tiles + i % lpb), (lambda i: i // lpb)
    return tpb * nbatch, (lambda i: i), (lambda i: _mod_row(i, tpb, ctx_tiles, nbatch))


def _stream_specs(stream, geom, tile_of):
    tpb, ctx_tiles, nbatch = geom
    if not isinstance(stream, tuple):
        return [pl.BlockSpec((TM, stream.shape[1]), lambda *g: (tile_of(*g), 0))], [stream]
    ctx_rows, lat_rows = stream
    lpb = tpb - ctx_tiles

    def ctx_tile(*g):
        i = tile_of(*g)
        return ((i // tpb) * ctx_tiles + jnp.minimum(i % tpb, ctx_tiles - 1), 0)

    def lat_tile(*g):
        i = tile_of(*g)
        return ((i // tpb) * lpb + jnp.maximum(i % tpb - ctx_tiles, 0), 0)

    D = ctx_rows.shape[1]
    return [pl.BlockSpec((TM, D), ctx_tile), pl.BlockSpec((TM, D), lat_tile)], [ctx_rows, lat_rows]


def _stream_tile(x_refs, tile, geom, cols=slice(None)):
    if len(x_refs) == 1:
        return x_refs[0][:, cols]
    tpb, ctx_tiles, _ = geom
    return jnp.where(tile % tpb < ctx_tiles, x_refs[0][:, cols], x_refs[1][:, cols])


def _rope_chunk(y, cos, sin_signed, first_half):
    rot = jnp.where(first_half, pltpu.roll(y, LANES - 16, axis=1), pltpu.roll(y, 16, axis=1))
    return y * cos + rot * sin_signed


def _modmm_kernel(*refs, rope_cols, n_x, geom):
    x_refs, (mod_ref, w_ref, *rest) = refs[:n_x], refs[n_x:]
    if rope_cols is None:
        (o_ref,) = rest
    else:
        tab_ref, o_ref = rest
    shift = mod_ref[0:1, :]
    scale = mod_ref[1:2, :]
    a = (_stream_tile(x_refs, pl.program_id(1), geom) * (1.0 + scale) + shift).astype(BF16)
    tn = o_ref.shape[1]
    ch = 512 if tn % 512 == 0 else 256
    for c in range(tn // ch):
        acc = jnp.dot(a, w_ref[:, c * ch:(c + 1) * ch].astype(BF16), preferred_element_type=F32)
        if rope_cols is None:
            o_ref[:, c * ch:(c + 1) * ch] = acc.astype(o_ref.dtype)
        else:
            q_cols, k_cols = rope_cols
            lane = lax.broadcasted_iota(jnp.int32, (acc.shape[0], LANES), 1)
            first_half = (lane % 32) < 16
            for s in range(ch // LANES):
                col = c * ch + s * LANES
                y = acc[:, s * LANES:(s + 1) * LANES]
                if col < q_cols:
                    y = _rope_chunk(y, tab_ref[0], tab_ref[1], first_half)
                elif col < q_cols + k_cols:
                    y = _rope_chunk(y, tab_ref[2], tab_ref[3], first_half)
                o_ref[:, col:col + LANES] = y.astype(o_ref.dtype)


def _mod_matmul(x, mods, ls, w_stack, layer, *, tn, out_dtype, geom, rope=None):
    K, N = w_stack.shape[1:]
    tpb, ctx_tiles, nbatch = geom
    M = tpb * nbatch * TM
    x_specs, x_args = _stream_specs(x, geom, lambda j, i: i)
    in_specs = x_specs + [
        pl.BlockSpec((None, None, 3, K), lambda j, i: (ls, _mod_row(i, tpb, ctx_tiles, nbatch), 0, 0)),
        pl.BlockSpec((None, K, tn), lambda j, i: (layer, 0, j)),
    ]
    args = x_args + [mods, w_stack]
    rope_cols = None
    if rope is not None:
        tab, rope_cols = rope
        in_specs.append(pl.BlockSpec((4, TM, LANES), lambda j, i: (0, i % tpb, 0)))
        args.append(tab)
    return pl.pallas_call(
        functools.partial(_modmm_kernel, rope_cols=rope_cols, n_x=len(x_args), geom=geom),
        out_shape=jax.ShapeDtypeStruct((M, N), out_dtype),
        grid=(N // tn, M // TM),
        in_specs=in_specs,
        out_specs=pl.BlockSpec((TM, tn), lambda j, i: (i, j)),
        compiler_params=_cparams(("parallel", "parallel"), VMEM_LIMIT),
        name="mod_matmul",
    )(*args)


def _layer_norm_rows(z, g, b):
    mu = jnp.mean(z, axis=-1, keepdims=True)
    zc = z - mu
    var = jnp.mean(zc * zc, axis=-1, keepdims=True)
    return zc * lax.rsqrt(var + LN_EPS) * g + b


def _proj_ln_kernel(y_ref, w_ref, *refs, alpha, n_x, geom):
    x_refs, (mod_ref, g_ref, b_ref, o_ref, z_ref) = refs[:n_x], refs[n_x:]
    gate = mod_ref[2:3, :]
    y = y_ref[...]
    n = o_ref.shape[1]
    ch = 512
    for c in range(n // ch):
        sl = slice(c * ch, (c + 1) * ch)
        acc = jnp.dot(y, w_ref[:, sl].astype(BF16), preferred_element_type=F32)
        z_ref[:, sl] = alpha * _stream_tile(x_refs, pl.program_id(0), geom, sl) + gate[:, sl] * acc
    o_ref[...] = _layer_norm_rows(z_ref[...], g_ref[...], b_ref[...])


def _proj_ln(y, w_stack, j, x, mods, ls, ln_g, ln_b, *, alpha, geom):
    M, K = y.shape
    D = w_stack.shape[2]
    tpb, ctx_tiles, nbatch = geom
    x_specs, x_args = _stream_specs(x, geom, lambda i: i)
    return pl.pallas_call(
        functools.partial(_proj_ln_kernel, alpha=alpha, n_x=len(x_args), geom=geom),
        out_shape=jax.ShapeDtypeStruct((M, D), F32),
        grid=(M // TM,),
        in_specs=[
            pl.BlockSpec((TM, K), lambda i: (i, 0)),
            pl.BlockSpec((None, K, D), lambda i: (j, 0, 0)),
            *x_specs,
            pl.BlockSpec((None, None, 3, D), lambda i: (ls, _mod_row(i, tpb, ctx_tiles, nbatch), 0, 0)),
            pl.BlockSpec((1, D), lambda i: (0, 0)),
            pl.BlockSpec((1, D), lambda i: (0, 0)),
        ],
        out_specs=pl.BlockSpec((TM, D), lambda i: (i, 0)),
        scratch_shapes=[pltpu.VMEM((TM, D), F32)],
        compiler_params=_cparams(("parallel",), VMEM_LIMIT),
        name="proj_ln",
    )(y, w_stack, *x_args, mods, ln_g.reshape(1, D), ln_b.reshape(1, D))


def _pack_bf16_pair(lo, hi):
    lo_w = lax.bitcast_convert_type(lo.astype(BF16).astype(F32), jnp.uint32) >> 16
    hi_w = lax.bitcast_convert_type(hi.astype(BF16).astype(F32), jnp.uint32) & jnp.uint32(0xFFFF0000)
    return lo_w | hi_w


def _unpack_bf16_pair(w):
    lo = lax.bitcast_convert_type(w << 16, F32)
    hi = lax.bitcast_convert_type(w & jnp.uint32(0xFFFF0000), F32)
    return lo, hi


def _index_copy(idx_hbm, idx_smem, sems, tile):
    return pltpu.make_async_copy(idx_hbm.at[tile], idx_smem.at[tile % 2], sems.at[tile % 2])


def _combine_ln_kernel(dest_hbm, ys_hbm, gate4_ref, x_ref, mod_ref, g_ref, b_ref, o_ref,
                       idx_smem, buf_ref, rows2d_ref, idx_sems, row_sems, *, alpha):
    i = pl.program_id(0)
    n = pl.num_programs(0)

    def gather_rows(tile):
        slot = tile % 2

        def issue(r, carry):
            for k in range(TOP_K):
                row = idx_smem[slot, r * TOP_K + k]
                pltpu.make_async_copy(ys_hbm.at[row], buf_ref.at[slot, k, r],
                                      row_sems.at[slot]).start(priority=k % 2)
            return carry

        lax.fori_loop(0, TM, issue, 0, unroll=8)

    @pl.when(i == 0)
    def _():
        cp = _index_copy(dest_hbm, idx_smem, idx_sems, 0)
        cp.start()
        cp.wait()
        gather_rows(0)

        @pl.when(n > 1)
        def _():
            _index_copy(dest_hbm, idx_smem, idx_sems, 1).start()

    @pl.when(i + 1 < n)
    def _():
        _index_copy(dest_hbm, idx_smem, idx_sems, i + 1).wait()
        gather_rows(i + 1)

        @pl.when(i + 2 < n)
        def _():
            _index_copy(dest_hbm, idx_smem, idx_sems, i + 2).start()

    slot = i % 2
    for k in range(TOP_K):
        pltpu.make_async_copy(ys_hbm.at[pl.ds(0, TM)], buf_ref.at[slot, k], row_sems.at[slot]).wait()
    y_lo = y_hi = None
    for k in range(TOP_K):
        rows2d_ref[...] = buf_ref[slot, k].reshape(rows2d_ref.shape)
        lo, hi = _unpack_bf16_pair(rows2d_ref[...])
        gk = gate4_ref[:, k:k + 1]
        y_lo = lo * gk if y_lo is None else y_lo + lo * gk
        y_hi = hi * gk if y_hi is None else y_hi + hi * gk
    y = jnp.concatenate([y_lo, y_hi], axis=1)
    z = alpha * x_ref[...] + mod_ref[2:3, :] * y
    o_ref[...] = _layer_norm_rows(z, g_ref[...], b_ref[...])


def _combine_ln(dest_tiles, ys, gate4, x, mods, ls, ln_g, ln_b, *, alpha, geom, latent_only):
    D = x.shape[1]
    n_tiles, row_tile, mod_row = _token_tiles(geom, latent_only)
    return pl.pallas_call(
        functools.partial(_combine_ln_kernel, alpha=alpha),
        out_shape=jax.ShapeDtypeStruct((n_tiles * TM, D), F32),
        grid=(n_tiles,),
        in_specs=[
            pl.BlockSpec(memory_space=pl.ANY),
            pl.BlockSpec(memory_space=pl.ANY),
            pl.BlockSpec((TM, LANES), lambda i: (i, 0)),
            pl.BlockSpec((TM, D), lambda i: (row_tile(i), 0)),
            pl.BlockSpec((None, None, 3, D), lambda i: (ls, mod_row(i), 0, 0)),
            pl.BlockSpec((1, D), lambda i: (0, 0)),
            pl.BlockSpec((1, D), lambda i: (0, 0)),
        ],
        out_specs=pl.BlockSpec((TM, D), lambda i: (i, 0)),
        scratch_shapes=[
            pltpu.SMEM((2, TM * TOP_K), jnp.int32),
            pltpu.VMEM((2, TOP_K, TM, 1, D // 2), jnp.uint32),
            pltpu.VMEM((TM, D // 2), jnp.uint32),
            pltpu.SemaphoreType.DMA((2,)),
            pltpu.SemaphoreType.DMA((2,)),
        ],
        compiler_params=_cparams(("arbitrary",), VMEM_LIMIT),
        name="combine_ln",
    )(dest_tiles, ys, gate4, x, mods, ln_g.reshape(1, D), ln_b.reshape(1, D))


def _dispatch_kernel(pad_lo_ref, nused_ref, dest_hbm, x_ref, mod_ref, xs_hbm, idx_smem, a_ref, zero_ref,
                     idx_sems, row_sems, *, bm):
    i = pl.program_id(0)
    n = pl.num_programs(0)
    slot = i % 2

    def wait_rows(s):
        for _ in range(TOP_K):
            pltpu.make_async_copy(a_ref.at[s], xs_hbm.at[pl.ds(0, TM)], row_sems.at[s]).wait()

    @pl.when(i == 0)
    def _():
        _index_copy(dest_hbm, idx_smem, idx_sems, 0).start()
        zero_ref[...] = jnp.zeros_like(zero_ref)
        n_exp = pad_lo_ref.shape[0]
        n_blk = xs_hbm.shape[0] // bm

        def zero_block(row0):
            return pltpu.make_async_copy(zero_ref, xs_hbm.at[pl.ds(pl.multiple_of(row0, bm), bm)],
                                         row_sems.at[0])

        for e in range(n_exp):
            @pl.when(pad_lo_ref[e] >= 0)
            def _():
                zero_block(pad_lo_ref[e]).start()

        def fill_tail(b, carry):
            zero_block(b * bm).start()
            return carry

        lax.fori_loop(nused_ref[0], n_blk, fill_tail, 0)
        for e in range(n_exp):
            @pl.when(pad_lo_ref[e] >= 0)
            def _():
                zero_block(0).wait()

        def wait_tail(b, carry):
            zero_block(0).wait()
            return carry

        lax.fori_loop(nused_ref[0], n_blk, wait_tail, 0)

    @pl.when(i + 1 < n)
    def _():
        _index_copy(dest_hbm, idx_smem, idx_sems, i + 1).start()

    @pl.when(i >= 2)
    def _():
        wait_rows(slot)

    shift = mod_ref[0:1, :]
    scale = mod_ref[1:2, :]
    a = x_ref[...] * (1.0 + scale) + shift
    half = a.shape[1] // 2
    a_ref[slot] = _pack_bf16_pair(a[:, :half], a[:, half:]).reshape(a_ref.shape[1:])
    _index_copy(dest_hbm, idx_smem, idx_sems, i).wait()

    def issue(r, carry):
        for k in range(TOP_K):
            row = idx_smem[slot, r * TOP_K + k]
            pltpu.make_async_copy(a_ref.at[slot, r], xs_hbm.at[row], row_sems.at[slot]).start(priority=k % 2)
        return carry

    lax.fori_loop(0, TM, issue, 0, unroll=8)

    @pl.when(i == n - 1)
    def _():
        wait_rows(slot)

        @pl.when(n > 1)
        def _():
            wait_rows(1 - slot)


def _dispatch(pad_lo, n_used, dest_tiles, x, mods, ls, n_rows, *, geom, latent_only):
    D = x.shape[1]
    n_tiles, row_tile, mod_row = _token_tiles(geom, latent_only)
    bm = EXPERT_BM
    return pl.pallas_call(
        functools.partial(_dispatch_kernel, bm=bm),
        out_shape=jax.ShapeDtypeStruct((n_rows, 1, D // 2), jnp.uint32),
        grid_spec=pltpu.PrefetchScalarGridSpec(
            num_scalar_prefetch=2,
            grid=(n_tiles,),
            in_specs=[
                pl.BlockSpec(memory_space=pl.ANY),
                pl.BlockSpec((TM, D), lambda i, p, n: (row_tile(i), 0)),
                pl.BlockSpec((None, None, 3, D), lambda i, p, n: (ls, mod_row(i), 0, 0)),
            ],
            out_specs=pl.BlockSpec(memory_space=pl.ANY),
            scratch_shapes=[
                pltpu.SMEM((2, TM * TOP_K), jnp.int32),
                pltpu.VMEM((2, TM, 1, D // 2), jnp.uint32),
                pltpu.VMEM((bm, 1, D // 2), jnp.uint32),
                pltpu.SemaphoreType.DMA((2,)),
                pltpu.SemaphoreType.DMA((2,)),
            ],
        ),
        compiler_params=_cparams(("arbitrary",), VMEM_LIMIT),
        name="moe_dispatch",
    )(pad_lo, n_used, dest_tiles, x, mods)


def _sigmoid(x):
    return 0.5 * jnp.tanh(0.5 * x) + 0.5


def _softplus(x):
    return jnp.maximum(x, 0.0) + jnp.log(1.0 + jnp.exp(-jnp.abs(x)))


def _gelu_tanh(x):
    c = math.sqrt(2.0 / math.pi)
    return 0.5 * x * (1.0 + jnp.tanh(c * (x + 0.044715 * (x * x * x))))


def _group_scan(a, b, row, reverse):
    for s in (1, 2, 4):
        if reverse:
            keep = row < SUBLANES - s
            shift = SUBLANES - s
        else:
            keep = row >= s
            shift = s
        a_sh = jnp.where(keep, pltpu.roll(a, shift, axis=0), 1.0)
        b_sh = jnp.where(keep, pltpu.roll(b, shift, axis=0), 0.0)
        b = b + a * b_sh
        a = a * a_sh
    return a, b


def _lru_kernel(ug_ref, ur_ref, cw_ref, cb_ref, wg_ref, bg_ref, lam_ref, o_ref,
                a0_ref, b0_ref, a1_ref, b1_ref, *, ctx_len):
    T, W = ur_ref.shape
    CH = 256
    n_chunks = T // CH
    ctx_chunks = ctx_len // CH
    row_c = lax.broadcasted_iota(jnp.int32, (CH, W), 0)
    kexp = (-0.5 * LRU_C * math.log2(math.e)) * _softplus(-lam_ref[...])

    def coeff_chunk(c, carry):
        c0 = pl.multiple_of(c * CH, CH)
        cur = ur_ref[pl.ds(c0, CH), :]
        prev8 = ur_ref[pl.ds(pl.multiple_of(jnp.maximum(c0 - SUBLANES, 0), SUBLANES), SUBLANES), :]
        next8 = ur_ref[pl.ds(pl.multiple_of(jnp.minimum(c0 + CH, T - SUBLANES), SUBLANES), SUBLANES), :]
        seg_start = jnp.logical_or(c == 0, c == ctx_chunks)
        seg_end = jnp.logical_or(c == ctx_chunks - 1, c == n_chunks - 1)
        p_last = jnp.where(seg_start, 0.0, prev8[SUBLANES - 1:SUBLANES, :])
        n_first = jnp.where(seg_end, 0.0, next8[0:1, :])
        n_second = jnp.where(seg_end, 0.0, next8[1:2, :])
        u_m1 = jnp.where(row_c == 0, p_last, pltpu.roll(cur, 1, axis=0))
        u_p1 = jnp.where(row_c == CH - 1, n_first, pltpu.roll(cur, CH - 1, axis=0))
        u_p2 = jnp.where(row_c == CH - 2, n_first,
                         jnp.where(row_c == CH - 1, n_second, pltpu.roll(cur, CH - 2, axis=0)))
        r = cb_ref[...] + u_m1 * cw_ref[0:1, :]
        r = r + cur * cw_ref[1:2, :]
        r = r + u_p1 * cw_ref[2:3, :]
        r = r + u_p2 * cw_ref[3:4, :]
        gts = jnp.dot(r.astype(BF16), wg_ref[0].astype(BF16), preferred_element_type=F32) + bg_ref[0]
        half_r = 0.5 * r
        for d, (a_ref, b_ref) in enumerate(((a0_ref, b0_ref), (a1_ref, b1_ref))):
            t_rec = jnp.tanh(gts[:, (2 * d) * W:(2 * d + 1) * W])
            t_in = jnp.tanh(gts[:, (2 * d + 1) * W:(2 * d + 2) * W])
            kd = kexp[d:d + 1, :]
            a = jnp.exp2(kd * t_rec + kd)
            a_ref[pl.ds(c0, CH), :] = a
            b_ref[pl.ds(c0, CH), :] = jnp.sqrt(jnp.maximum(1.0 - a * a, 0.0)) * (half_r * t_in + half_r)
        return carry

    lax.fori_loop(0, n_chunks, coeff_chunk, 0)

    row8 = lax.broadcasted_iota(jnp.int32, (SUBLANES, W), 0)
    n_groups = T // SUBLANES
    ctx_groups = ctx_len // SUBLANES

    def scan_step(k, carry):
        hf, hb = carry
        gf = pl.multiple_of(k * SUBLANES, SUBLANES)
        a, b = _group_scan(a0_ref[pl.ds(gf, SUBLANES), :], b0_ref[pl.ds(gf, SUBLANES), :], row8, False)
        h = a * hf + b
        a0_ref[pl.ds(gf, SUBLANES), :] = h
        hf = jnp.broadcast_to(h[SUBLANES - 1:SUBLANES, :], h.shape)
        g = jnp.where(k < ctx_groups, ctx_groups - 1 - k, n_groups - 1 - (k - ctx_groups))
        gb = pl.multiple_of(g * SUBLANES, SUBLANES)
        a, b = _group_scan(a1_ref[pl.ds(gb, SUBLANES), :], b1_ref[pl.ds(gb, SUBLANES), :], row8, True)
        h = a * hb + b
        a1_ref[pl.ds(gb, SUBLANES), :] = h
        hb = jnp.broadcast_to(h[0:1, :], h.shape)
        return hf, hb

    zero = jnp.zeros((SUBLANES, W), F32)
    lax.fori_loop(0, n_groups, scan_step, (zero, zero), unroll=4)

    def out_chunk(c, carry):
        c0 = pl.multiple_of(c * CH, CH)
        y = a0_ref[pl.ds(c0, CH), :] + a1_ref[pl.ds(c0, CH), :]
        o_ref[pl.ds(c0, CH), :] = (y * _gelu_tanh(ug_ref[pl.ds(c0, CH), :])).astype(o_ref.dtype)
        return carry

    lax.fori_loop(0, n_chunks, out_chunk, 0)


def _lru_mix(u, conv_w, conv_b, gate_w, gate_b, lam, *, nbatch, ctx_len):
    M, W2 = u.shape
    W = W2 // 2
    T = M // nbatch
    nblk = W // LRU_BLOCK_W
    bw = LRU_BLOCK_W
    wg = 0.5 * jnp.transpose(gate_w, (2, 3, 0, 1, 4)).reshape(nblk, bw, 4 * bw)
    bg = 0.5 * jnp.transpose(gate_b.reshape(2, 2, nblk, bw), (2, 0, 1, 3)).reshape(nblk, 1, 4 * bw)
    return pl.pallas_call(
        functools.partial(_lru_kernel, ctx_len=ctx_len),
        out_shape=jax.ShapeDtypeStruct((M, W), BF16),
        grid=(nbatch, nblk),
        in_specs=[
            pl.BlockSpec((T, bw), lambda b, n: (b, n)),
            pl.BlockSpec((T, bw), lambda b, n: (b, nblk + n)),
            pl.BlockSpec((CONV_W, bw), lambda b, n: (0, n)),
            pl.BlockSpec((1, bw), lambda b, n: (0, n)),
            pl.BlockSpec((1, bw, 4 * bw), lambda b, n: (n, 0, 0)),
            pl.BlockSpec((1, 1, 4 * bw), lambda b, n: (n, 0, 0)),
            pl.BlockSpec((2, bw), lambda b, n: (0, n)),
        ],
        out_specs=pl.BlockSpec((T, bw), lambda b, n: (b, n)),
        scratch_shapes=[pltpu.VMEM((T, bw), F32) for _ in range(4)],
        compiler_params=_cparams(("parallel", "parallel"), VMEM_LIMIT),
        name="lru_mix",
    )(u, u, conv_w, conv_b.reshape(1, W), wg, bg, lam)


def _attn_kernel(sink_ref, q_ref, k_ref, v_ref, o_ref, k2_ref, v2_ref, p_ref, *, ctx_len, n_kv):
    QB = q_ref.shape[0]
    T = k_ref.shape[0]
    n_lat = (T - ctx_len) // QB
    ctx_blocks = ctx_len // QB
    band = 3 * QB
    pairs = GROUP // 2
    qb = pl.program_id(1)
    j = qb - ctx_blocks
    start = jnp.clip(j - 1, 0, n_lat - 3)
    off = pl.multiple_of(ctx_len + start * QB, QB)
    lo_half = lax.broadcasted_iota(jnp.int32, (1, LANES), 1) < HEAD_DIM
    nt = (((1,), (1,)), ((), ()))

    def attend(with_band):
        nk = ctx_len + (band if with_band else 0)
        if with_band:
            col = lax.broadcasted_iota(jnp.int32, (QB, nk), 1)
            qpos = j * QB + lax.broadcasted_iota(jnp.int32, (QB, nk), 0)
            kpos = start * QB + (col - ctx_len)
            valid = jnp.logical_or(col < ctx_len, jnp.abs(qpos - kpos) <= WINDOW)
        for h in range(n_kv):
            slab = slice((h // 2) * LANES, (h // 2 + 1) * LANES)

            def two_blocks(x):
                swapped = jnp.concatenate([x[:, HEAD_DIM:], x[:, :HEAD_DIM]], axis=1)
                own, other = (x, swapped) if h % 2 == 0 else (swapped, x)
                zero = jnp.zeros_like(x)
                return jnp.where(lo_half, own, zero), jnp.where(lo_half, zero, other)

            for dst, src in ((k2_ref, k_ref), (v2_ref, v_ref)):
                top_c, bot_c = two_blocks(src[0:ctx_len, slab])
                dst[0:ctx_len, :] = top_c
                dst[nk:nk + ctx_len, :] = bot_c
                if with_band:
                    top_b, bot_b = two_blocks(src[pl.ds(off, band), slab])
                    dst[ctx_len:nk, :] = top_b
                    dst[nk + ctx_len:2 * nk, :] = bot_b
            c_q = h * GROUP * HEAD_DIM
            qs = jnp.concatenate([q_ref[:, c_q + p * LANES:c_q + (p + 1) * LANES] for p in range(pairs)], axis=0)
            s = lax.dot_general(qs, k2_ref[0:2 * nk, :], nt, preferred_element_type=F32)
            invs = []
            for p in range(pairs):
                rows = slice(p * QB, (p + 1) * QB)
                inv_pair = []
                for half in range(2):
                    c0 = half * nk
                    sink = sink_ref[h * GROUP + 2 * p + half] * LOG2_E
                    s_h = s[rows, c0:c0 + nk]
                    if with_band:
                        s_h = jnp.where(valid, s_h, NEG_INF)
                    m = jnp.maximum(jnp.max(s_h, axis=-1, keepdims=True), sink)
                    p_h = jnp.exp2(s_h - m)
                    denom = jnp.sum(p_h, axis=-1, keepdims=True) + jnp.exp2(sink - m)
                    p_ref[rows, c0:c0 + nk] = p_h.astype(BF16)
                    inv_pair.append(1.0 / denom)
                invs.append(inv_pair)
            o = jnp.dot(p_ref[:, 0:2 * nk], v2_ref[0:2 * nk, :], preferred_element_type=F32)
            for p in range(pairs):
                inv = jnp.where(lo_half, invs[p][0], invs[p][1])
                o_ref[:, c_q + p * LANES:c_q + (p + 1) * LANES] = (o[p * QB:(p + 1) * QB] * inv).astype(o_ref.dtype)

    @pl.when(qb >= ctx_blocks)
    def _():
        attend(True)

    @pl.when(qb < ctx_blocks)
    def _():
        attend(False)


def _attention(qkv, sinks, *, nbatch, ctx_len, q_dim, kv_dim):
    M = qkv.shape[0]
    T = M // nbatch
    nqb = T // Q_BLOCK
    kcol = q_dim // kv_dim
    return pl.pallas_call(
        functools.partial(_attn_kernel, ctx_len=ctx_len, n_kv=kv_dim // HEAD_DIM),
        out_shape=jax.ShapeDtypeStruct((M, q_dim), BF16),
        grid_spec=pltpu.PrefetchScalarGridSpec(
            num_scalar_prefetch=1,
            grid=(nbatch, nqb),
            in_specs=[
                pl.BlockSpec((Q_BLOCK, q_dim), lambda b, q, s: (b * nqb + q, 0)),
                pl.BlockSpec((T, kv_dim), lambda b, q, s: (b, kcol)),
                pl.BlockSpec((T, kv_dim), lambda b, q, s: (b, kcol + 1)),
            ],
            out_specs=pl.BlockSpec((Q_BLOCK, q_dim), lambda b, q, s: (b * nqb + q, 0)),
            scratch_shapes=[
                pltpu.VMEM((2 * (ctx_len + 3 * Q_BLOCK), LANES), BF16),
                pltpu.VMEM((2 * (ctx_len + 3 * Q_BLOCK), LANES), BF16),
                pltpu.VMEM((GROUP // 2 * Q_BLOCK, 2 * (ctx_len + 3 * Q_BLOCK)), BF16),
            ],
        ),
        compiler_params=_cparams(("parallel", "parallel"), VMEM_LIMIT),
        name="swa_attention",
    )(sinks, qkv, qkv, qkv)


def _split_bf16(x):
    hi = x.astype(BF16)
    lo = (x - hi.astype(F32)).astype(BF16)
    return hi, lo


def _router_kernel(x_ref, mod_ref, wr_ref, br_ref, e_ref, g_ref, r_ref, cnt_ref, carry_ref):
    i = pl.program_id(0)

    @pl.when(i == 0)
    def _():
        carry_ref[...] = jnp.zeros_like(carry_ref)

    shift = mod_ref[0:1, :]
    scale = mod_ref[1:2, :]
    a = x_ref[...] * (1.0 + scale) + shift
    a_hi, a_lo = _split_bf16(a)
    ne = br_ref.shape[1]
    hi_terms = jnp.dot(a_hi, wr_ref[...], preferred_element_type=F32)
    lo_term = jnp.dot(a_lo, wr_ref[:, 0:ne], preferred_element_type=F32)
    logits = (hi_terms[:, 0:ne] + (lo_term + hi_terms[:, ne:2 * ne])) + br_ref[...]
    tm = logits.shape[0]
    lane_e = lax.broadcasted_iota(jnp.int32, (tm, ne), 1)
    lane_o = lax.broadcasted_iota(jnp.int32, (tm, LANES), 1)
    vals, idxs, hots = [], [], []
    cur = logits
    for k in range(TOP_K):
        m = jnp.max(cur, axis=-1, keepdims=True)
        idx = jnp.min(jnp.where(cur == m, lane_e, ne), axis=-1, keepdims=True)
        hot = lane_e == idx
        vals.append(m)
        idxs.append(idx)
        hots.append(hot)
        cur = jnp.where(hot, -jnp.inf, cur)
    exps = [jnp.exp(v - vals[0]) for v in vals]
    inv = 1.0 / (exps[0] + exps[1] + exps[2] + exps[3])
    msum = jnp.zeros((tm, ne), F32)
    for hot in hots:
        msum = msum + hot.astype(F32)
    rr = lax.broadcasted_iota(jnp.int32, (tm, tm), 0)
    cc = lax.broadcasted_iota(jnp.int32, (tm, tm), 1)
    tri = (cc < rr).astype(BF16)
    before = jnp.dot(tri, msum.astype(BF16), preferred_element_type=F32) + carry_ref[...]
    e_out = jnp.zeros((tm, LANES), jnp.int32)
    g_out = jnp.zeros((tm, LANES), F32)
    r_out = jnp.zeros((tm, LANES), jnp.int32)
    for k in range(TOP_K):
        rank = jnp.sum(jnp.where(hots[k], before, 0.0), axis=-1, keepdims=True).astype(jnp.int32)
        e_out = jnp.where(lane_o == k, idxs[k], e_out)
        g_out = jnp.where(lane_o == k, exps[k] * inv, g_out)
        r_out = jnp.where(lane_o == k, rank, r_out)
    e_ref[...] = e_out
    g_ref[...] = g_out
    r_ref[...] = r_out
    carry_ref[...] = carry_ref[...] + jnp.sum(msum, axis=0, keepdims=True)
    cnt_ref[...] = carry_ref[...].astype(jnp.int32)


def _router(x, mods, ls, w_r, b_r, *, geom, latent_only):
    D = x.shape[1]
    ne = w_r.shape[1]
    n_tiles, row_tile, mod_row = _token_tiles(geom, latent_only)
    M = n_tiles * TM
    return pl.pallas_call(
        _router_kernel,
        out_shape=(
            jax.ShapeDtypeStruct((M, LANES), jnp.int32),
            jax.ShapeDtypeStruct((M, LANES), F32),
            jax.ShapeDtypeStruct((M, LANES), jnp.int32),
            jax.ShapeDtypeStruct((1, ne), jnp.int32),
        ),
        grid=(n_tiles,),
        in_specs=[
            pl.BlockSpec((TM, D), lambda i: (row_tile(i), 0)),
            pl.BlockSpec((None, None, 3, D), lambda i: (ls, mod_row(i), 0, 0)),
            pl.BlockSpec((D, 2 * ne), lambda i: (0, 0)),
            pl.BlockSpec((1, ne), lambda i: (0, 0)),
        ],
        out_specs=(
            pl.BlockSpec((TM, LANES), lambda i: (i, 0)),
            pl.BlockSpec((TM, LANES), lambda i: (i, 0)),
            pl.BlockSpec((TM, LANES), lambda i: (i, 0)),
            pl.BlockSpec((1, ne), lambda i: (0, 0)),
        ),
        scratch_shapes=[pltpu.VMEM((1, ne), F32)],
        compiler_params=_cparams(("arbitrary",), VMEM_LIMIT),
        name="moe_router",
    )(x, mods, jnp.concatenate(_split_bf16(w_r), axis=1), b_r.reshape(1, ne))


def _expert_kernel(blk_e_ref, slot_ref, first_ref, next_e_ref, nused_ref, xs_ref, bgu_ref, bdn_ref,
                   wgu_hbm, wdn_hbm, o_ref, wgu_buf, wdn_buf, x2d_ref, sem_gu, sem_dn, *, layer):
    b = pl.program_id(0)
    de, d = wdn_buf.shape[1:]
    half = d // 2

    def fetch(expert, slot):
        cps = []
        for hbm, buf, sem, chunks in ((wgu_hbm, wgu_buf, sem_gu, WEIGHT_DMA_CHUNKS),
                                      (wdn_hbm, wdn_buf, sem_dn, WEIGHT_DMA_CHUNKS // 2)):
            rows = buf.shape[1] // chunks
            for c in range(chunks):
                rs = pl.ds(c * rows, rows)
                cps.append(pltpu.make_async_copy(hbm.at[layer, expert, rs], buf.at[slot, rs], sem.at[slot]))
        return cps

    @pl.when(b < nused_ref[0])
    def _():
        slot = slot_ref[b]

        @pl.when(b == 0)
        def _():
            for c, cp in enumerate(fetch(blk_e_ref[0], 0)):
                cp.start(priority=c % 2)

        @pl.when(first_ref[b] == 1)
        def _():
            @pl.when(next_e_ref[b] >= 0)
            def _():
                for c, cp in enumerate(fetch(next_e_ref[b], 1 - slot)):
                    cp.start(priority=c % 2)
            for cp in fetch(blk_e_ref[b], slot):
                cp.wait()

        x2d_ref[...] = xs_ref[...].reshape(x2d_ref.shape)
        x_lo, x_hi = (v.astype(BF16) for v in _unpack_bf16_pair(x2d_ref[...]))
        hs = []
        for c in range(2 * de // 256):
            sl = slice(c * 256, (c + 1) * 256)
            acc = jnp.dot(x_lo, wgu_buf[slot, 0:half, sl].astype(BF16), preferred_element_type=F32)
            acc = acc + jnp.dot(x_hi, wgu_buf[slot, half:d, sl].astype(BF16), preferred_element_type=F32)
            hs.append(acc + bgu_ref[:, sl])
        h = jnp.concatenate(hs, axis=1)
        g = jnp.minimum(h[:, :de], SWIGLU_LIMIT)
        u = jnp.clip(h[:, de:], -SWIGLU_LIMIT, SWIGLU_LIMIT)
        act = ((u + 1.0) * (g * _sigmoid(SWIGLU_ALPHA * g))).astype(BF16)
        ys = []
        for c in range(d // 512):
            sl = slice(c * 512, (c + 1) * 512)
            ys.append(jnp.dot(act, wdn_buf[slot, :, sl].astype(BF16), preferred_element_type=F32) + bdn_ref[:, sl])
        nh = len(ys) // 2
        for c in range(nh):
            packed = _pack_bf16_pair(ys[c], ys[c + nh])
            o_ref[:, :, c * 512:(c + 1) * 512] = packed.reshape(packed.shape[0], 1, 512)

    @pl.when(b >= nused_ref[0])
    def _():
        o_ref[...] = jnp.zeros_like(o_ref)


def _expert_ffn(xs, blk_tabs, n_used, layer, w_gu, b_gu, w_dn, b_dn):
    P = xs.shape[0]
    _, ne, D, de2 = w_gu.shape
    de = de2 // 2
    n_blk = P // EXPERT_BM
    blk_e, slot, first, next_e = blk_tabs
    return pl.pallas_call(
        functools.partial(_expert_kernel, layer=layer),
        out_shape=jax.ShapeDtypeStruct((P, 1, D // 2), jnp.uint32),
        grid_spec=pltpu.PrefetchScalarGridSpec(
            num_scalar_prefetch=5,
            grid=(n_blk,),
            in_specs=[
                pl.BlockSpec((EXPERT_BM, 1, D // 2), lambda b, e, s, f, x, n: (jnp.minimum(b, n[0] - 1), 0, 0)),
                pl.BlockSpec((None, None, 1, de2), lambda b, e, s, f, x, n: (layer, e[b], 0, 0)),
                pl.BlockSpec((None, None, 1, D), lambda b, e, s, f, x, n: (layer, e[b], 0, 0)),
                pl.BlockSpec(memory_space=pl.ANY),
                pl.BlockSpec(memory_space=pl.ANY),
            ],
            out_specs=pl.BlockSpec((EXPERT_BM, 1, D // 2), lambda b, e, s, f, x, n: (b, 0, 0)),
            scratch_shapes=[
                pltpu.VMEM((2, D, de2), F32),
                pltpu.VMEM((2, de, D), F32),
                pltpu.VMEM((EXPERT_BM, D // 2), jnp.uint32),
                pltpu.SemaphoreType.DMA((2,)),
                pltpu.SemaphoreType.DMA((2,)),
            ],
        ),
        compiler_params=_cparams(("arbitrary",), VMEM_LIMIT),
        name="expert_ffn",
    )(blk_e, slot, first, next_e, n_used, xs, b_gu.reshape(b_gu.shape[0], ne, 1, de2),
      b_dn.reshape(b_dn.shape[0], ne, 1, D), w_gu, w_dn)


def _moe_layer(x, mods, ls, layer, w_r, b_r, w_gu, b_gu, w_dn, b_dn, ln_g, ln_b, *, alpha, geom, latent_only):
    ne = w_r.shape[1]
    bm = EXPERT_BM
    e128, g128, r128, counts = _router(x, mods, ls, w_r, b_r, geom=geom, latent_only=latent_only)
    M = e128.shape[0]
    top_e = e128[:, :TOP_K]
    rank = r128[:, :TOP_K]
    counts = counts[0]
    padded = ((counts + bm - 1) // bm) * bm
    ends = jnp.cumsum(padded)
    pstart = ends - padded
    experts = jnp.arange(ne, dtype=jnp.int32)
    dest = jnp.sum(jnp.where(top_e[..., None] == experts, pstart, 0), axis=-1) + rank
    dest_tiles = dest.astype(jnp.int32).reshape(M // TM, TM * TOP_K)
    A = M * TOP_K
    P = (-(-A // bm)) * bm + ne * bm
    n_blk = P // bm
    n_used = (ends[-1] // bm).astype(jnp.int32).reshape(1)
    blk = jnp.arange(n_blk, dtype=jnp.int32)
    blk_row = jnp.minimum(blk, n_used[0] - 1) * bm
    blk_e = jnp.minimum(jnp.sum(ends[None, :] <= blk_row[:, None], axis=1), ne - 1).astype(jnp.int32)
    first = jnp.logical_and(blk < n_used[0],
                            jnp.concatenate([jnp.ones((1,), bool), blk_e[1:] != blk_e[:-1]])).astype(jnp.int32)
    slot = ((jnp.cumsum(first) - 1) % 2).astype(jnp.int32)
    later = jnp.logical_and(experts[None, :] > experts[:, None], (padded > 0)[None, :])
    next_nonempty = jnp.min(jnp.where(later, experts[None, :], ne), axis=1)
    next_nonempty = jnp.where(next_nonempty == ne, -1, next_nonempty)
    next_e = jnp.sum(jnp.where(blk_e[:, None] == experts[None, :], next_nonempty[None, :], 0), axis=1).astype(jnp.int32)
    pad_lo = jnp.where(padded > 0, ends - bm, -1).astype(jnp.int32)
    xs = _dispatch(pad_lo, n_used, dest_tiles, x, mods, ls, P, geom=geom, latent_only=latent_only)
    ys = _expert_ffn(xs, (blk_e, slot, first, next_e), n_used, layer, w_gu, b_gu, w_dn, b_dn)
    return _combine_ln(dest_tiles, ys, g128, x, mods, ls, ln_g, ln_b, alpha=alpha, geom=geom,
                       latent_only=latent_only)


def _rope_tables(seq, ctx_len):
    rows = seq // GRID_W
    row_pos = jnp.repeat(jnp.arange(rows), GRID_W).astype(F32)
    col_pos = jnp.tile(jnp.arange(GRID_W), rows).astype(F32)
    n = HEAD_DIM // 2
    freqs = ROPE_BASE ** (-jnp.arange(0, n, 2, dtype=F32) / n)
    ang_r = row_pos[:, None] * freqs[None, :]
    ang_c = col_pos[:, None] * freqs[None, :]
    cos = jnp.concatenate([jnp.cos(ang_r), jnp.cos(ang_r), jnp.cos(ang_c), jnp.cos(ang_c)], axis=1)
    sin = jnp.concatenate([-jnp.sin(ang_r), jnp.sin(ang_r), -jnp.sin(ang_c), jnp.sin(ang_c)], axis=1)
    cos = jnp.concatenate([jnp.ones((ctx_len, HEAD_DIM), F32), cos], axis=0)
    sin = jnp.concatenate([jnp.zeros((ctx_len, HEAD_DIM), F32), sin], axis=0)
    cos = jnp.tile(cos, (1, LANES // HEAD_DIM))
    sin = jnp.tile(sin, (1, LANES // HEAD_DIM))
    scale = HEAD_DIM ** -0.5 * LOG2_E
    return jnp.stack([cos * scale, sin * scale, cos, sin], axis=0)


def kernel(x, c, ctx, c_ctx, ada_w, ada_b, ln_g, ln_b, lru_w_in, lru_conv_w, lru_conv_b, lru_gate_w, lru_gate_b, lru_lambda, lru_w_out, attn_w_qkv, attn_sinks, attn_w_o, router_w, router_b, moe_w_gu, moe_b_gu, moe_w_down, moe_b_down):
    B, S, D = x.shape
    C = ctx.shape[1]
    depth = ada_w.shape[0]
    T = C + S
    assert C % TM == 0 and S % TM == 0 and C % Q_BLOCK == 0
    geom = (T // TM, C // TM, B)
    alpha = (2.0 * depth) ** 0.25
    kv_dim = (attn_w_qkv.shape[2] - D) // 2

    act = jax.nn.silu(jnp.concatenate([c, c_ctx[None, :]], axis=0))
    mods = _ada_params(act, ada_w.reshape(depth * 2, D, 3 * D), ada_b.reshape(depth * 2, 3 * D))
    mods = mods.reshape(depth * 2, SUBLANES, 3, D)
    tab = _rope_tables(S, C)

    z = (ctx.reshape(B * C, D), x.reshape(B * S, D))
    for i in range(depth):
        j = i // 2
        if i % 2 == 0:
            u = _mod_matmul(z, mods, 2 * i, lru_w_in, j, tn=D, out_dtype=F32, geom=geom)
            y = _lru_mix(u, lru_conv_w[j], lru_conv_b[j], lru_gate_w[j], lru_gate_b[j], lru_lambda[j],
                         nbatch=B, ctx_len=C)
            w_out = lru_w_out
        else:
            qkv = _mod_matmul(z, mods, 2 * i, attn_w_qkv, j, tn=D + 2 * kv_dim, out_dtype=BF16,
                              geom=geom, rope=(tab, (D, kv_dim)))
            y = _attention(qkv, attn_sinks[j], nbatch=B, ctx_len=C, q_dim=D, kv_dim=kv_dim)
            w_out = attn_w_o
        z = _proj_ln(y, w_out, j, z, mods, 2 * i, ln_g[i, 0], ln_b[i, 0], alpha=alpha, geom=geom)
        z = _moe_layer(z, mods, 2 * i + 1, i, router_w[i], router_b[i], moe_w_gu, moe_b_gu,
                       moe_w_down, moe_b_down, ln_g[i, 1], ln_b[i, 1], alpha=alpha, geom=geom,
                       latent_only=(i == depth - 1))
    return z.reshape(B, S, D)
```

```python
import functools
import math

import jax
import jax.numpy as jnp
from jax import lax
from jax.experimental import pallas as pl
from jax.experimental.pallas import tpu as pltpu

GRID_W = 64
LN_EPS = 1e-5
LRU_BLOCK_W = 256
CONV_W = 4
LRU_C = 8.0
HEAD_DIM = 64
GROUP = 8
WINDOW = 128
Q_BLOCK = 128
ROPE_BASE = 10000.0
NEG_INF = -1e30
TOP_K = 4
SWIGLU_LIMIT = 7.0
SWIGLU_ALPHA = 1.702
LOG2_E = math.log2(math.e)
SQRT_FLOOR = 2.0 ** -100

LANES = 128
SUBLANES = 8
TM = 256
EXPERT_BM = 256
WEIGHT_DMA_CHUNKS = 4
V7X_VMEM_BYTES = 64 * 1024 * 1024
VMEM_LIMIT = V7X_VMEM_BYTES * 7 // 8

BF16 = jnp.bfloat16
F32 = jnp.float32


def _cparams(sem, vmem=None):
    return pltpu.CompilerParams(dimension_semantics=sem, vmem_limit_bytes=vmem)


def _ada_kernel(a_ref, w_ref, b_ref, o_ref):
    acc = jnp.dot(a_ref[...], w_ref[0].astype(BF16), preferred_element_type=F32)
    o_ref[0] = acc[0:SUBLANES] + acc[SUBLANES:2 * SUBLANES] + b_ref[0]


def _ada_params(act, w, bias):
    R, D = act.shape
    L, _, N = w.shape
    tn = 1024
    a8 = jnp.zeros((SUBLANES, D), F32).at[:R].set(act)
    hi = a8.astype(BF16)
    lo = (a8 - hi.astype(F32)).astype(BF16)
    a16 = jnp.concatenate([hi, lo], axis=0)
    return pl.pallas_call(
        _ada_kernel,
        out_shape=jax.ShapeDtypeStruct((L, SUBLANES, N), F32),
        grid=(L, N // tn),
        in_specs=[
            pl.BlockSpec((2 * SUBLANES, D), lambda l, n: (0, 0)),
            pl.BlockSpec((1, D, tn), lambda l, n: (l, 0, n)),
            pl.BlockSpec((1, 1, tn), lambda l, n: (l, 0, n)),
        ],
        out_specs=pl.BlockSpec((1, SUBLANES, tn), lambda l, n: (l, 0, n)),
        compiler_params=_cparams(("parallel", "parallel"), VMEM_LIMIT),
        name="ada_params",
    )(a16, w, bias.reshape(L, 1, N))


def _mod_row(i, tiles_per_batch, ctx_tiles, nbatch):
    return jnp.where(i % tiles_per_batch < ctx_tiles, nbatch, i // tiles_per_batch)


def _token_tiles(geom, latent_only):
    tpb, ctx_tiles, nbatch = geom
    if latent_only:
        lpb = tpb - ctx_tiles
        return lpb * nbatch, (lambda i: (i // lpb) * tpb + ctx_tiles + i % lpb), (lambda i: i // lpb)
    return tpb * nbatch, (lambda i: i), (lambda i: _mod_row(i, tpb, ctx_tiles, nbatch))


def _stream_specs(stream, geom, tile_of):
    tpb, ctx_tiles, nbatch = geom
    if not isinstance(stream, tuple):
        return [pl.BlockSpec((TM, stream.shape[1]), lambda *g: (tile_of(*g), 0))], [stream]
    ctx_rows, lat_rows = stream
    lpb = tpb - ctx_tiles

    def ctx_tile(*g):
        i = tile_of(*g)
        return ((i // tpb) * ctx_tiles + jnp.minimum(i % tpb, ctx_tiles - 1), 0)

    def lat_tile(*g):
        i = tile_of(*g)
        return ((i // tpb) * lpb + jnp.maximum(i % tpb - ctx_tiles, 0), 0)

    D = ctx_rows.shape[1]
    return [pl.BlockSpec((TM, D), ctx_tile), pl.BlockSpec((TM, D), lat_tile)], [ctx_rows, lat_rows]


def _stream_tile(x_refs, tile, geom, cols=slice(None)):
    if len(x_refs) == 1:
        return x_refs[0][:, cols]
    tpb, ctx_tiles, _ = geom
    return jnp.where(tile % tpb < ctx_tiles, x_refs[0][:, cols], x_refs[1][:, cols])


def _rope_chunk(y, cos, sin_signed, first_half):
    rot = jnp.where(first_half, pltpu.roll(y, LANES - 16, axis=1), pltpu.roll(y, 16, axis=1))
    return y * cos + rot * sin_signed


def _modmm_kernel(*refs, rope_cols, n_x, geom):
    x_refs, (mod_ref, w_ref, *rest) = refs[:n_x], refs[n_x:]
    if rope_cols is None:
        (o_ref,) = rest
    else:
        tab_ref, o_ref = rest
    shift = mod_ref[0:1, :]
    scale = mod_ref[1:2, :]
    a = (_stream_tile(x_refs, pl.program_id(1), geom) * (1.0 + scale) + shift).astype(BF16)
    tn = o_ref.shape[1]
    ch = 512 if tn % 512 == 0 else 256
    for c in range(tn // ch):
        acc = jnp.dot(a, w_ref[:, c * ch:(c + 1) * ch].astype(BF16), preferred_element_type=F32)
        if rope_cols is None:
            o_ref[:, c * ch:(c + 1) * ch] = acc.astype(o_ref.dtype)
        else:
            q_cols, k_cols = rope_cols
            lane = lax.broadcasted_iota(jnp.int32, (acc.shape[0], LANES), 1)
            first_half = (lane % 32) < 16
            for s in range(ch // LANES):
                col = c * ch + s * LANES
                y = acc[:, s * LANES:(s + 1) * LANES]
                if col < q_cols:
                    y = _rope_chunk(y, tab_ref[0], tab_ref[1], first_half)
                elif col < q_cols + k_cols:
                    y = _rope_chunk(y, tab_ref[2], tab_ref[3], first_half)
                o_ref[:, col:col + LANES] = y.astype(o_ref.dtype)


def _mod_matmul(x, mods, ls, w_stack, layer, *, tn, out_dtype, geom, rope=None):
    K, N = w_stack.shape[1:]
    tpb, ctx_tiles, nbatch = geom
    M = tpb * nbatch * TM
    x_specs, x_args = _stream_specs(x, geom, lambda j, i: i)
    in_specs = x_specs + [
        pl.BlockSpec((None, None, 3, K), lambda j, i: (ls, _mod_row(i, tpb, ctx_tiles, nbatch), 0, 0)),
        pl.BlockSpec((None, K, tn), lambda j, i: (layer, 0, j)),
    ]
    args = x_args + [mods, w_stack]
    rope_cols = None
    if rope is not None:
        tab, rope_cols = rope
        in_specs.append(pl.BlockSpec((4, TM, LANES), lambda j, i: (0, i % tpb, 0)))
        args.append(tab)
    return pl.pallas_call(
        functools.partial(_modmm_kernel, rope_cols=rope_cols, n_x=len(x_args), geom=geom),
        out_shape=jax.ShapeDtypeStruct((M, N), out_dtype),
        grid=(N // tn, M // TM),
        in_specs=in_specs,
        out_specs=pl.BlockSpec((TM, tn), lambda j, i: (i, j)),
        compiler_params=_cparams(("parallel", "parallel"), VMEM_LIMIT),
        name="mod_matmul",
    )(*args)


def _layer_norm_rows(z, g, b):
    mu = jnp.mean(z, axis=-1, keepdims=True)
    zc = z - mu
    var = jnp.mean(zc * zc, axis=-1, keepdims=True)
    return zc * lax.rsqrt(var + LN_EPS) * g + b


def _proj_ln_kernel(y_ref, w_ref, *refs, alpha, n_x, geom):
    x_refs, (mod_ref, g_ref, b_ref, o_ref, z_ref) = refs[:n_x], refs[n_x:]
    gate = mod_ref[2:3, :]
    y = y_ref[...]
    n = o_ref.shape[1]
    ch = 512
    for c in range(n // ch):
        sl = slice(c * ch, (c + 1) * ch)
        acc = jnp.dot(y, w_ref[:, sl].astype(BF16), preferred_element_type=F32)
        z_ref[:, sl] = alpha * _stream_tile(x_refs, pl.program_id(0), geom, sl) + gate[:, sl] * acc
    o_ref[...] = _layer_norm_rows(z_ref[...], g_ref[...], b_ref[...])


def _proj_ln(y, w_stack, j, x, mods, ls, ln_g, ln_b, *, alpha, geom):
    M, K = y.shape
    D = w_stack.shape[2]
    tpb, ctx_tiles, nbatch = geom
    x_specs, x_args = _stream_specs(x, geom, lambda i: i)
    return pl.pallas_call(
        functools.partial(_proj_ln_kernel, alpha=alpha, n_x=len(x_args), geom=geom),
        out_shape=jax.ShapeDtypeStruct((M, D), F32),
        grid=(M // TM,),
        in_specs=[
            pl.BlockSpec((TM, K), lambda i: (i, 0)),
            pl.BlockSpec((None, K, D), lambda i: (j, 0, 0)),
            *x_specs,
            pl.BlockSpec((None, None, 3, D), lambda i: (ls, _mod_row(i, tpb, ctx_tiles, nbatch), 0, 0)),
            pl.BlockSpec((1, D), lambda i: (0, 0)),
            pl.BlockSpec((1, D), lambda i: (0, 0)),
        ],
        out_specs=pl.BlockSpec((TM, D), lambda i: (i, 0)),
        scratch_shapes=[pltpu.VMEM((TM, D), F32)],
        compiler_params=_cparams(("parallel",), VMEM_LIMIT),
        name="proj_ln",
    )(y, w_stack, *x_args, mods, ln_g.reshape(1, D), ln_b.reshape(1, D))


def _pack_bf16_pair(lo, hi):
    lo_w = lax.bitcast_convert_type(lo.astype(BF16).astype(F32), jnp.uint32) >> 16
    hi_w = lax.bitcast_convert_type(hi.astype(BF16).astype(F32), jnp.uint32) & jnp.uint32(0xFFFF0000)
    return lo_w | hi_w


def _unpack_bf16_pair(w):
    lo = lax.bitcast_convert_type(w << 16, F32)
    hi = lax.bitcast_convert_type(w & jnp.uint32(0xFFFF0000), F32)
    return lo, hi


def _index_copy(idx_hbm, idx_smem, sems, tile):
    return pltpu.make_async_copy(idx_hbm.at[tile], idx_smem.at[tile % 2], sems.at[tile % 2])


def _combine_ln_kernel(dest_hbm, ys_hbm, gate4_ref, x_ref, mod_ref, g_ref, b_ref, o_ref,
                       idx_smem, buf_ref, rows2d_ref, idx_sems, row_sems, *, alpha):
    i = pl.program_id(0)
    n = pl.num_programs(0)

    def gather_rows(tile):
        slot = tile % 2

        def issue(r, carry):
            for k in range(TOP_K):
                row = idx_smem[slot, r * TOP_K + k]
                pltpu.make_async_copy(ys_hbm.at[row], buf_ref.at[slot, k, r],
                                      row_sems.at[slot]).start(priority=k % 2)
            return carry

        lax.fori_loop(0, TM, issue, 0, unroll=8)

    @pl.when(i == 0)
    def _():
        cp = _index_copy(dest_hbm, idx_smem, idx_sems, 0)
        cp.start()
        cp.wait()
        gather_rows(0)

        @pl.when(n > 1)
        def _():
            _index_copy(dest_hbm, idx_smem, idx_sems, 1).start()

    @pl.when(i + 1 < n)
    def _():
        _index_copy(dest_hbm, idx_smem, idx_sems, i + 1).wait()
        gather_rows(i + 1)

        @pl.when(i + 2 < n)
        def _():
            _index_copy(dest_hbm, idx_smem, idx_sems, i + 2).start()

    slot = i % 2
    for k in range(TOP_K):
        pltpu.make_async_copy(ys_hbm.at[pl.ds(0, TM)], buf_ref.at[slot, k], row_sems.at[slot]).wait()
    y_lo = y_hi = None
    for k in range(TOP_K):
        rows2d_ref[...] = buf_ref[slot, k].reshape(rows2d_ref.shape)
        lo, hi = _unpack_bf16_pair(rows2d_ref[...])
        gk = gate4_ref[:, k:k + 1]
        y_lo = lo * gk if y_lo is None else y_lo + lo * gk
        y_hi = hi * gk if y_hi is None else y_hi + hi * gk
    y = jnp.concatenate([y_lo, y_hi], axis=1)
    z = alpha * x_ref[...] + mod_ref[2:3, :] * y
    o_ref[...] = _layer_norm_rows(z, g_ref[...], b_ref[...])


def _combine_ln(dest_tiles, ys, gate4, x, mods, ls, ln_g, ln_b, *, alpha, geom, latent_only):
    D = x.shape[1]
    n_tiles, row_tile, mod_row = _token_tiles(geom, latent_only)
    return pl.pallas_call(
        functools.partial(_combine_ln_kernel, alpha=alpha),
        out_shape=jax.ShapeDtypeStruct((n_tiles * TM, D), F32),
        grid=(n_tiles,),
        in_specs=[
            pl.BlockSpec(memory_space=pl.ANY),
            pl.BlockSpec(memory_space=pl.ANY),
            pl.BlockSpec((TM, LANES), lambda i: (i, 0)),
            pl.BlockSpec((TM, D), lambda i: (row_tile(i), 0)),
            pl.BlockSpec((None, None, 3, D), lambda i: (ls, mod_row(i), 0, 0)),
            pl.BlockSpec((1, D), lambda i: (0, 0)),
            pl.BlockSpec((1, D), lambda i: (0, 0)),
        ],
        out_specs=pl.BlockSpec((TM, D), lambda i: (i, 0)),
        scratch_shapes=[
            pltpu.SMEM((2, TM * TOP_K), jnp.int32),
            pltpu.VMEM((2, TOP_K, TM, 1, D // 2), jnp.uint32),
            pltpu.VMEM((TM, D // 2), jnp.uint32),
            pltpu.SemaphoreType.DMA((2,)),
            pltpu.SemaphoreType.DMA((2,)),
        ],
        compiler_params=_cparams(("arbitrary",), VMEM_LIMIT),
        name="combine_ln",
    )(dest_tiles, ys, gate4, x, mods, ln_g.reshape(1, D), ln_b.reshape(1, D))


def _dispatch_kernel(pad_lo_ref, nused_ref, dest_hbm, x_ref, mod_ref, xs_hbm, idx_smem, a_ref, zero_ref,
                     idx_sems, row_sems, *, bm):
    i = pl.program_id(0)
    n = pl.num_programs(0)
    slot = i % 2

    def wait_rows(s):
        for _ in range(TOP_K):
            pltpu.make_async_copy(a_ref.at[s], xs_hbm.at[pl.ds(0, TM)], row_sems.at[s]).wait()

    @pl.when(i == 0)
    def _():
        _index_copy(dest_hbm, idx_smem, idx_sems, 0).start()
        zero_ref[...] = jnp.zeros_like(zero_ref)
        n_exp = pad_lo_ref.shape[0]
        n_blk = xs_hbm.shape[0] // bm

        def zero_block(row0):
            return pltpu.make_async_copy(zero_ref, xs_hbm.at[pl.ds(pl.multiple_of(row0, bm), bm)],
                                         row_sems.at[0])

        for e in range(n_exp):
            @pl.when(pad_lo_ref[e] >= 0)
            def _():
                zero_block(pad_lo_ref[e]).start()

        def fill_tail(b, carry):
            zero_block(b * bm).start()
            return carry

        lax.fori_loop(nused_ref[0], n_blk, fill_tail, 0)
        for e in range(n_exp):
            @pl.when(pad_lo_ref[e] >= 0)
            def _():
                zero_block(0).wait()

        def wait_tail(b, carry):
            zero_block(0).wait()
            return carry

        lax.fori_loop(nused_ref[0], n_blk, wait_tail, 0)

    @pl.when(i + 1 < n)
    def _():
        _index_copy(dest_hbm, idx_smem, idx_sems, i + 1).start()

    @pl.when(i >= 2)
    def _():
        wait_rows(slot)

    shift = mod_ref[0:1, :]
    scale = mod_ref[1:2, :]
    a = x_ref[...] * (1.0 + scale) + shift
    half = a.shape[1] // 2
    a_ref[slot] = _pack_bf16_pair(a[:, :half], a[:, half:]).reshape(a_ref.shape[1:])
    _index_copy(dest_hbm, idx_smem, idx_sems, i).wait()

    def issue(r, carry):
        for k in range(TOP_K):
            row = idx_smem[slot, r * TOP_K + k]
            pltpu.make_async_copy(a_ref.at[slot, r], xs_hbm.at[row], row_sems.at[slot]).start(priority=k % 2)
        return carry

    lax.fori_loop(0, TM, issue, 0, unroll=8)

    @pl.when(i == n - 1)
    def _():
        wait_rows(slot)

        @pl.when(n > 1)
        def _():
            wait_rows(1 - slot)


def _dispatch(pad_lo, n_used, dest_tiles, x, mods, ls, n_rows, *, geom, latent_only):
    D = x.shape[1]
    n_tiles, row_tile, mod_row = _token_tiles(geom, latent_only)
    bm = EXPERT_BM
    return pl.pallas_call(
        functools.partial(_dispatch_kernel, bm=bm),
        out_shape=jax.ShapeDtypeStruct((n_rows, 1, D // 2), jnp.uint32),
        grid_spec=pltpu.PrefetchScalarGridSpec(
            num_scalar_prefetch=2,
            grid=(n_tiles,),
            in_specs=[
                pl.BlockSpec(memory_space=pl.ANY),
                pl.BlockSpec((TM, D), lambda i, p, n: (row_tile(i), 0)),
                pl.BlockSpec((None, None, 3, D), lambda i, p, n: (ls, mod_row(i), 0, 0)),
            ],
            out_specs=pl.BlockSpec(memory_space=pl.ANY),
            scratch_shapes=[
                pltpu.SMEM((2, TM * TOP_K), jnp.int32),
                pltpu.VMEM((2, TM, 1, D // 2), jnp.uint32),
                pltpu.VMEM((bm, 1, D // 2), jnp.uint32),
                pltpu.SemaphoreType.DMA((2,)),
                pltpu.SemaphoreType.DMA((2,)),
            ],
        ),
        compiler_params=_cparams(("arbitrary",), VMEM_LIMIT),
        name="moe_dispatch",
    )(pad_lo, n_used, dest_tiles, x, mods)


def _sigmoid(x):
    return 0.5 * jnp.tanh(0.5 * x) + 0.5


def _softplus(x):
    return jnp.maximum(x, 0.0) + jnp.log(1.0 + jnp.exp(-jnp.abs(x)))


def _gelu_tanh(x):
    c = math.sqrt(2.0 / math.pi)
    return 0.5 * x * (1.0 + jnp.tanh(c * (x + 0.044715 * (x * x * x))))


def _group_scan(a, b, row, reverse):
    for s in (1, 2, 4):
        if reverse:
            keep = row < SUBLANES - s
            shift = SUBLANES - s
        else:
            keep = row >= s
            shift = s
        a_sh = jnp.where(keep, pltpu.roll(a, shift, axis=0), 1.0)
        b_sh = jnp.where(keep, pltpu.roll(b, shift, axis=0), 0.0)
        b = b + a * b_sh
        a = a * a_sh
    return a, b


def _lru_kernel(ug_ref, ur_ref, cw_ref, cb_ref, wg_ref, bg_ref, lam_ref, o_ref,
                a0_ref, b0_ref, a1_ref, b1_ref, *, ctx_len):
    T, W = ur_ref.shape
    CH = 256
    n_chunks = T // CH
    ctx_chunks = ctx_len // CH
    row_c = lax.broadcasted_iota(jnp.int32, (CH, W), 0)
    kexp = (-0.5 * LRU_C * math.log2(math.e)) * _softplus(-lam_ref[...])

    def coeff_chunk(c, carry):
        c0 = pl.multiple_of(c * CH, CH)
        cur = ur_ref[pl.ds(c0, CH), :]
        prev8 = ur_ref[pl.ds(pl.multiple_of(jnp.maximum(c0 - SUBLANES, 0), SUBLANES), SUBLANES), :]
        next8 = ur_ref[pl.ds(pl.multiple_of(jnp.minimum(c0 + CH, T - SUBLANES), SUBLANES), SUBLANES), :]
        seg_start = jnp.logical_or(c == 0, c == ctx_chunks)
        seg_end = jnp.logical_or(c == ctx_chunks - 1, c == n_chunks - 1)
        p_last = jnp.where(seg_start, 0.0, prev8[SUBLANES - 1:SUBLANES, :])
        n_first = jnp.where(seg_end, 0.0, next8[0:1, :])
        n_second = jnp.where(seg_end, 0.0, next8[1:2, :])
        u_m1 = jnp.where(row_c == 0, p_last, pltpu.roll(cur, 1, axis=0))
        u_p1 = jnp.where(row_c == CH - 1, n_first, pltpu.roll(cur, CH - 1, axis=0))
        u_p2 = jnp.where(row_c == CH - 2, n_first,
                         jnp.where(row_c == CH - 1, n_second, pltpu.roll(cur, CH - 2, axis=0)))
        r = cb_ref[...] + u_m1 * cw_ref[0:1, :]
        r = r + cur * cw_ref[1:2, :]
        r = r + u_p1 * cw_ref[2:3, :]
        r = r + u_p2 * cw_ref[3:4, :]
        gts = jnp.dot(r.astype(BF16), wg_ref[0].astype(BF16), preferred_element_type=F32) + bg_ref[0]
        half_r = 0.5 * r
        for d, (a_ref, b_ref) in enumerate(((a0_ref, b0_ref), (a1_ref, b1_ref))):
            t_rec = jnp.tanh(gts[:, (2 * d) * W:(2 * d + 1) * W])
            t_in = jnp.tanh(gts[:, (2 * d + 1) * W:(2 * d + 2) * W])
            kd = kexp[d:d + 1, :]
            a = jnp.exp2(kd * t_rec + kd)
            a_ref[pl.ds(c0, CH), :] = a
            t = jnp.maximum(1.0 - a * a, 0.0)
            b_ref[pl.ds(c0, CH), :] = (t * lax.rsqrt(jnp.maximum(t, SQRT_FLOOR))) * (half_r * t_in + half_r)
        return carry

    lax.fori_loop(0, n_chunks, coeff_chunk, 0)

    row8 = lax.broadcasted_iota(jnp.int32, (SUBLANES, W), 0)
    n_groups = T // SUBLANES
    ctx_groups = ctx_len // SUBLANES

    def scan_step(k, carry):
        hf, hb = carry
        gf = pl.multiple_of(k * SUBLANES, SUBLANES)
        a, b = _group_scan(a0_ref[pl.ds(gf, SUBLANES), :], b0_ref[pl.ds(gf, SUBLANES), :], row8, False)
        h = a * hf + b
        a0_ref[pl.ds(gf, SUBLANES), :] = h
        hf = jnp.broadcast_to(h[SUBLANES - 1:SUBLANES, :], h.shape)
        g = jnp.where(k < ctx_groups, ctx_groups - 1 - k, n_groups - 1 - (k - ctx_groups))
        gb = pl.multiple_of(g * SUBLANES, SUBLANES)
        a, b = _group_scan(a1_ref[pl.ds(gb, SUBLANES), :], b1_ref[pl.ds(gb, SUBLANES), :], row8, True)
        h = a * hb + b
        a1_ref[pl.ds(gb, SUBLANES), :] = h
        hb = jnp.broadcast_to(h[0:1, :], h.shape)
        return hf, hb

    zero = jnp.zeros((SUBLANES, W), F32)
    lax.fori_loop(0, n_groups, scan_step, (zero, zero), unroll=8)

    def out_chunk(c, carry):
        c0 = pl.multiple_of(c * CH, CH)
        y = a0_ref[pl.ds(c0, CH), :] + a1_ref[pl.ds(c0, CH), :]
        o_ref[pl.ds(c0, CH), :] = (y * _gelu_tanh(ug_ref[pl.ds(c0, CH), :])).astype(o_ref.dtype)
        return carry

    lax.fori_loop(0, n_chunks, out_chunk, 0)


def _lru_mix(u, conv_w, conv_b, gate_w, gate_b, lam, *, nbatch, ctx_len):
    M, W2 = u.shape
    W = W2 // 2
    T = M // nbatch
    nblk = W // LRU_BLOCK_W
    bw = LRU_BLOCK_W
    wg = 0.5 * jnp.transpose(gate_w, (2, 3, 0, 1, 4)).reshape(nblk, bw, 4 * bw)
    bg = 0.5 * jnp.transpose(gate_b.reshape(2, 2, nblk, bw), (2, 0, 1, 3)).reshape(nblk, 1, 4 * bw)
    return pl.pallas_call(
        functools.partial(_lru_kernel, ctx_len=ctx_len),
        out_shape=jax.ShapeDtypeStruct((M, W), BF16),
        grid=(nbatch, nblk),
        in_specs=[
            pl.BlockSpec((T, bw), lambda b, n: (b, n)),
            pl.BlockSpec((T, bw), lambda b, n: (b, nblk + n)),
            pl.BlockSpec((CONV_W, bw), lambda b, n: (0, n)),
            pl.BlockSpec((1, bw), lambda b, n: (0, n)),
            pl.BlockSpec((1, bw, 4 * bw), lambda b, n: (n, 0, 0)),
            pl.BlockSpec((1, 1, 4 * bw), lambda b, n: (n, 0, 0)),
            pl.BlockSpec((2, bw), lambda b, n: (0, n)),
        ],
        out_specs=pl.BlockSpec((T, bw), lambda b, n: (b, n)),
        scratch_shapes=[pltpu.VMEM((T, bw), F32) for _ in range(4)],
        compiler_params=_cparams(("parallel", "parallel"), VMEM_LIMIT),
        name="lru_mix",
    )(u, u, conv_w, conv_b.reshape(1, W), wg, bg, lam)


def _attn_kernel(sink_ref, q_ref, k_ref, v_ref, o_ref, k2_ref, v2_ref, p_ref, *, ctx_len, n_kv):
    QB = q_ref.shape[0]
    T = k_ref.shape[0]
    n_lat = (T - ctx_len) // QB
    ctx_blocks = ctx_len // QB
    band = 3 * QB
    pairs = GROUP // 2
    qb = pl.program_id(1)
    j = qb - ctx_blocks
    start = jnp.clip(j - 1, 0, n_lat - 3)
    off = pl.multiple_of(ctx_len + start * QB, QB)
    lo_half = lax.broadcasted_iota(jnp.int32, (1, LANES), 1) < HEAD_DIM
    nt = (((1,), (1,)), ((), ()))

    def attend(with_band):
        nk = ctx_len + (band if with_band else 0)
        if with_band:
            col = lax.broadcasted_iota(jnp.int32, (QB, nk), 1)
            qpos = j * QB + lax.broadcasted_iota(jnp.int32, (QB, nk), 0)
            kpos = start * QB + (col - ctx_len)
            valid = jnp.logical_or(col < ctx_len, jnp.abs(qpos - kpos) <= WINDOW)
        for h in range(n_kv):
            slab = slice((h // 2) * LANES, (h // 2 + 1) * LANES)

            def two_blocks(x):
                swapped = jnp.concatenate([x[:, HEAD_DIM:], x[:, :HEAD_DIM]], axis=1)
                own, other = (x, swapped) if h % 2 == 0 else (swapped, x)
                zero = jnp.zeros_like(x)
                return jnp.where(lo_half, own, zero), jnp.where(lo_half, zero, other)

            for dst, src in ((k2_ref, k_ref), (v2_ref, v_ref)):
                top_c, bot_c = two_blocks(src[0:ctx_len, slab])
                dst[0:ctx_len, :] = top_c
                dst[nk:nk + ctx_len, :] = bot_c
                if with_band:
                    top_b, bot_b = two_blocks(src[pl.ds(off, band), slab])
                    dst[ctx_len:nk, :] = top_b
                    dst[nk + ctx_len:2 * nk, :] = bot_b
            c_q = h * GROUP * HEAD_DIM
            qs = jnp.concatenate([q_ref[:, c_q + p * LANES:c_q + (p + 1) * LANES] for p in range(pairs)], axis=0)
            s = lax.dot_general(qs, k2_ref[0:2 * nk, :], nt, preferred_element_type=F32)
            invs = []
            for p in range(pairs):
                rows = slice(p * QB, (p + 1) * QB)
                inv_pair = []
                for half in range(2):
                    c0 = half * nk
                    sink = sink_ref[h * GROUP + 2 * p + half] * LOG2_E
                    s_h = s[rows, c0:c0 + nk]
                    if with_band:
                        s_h = jnp.where(valid, s_h, NEG_INF)
                    m = jnp.maximum(jnp.max(s_h, axis=-1, keepdims=True), sink)
                    p_h = jnp.exp2(s_h - m)
                    denom = jnp.sum(p_h, axis=-1, keepdims=True) + jnp.exp2(sink - m)
                    p_ref[rows, c0:c0 + nk] = p_h.astype(BF16)
                    inv_pair.append(1.0 / denom)
                invs.append(inv_pair)
            o = jnp.dot(p_ref[:, 0:2 * nk], v2_ref[0:2 * nk, :], preferred_element_type=F32)
            for p in range(pairs):
                inv = jnp.where(lo_half, invs[p][0], invs[p][1])
                o_ref[:, c_q + p * LANES:c_q + (p + 1) * LANES] = (o[p * QB:(p + 1) * QB] * inv).astype(o_ref.dtype)

    @pl.when(qb >= ctx_blocks)
    def _():
        attend(True)

    @pl.when(qb < ctx_blocks)
    def _():
        attend(False)


def _attention(qkv, sinks, *, nbatch, ctx_len, q_dim, kv_dim):
    M = qkv.shape[0]
    T = M // nbatch
    nqb = T // Q_BLOCK
    kcol = q_dim // kv_dim
    return pl.pallas_call(
        functools.partial(_attn_kernel, ctx_len=ctx_len, n_kv=kv_dim // HEAD_DIM),
        out_shape=jax.ShapeDtypeStruct((M, q_dim), BF16),
        grid_spec=pltpu.PrefetchScalarGridSpec(
            num_scalar_prefetch=1,
            grid=(nbatch, nqb),
            in_specs=[
                pl.BlockSpec((Q_BLOCK, q_dim), lambda b, q, s: (b * nqb + q, 0)),
                pl.BlockSpec((T, kv_dim), lambda b, q, s: (b, kcol)),
                pl.BlockSpec((T, kv_dim), lambda b, q, s: (b, kcol + 1)),
            ],
            out_specs=pl.BlockSpec((Q_BLOCK, q_dim), lambda b, q, s: (b * nqb + q, 0)),
            scratch_shapes=[
                pltpu.VMEM((2 * (ctx_len + 3 * Q_BLOCK), LANES), BF16),
                pltpu.VMEM((2 * (ctx_len + 3 * Q_BLOCK), LANES), BF16),
                pltpu.VMEM((GROUP // 2 * Q_BLOCK, 2 * (ctx_len + 3 * Q_BLOCK)), BF16),
            ],
        ),
        compiler_params=_cparams(("parallel", "parallel"), VMEM_LIMIT),
        name="swa_attention",
    )(sinks, qkv, qkv, qkv)


def _split_bf16(x):
    hi = x.astype(BF16)
    lo = (x - hi.astype(F32)).astype(BF16)
    return hi, lo


def _router_kernel(x_ref, mod_ref, wr_ref, br_ref, e_ref, g_ref, r_ref, cnt_ref, carry_ref):
    i = pl.program_id(0)

    @pl.when(i == 0)
    def _():
        carry_ref[...] = jnp.zeros_like(carry_ref)

    shift = mod_ref[0:1, :]
    scale = mod_ref[1:2, :]
    a = x_ref[...] * (1.0 + scale) + shift
    a_hi, a_lo = _split_bf16(a)
    ne = br_ref.shape[1]
    hi_terms = jnp.dot(a_hi, wr_ref[...], preferred_element_type=F32)
    lo_term = jnp.dot(a_lo, wr_ref[:, 0:ne], preferred_element_type=F32)
    logits = (hi_terms[:, 0:ne] + (lo_term + hi_terms[:, ne:2 * ne])) + br_ref[...]
    tm = logits.shape[0]
    lane_e = lax.broadcasted_iota(jnp.int32, (tm, ne), 1)
    lane_o = lax.broadcasted_iota(jnp.int32, (tm, LANES), 1)
    vals, idxs, hots = [], [], []
    cur = logits
    for k in range(TOP_K):
        m = jnp.max(cur, axis=-1, keepdims=True)
        idx = jnp.min(jnp.where(cur == m, lane_e, ne), axis=-1, keepdims=True)
        hot = lane_e == idx
        vals.append(m)
        idxs.append(idx)
        hots.append(hot)
        cur = jnp.where(hot, -jnp.inf, cur)
    exps = [jnp.exp(v - vals[0]) for v in vals]
    inv = 1.0 / (exps[0] + exps[1] + exps[2] + exps[3])
    msum = jnp.zeros((tm, ne), F32)
    for hot in hots:
        msum = msum + hot.astype(F32)
    rr = lax.broadcasted_iota(jnp.int32, (tm, tm), 0)
    cc = lax.broadcasted_iota(jnp.int32, (tm, tm), 1)
    tri = (cc < rr).astype(BF16)
    before = jnp.dot(tri, msum.astype(BF16), preferred_element_type=F32) + carry_ref[...]
    e_out = jnp.zeros((tm, LANES), jnp.int32)
    g_out = jnp.zeros((tm, LANES), F32)
    r_out = jnp.zeros((tm, LANES), jnp.int32)
    for k in range(TOP_K):
        rank = jnp.sum(jnp.where(hots[k], before, 0.0), axis=-1, keepdims=True).astype(jnp.int32)
        e_out = jnp.where(lane_o == k, idxs[k], e_out)
        g_out = jnp.where(lane_o == k, exps[k] * inv, g_out)
        r_out = jnp.where(lane_o == k, rank, r_out)
    e_ref[...] = e_out
    g_ref[...] = g_out
    r_ref[...] = r_out
    carry_ref[...] = carry_ref[...] + jnp.sum(msum, axis=0, keepdims=True)
    cnt_ref[...] = carry_ref[...].astype(jnp.int32)


def _router(x, mods, ls, w_r, b_r, *, geom, latent_only):
    D = x.shape[1]
    ne = w_r.shape[1]
    n_tiles, row_tile, mod_row = _token_tiles(geom, latent_only)
    M = n_tiles * TM
    return pl.pallas_call(
        _router_kernel,
        out_shape=(
            jax.ShapeDtypeStruct((M, LANES), jnp.int32),
            jax.ShapeDtypeStruct((M, LANES), F32),
            jax.ShapeDtypeStruct((M, LANES), jnp.int32),
            jax.ShapeDtypeStruct((1, ne), jnp.int32),
        ),
        grid=(n_tiles,),
        in_specs=[
            pl.BlockSpec((TM, D), lambda i: (row_tile(i), 0)),
            pl.BlockSpec((None, None, 3, D), lambda i: (ls, mod_row(i), 0, 0)),
            pl.BlockSpec((D, 2 * ne), lambda i: (0, 0)),
            pl.BlockSpec((1, ne), lambda i: (0, 0)),
        ],
        out_specs=(
            pl.BlockSpec((TM, LANES), lambda i: (i, 0)),
            pl.BlockSpec((TM, LANES), lambda i: (i, 0)),
            pl.BlockSpec((TM, LANES), lambda i: (i, 0)),
            pl.BlockSpec((1, ne), lambda i: (0, 0)),
        ),
        scratch_shapes=[pltpu.VMEM((1, ne), F32)],
        compiler_params=_cparams(("arbitrary",), VMEM_LIMIT),
        name="moe_router",
    )(x, mods, jnp.concatenate(_split_bf16(w_r), axis=1), b_r.reshape(1, ne))


def _expert_kernel(blk_e_ref, slot_ref, first_ref, next_e_ref, nused_ref, xs_ref, bgu_ref, bdn_ref,
                   wgu_hbm, wdn_hbm, o_ref, wgu_buf, wdn_buf, x2d_ref, sem_gu, sem_dn, *, layer):
    b = pl.program_id(0)
    de, d = wdn_buf.shape[1:]
    half = d // 2

    def fetch(expert, slot):
        cps = []
        for hbm, buf, sem, chunks in ((wgu_hbm, wgu_buf, sem_gu, WEIGHT_DMA_CHUNKS),
                                      (wdn_hbm, wdn_buf, sem_dn, WEIGHT_DMA_CHUNKS // 2)):
            rows = buf.shape[1] // chunks
            for c in range(chunks):
                rs = pl.ds(c * rows, rows)
                cps.append(pltpu.make_async_copy(hbm.at[layer, expert, rs], buf.at[slot, rs], sem.at[slot]))
        return cps

    @pl.when(b < nused_ref[0])
    def _():
        slot = slot_ref[b]

        @pl.when(b == 0)
        def _():
            for c, cp in enumerate(fetch(blk_e_ref[0], 0)):
                cp.start(priority=c % 2)

        @pl.when(first_ref[b] == 1)
        def _():
            @pl.when(next_e_ref[b] >= 0)
            def _():
                for c, cp in enumerate(fetch(next_e_ref[b], 1 - slot)):
                    cp.start(priority=c % 2)
            for cp in fetch(blk_e_ref[b], slot):
                cp.wait()

        x2d_ref[...] = xs_ref[...].reshape(x2d_ref.shape)
        x_lo, x_hi = (v.astype(BF16) for v in _unpack_bf16_pair(x2d_ref[...]))
        hs = []
        for c in range(2 * de // 256):
            sl = slice(c * 256, (c + 1) * 256)
            acc = jnp.dot(x_lo, wgu_buf[slot, 0:half, sl].astype(BF16), preferred_element_type=F32)
            acc = acc + jnp.dot(x_hi, wgu_buf[slot, half:d, sl].astype(BF16), preferred_element_type=F32)
            hs.append(acc + bgu_ref[:, sl])
        h = jnp.concatenate(hs, axis=1)
        g = jnp.minimum(h[:, :de], SWIGLU_LIMIT)
        u = jnp.clip(h[:, de:], -SWIGLU_LIMIT, SWIGLU_LIMIT)
        act = ((u + 1.0) * (g * _sigmoid(SWIGLU_ALPHA * g))).astype(BF16)
        ys = []
        for c in range(d // 512):
            sl = slice(c * 512, (c + 1) * 512)
            ys.append(jnp.dot(act, wdn_buf[slot, :, sl].astype(BF16), preferred_element_type=F32) + bdn_ref[:, sl])
        nh = len(ys) // 2
        for c in range(nh):
            packed = _pack_bf16_pair(ys[c], ys[c + nh])
            o_ref[:, :, c * 512:(c + 1) * 512] = packed.reshape(packed.shape[0], 1, 512)

    @pl.when(b >= nused_ref[0])
    def _():
        o_ref[...] = jnp.zeros_like(o_ref)


def _expert_ffn(xs, blk_tabs, n_used, layer, w_gu, b_gu, w_dn, b_dn):
    P = xs.shape[0]
    _, ne, D, de2 = w_gu.shape
    de = de2 // 2
    n_blk = P // EXPERT_BM
    blk_e, slot, first, next_e = blk_tabs
    return pl.pallas_call(
        functools.partial(_expert_kernel, layer=layer),
        out_shape=jax.ShapeDtypeStruct((P, 1, D // 2), jnp.uint32),
        grid_spec=pltpu.PrefetchScalarGridSpec(
            num_scalar_prefetch=5,
            grid=(n_blk,),
            in_specs=[
                pl.BlockSpec((EXPERT_BM, 1, D // 2), lambda b, e, s, f, x, n: (jnp.minimum(b, n[0] - 1), 0, 0)),
                pl.BlockSpec((None, None, 1, de2), lambda b, e, s, f, x, n: (layer, e[b], 0, 0)),
                pl.BlockSpec((None, None, 1, D), lambda b, e, s, f, x, n: (layer, e[b], 0, 0)),
                pl.BlockSpec(memory_space=pl.ANY),
                pl.BlockSpec(memory_space=pl.ANY),
            ],
            out_specs=pl.BlockSpec((EXPERT_BM, 1, D // 2), lambda b, e, s, f, x, n: (b, 0, 0)),
            scratch_shapes=[
                pltpu.VMEM((2, D, de2), F32),
                pltpu.VMEM((2, de, D), F32),
                pltpu.VMEM((EXPERT_BM, D // 2), jnp.uint32),
                pltpu.SemaphoreType.DMA((2,)),
                pltpu.SemaphoreType.DMA((2,)),
            ],
        ),
        compiler_params=_cparams(("arbitrary",), VMEM_LIMIT),
        name="expert_ffn",
    )(blk_e, slot, first, next_e, n_used, xs, b_gu.reshape(b_gu.shape[0], ne, 1, de2),
      b_dn.reshape(b_dn.shape[0], ne, 1, D), w_gu, w_dn)


def _moe_layer(x, mods, ls, layer, w_r, b_r, w_gu, b_gu, w_dn, b_dn, ln_g, ln_b, *, alpha, geom, latent_only):
    ne = w_r.shape[1]
    bm = EXPERT_BM
    e128, g128, r128, counts = _router(x, mods, ls, w_r, b_r, geom=geom, latent_only=latent_only)
    M = e128.shape[0]
    top_e = e128[:, :TOP_K]
    rank = r128[:, :TOP_K]
    counts = counts[0]
    padded = ((counts + bm - 1) // bm) * bm
    ends = jnp.cumsum(padded)
    pstart = ends - padded
    experts = jnp.arange(ne, dtype=jnp.int32)
    dest = jnp.sum(jnp.where(top_e[..., None] == experts, pstart, 0), axis=-1) + rank
    dest_tiles = dest.astype(jnp.int32).reshape(M // TM, TM * TOP_K)
    A = M * TOP_K
    P = (-(-A // bm)) * bm + ne * bm
    n_blk = P // bm
    n_used = (ends[-1] // bm).astype(jnp.int32).reshape(1)
    blk = jnp.arange(n_blk, dtype=jnp.int32)
    blk_row = jnp.minimum(blk, n_used[0] - 1) * bm
    blk_e = jnp.minimum(jnp.sum(ends[None, :] <= blk_row[:, None], axis=1), ne - 1).astype(jnp.int32)
    first = jnp.logical_and(blk < n_used[0],
                            jnp.concatenate([jnp.ones((1,), bool), blk_e[1:] != blk_e[:-1]])).astype(jnp.int32)
    slot = ((jnp.cumsum(first) - 1) % 2).astype(jnp.int32)
    later = jnp.logical_and(experts[None, :] > experts[:, None], (padded > 0)[None, :])
    next_nonempty = jnp.min(jnp.where(later, experts[None, :], ne), axis=1)
    next_nonempty = jnp.where(next_nonempty == ne, -1, next_nonempty)
    next_e = jnp.sum(jnp.where(blk_e[:, None] == experts[None, :], next_nonempty[None, :], 0), axis=1).astype(jnp.int32)
    pad_lo = jnp.where(padded > 0, ends - bm, -1).astype(jnp.int32)
    xs = _dispatch(pad_lo, n_used, dest_tiles, x, mods, ls, P, geom=geom, latent_only=latent_only)
    ys = _expert_ffn(xs, (blk_e, slot, first, next_e), n_used, layer, w_gu, b_gu, w_dn, b_dn)
    return _combine_ln(dest_tiles, ys, g128, x, mods, ls, ln_g, ln_b, alpha=alpha, geom=geom,
                       latent_only=latent_only)


def _rope_tables(seq, ctx_len):
    rows = seq // GRID_W
    row_pos = jnp.repeat(jnp.arange(rows), GRID_W).astype(F32)
    col_pos = jnp.tile(jnp.arange(GRID_W), rows).astype(F32)
    n = HEAD_DIM // 2
    freqs = ROPE_BASE ** (-jnp.arange(0, n, 2, dtype=F32) / n)
    ang_r = row_pos[:, None] * freqs[None, :]
    ang_c = col_pos[:, None] * freqs[None, :]
    cos = jnp.concatenate([jnp.cos(ang_r), jnp.cos(ang_r), jnp.cos(ang_c), jnp.cos(ang_c)], axis=1)
    sin = jnp.concatenate([-jnp.sin(ang_r), jnp.sin(ang_r), -jnp.sin(ang_c), jnp.sin(ang_c)], axis=1)
    cos = jnp.concatenate([jnp.ones((ctx_len, HEAD_DIM), F32), cos], axis=0)
    sin = jnp.concatenate([jnp.zeros((ctx_len, HEAD_DIM), F32), sin], axis=0)
    cos = jnp.tile(cos, (1, LANES // HEAD_DIM))
    sin = jnp.tile(sin, (1, LANES // HEAD_DIM))
    scale = HEAD_DIM ** -0.5 * LOG2_E
    return jnp.stack([cos * scale, sin * scale, cos, sin], axis=0)


def kernel(x, c, ctx, c_ctx, ada_w, ada_b, ln_g, ln_b, lru_w_in, lru_conv_w, lru_conv_b, lru_gate_w, lru_gate_b, lru_lambda, lru_w_out, attn_w_qkv, attn_sinks, attn_w_o, router_w, router_b, moe_w_gu, moe_b_gu, moe_w_down, moe_b_down):
    B, S, D = x.shape
    C = ctx.shape[1]
    depth = ada_w.shape[0]
    T = C + S
    assert C % TM == 0 and S % TM == 0 and C % Q_BLOCK == 0
    geom = (T // TM, C // TM, B)
    alpha = (2.0 * depth) ** 0.25
    kv_dim = (attn_w_qkv.shape[2] - D) // 2

    act = jax.nn.silu(jnp.concatenate([c, c_ctx[None, :]], axis=0))
    mods = _ada_params(act, ada_w.reshape(depth * 2, D, 3 * D), ada_b.reshape(depth * 2, 3 * D))
    mods = mods.reshape(depth * 2, SUBLANES, 3, D)
    tab = _rope_tables(S, C)

    z = (ctx.reshape(B * C, D), x.reshape(B * S, D))
    for i in range(depth):
        j = i // 2
        if i % 2 == 0:
            u = _mod_matmul(z, mods, 2 * i, lru_w_in, j, tn=D, out_dtype=F32, geom=geom)
            y = _lru_mix(u, lru_conv_w[j], lru_conv_b[j], lru_gate_w[j], lru_gate_b[j], lru_lambda[j],
                         nbatch=B, ctx_len=C)
            w_out = lru_w_out
        else:
            qkv = _mod_matmul(z, mods, 2 * i, attn_w_qkv, j, tn=D + 2 * kv_dim, out_dtype=BF16,
                              geom=geom, rope=(tab, (D, kv_dim)))
            y = _attention(qkv, attn_sinks[j], nbatch=B, ctx_len=C, q_dim=D, kv_dim=kv_dim)
            w_out = attn_w_o
        z = _proj_ln(y, w_out, j, z, mods, 2 * i, ln_g[i, 0], ln_b[i, 0], alpha=alpha, geom=geom)
        z = _moe_layer(z, mods, 2 * i + 1, i, router_w[i], router_b[i], moe_w_gu, moe_b_gu,
                       moe_w_down, moe_b_down, ln_g[i, 1], ln_b[i, 1], alpha=alpha, geom=geom,
                       latent_only=(i == depth - 1))
    return z.reshape(B, S, D)
```
